```python
import math
import jax, jax.numpy as jnp
from jax import lax
import numpy as np

D_MODEL = 1024
BATCH = 8
SEQ = 2048
DEPTH = 2
DEC_BATCH = 16
DEC_SEQ = 32
PAST_LEN = 2048

CHUNK = 64
D_PLE = 256
HEAD_DIM = 64
N_A_LAYERS = DEPTH // 2
N_B_LAYERS = DEPTH - N_A_LAYERS
A_HEADS = D_MODEL // HEAD_DIM
SB_QBLOCK = 128
B_Q_HEADS = D_MODEL // HEAD_DIM
B_KV_HEADS = B_Q_HEADS // 8
B_GROUP = B_Q_HEADS // B_KV_HEADS
WINDOW = 128
WIN_CHUNKS = WINDOW // CHUNK
NUM_BUCKETS = 32
MAX_DISTANCE = 128
N_GROUPS = 4
EXPERTS_PER_GROUP = 4
N_EXPERTS = N_GROUPS * EXPERTS_PER_GROUP
TOP_K_IN_GROUP = 2
D_EXPERT = D_MODEL // 4
EPS = 1e-6

kernel_name = "yoco_stickbreak_swa_sink_hmoe_stream_step"


def rms_norm(x, g):
    xf = x.astype(jnp.float32)
    y = xf * lax.rsqrt(jnp.mean(xf * xf, axis=-1, keepdims=True) + EPS)
    return (y * g.astype(jnp.float32)).astype(x.dtype)


def split_heads(t, n):
    b, s, _ = t.shape
    return t.reshape(b, s, n, -1).transpose(0, 2, 1, 3)


def merge_heads(t):
    b, n, s, d = t.shape
    return t.transpose(0, 2, 1, 3).reshape(b, s, n * d)


def stick_breaking_block(q, k, v, q_pos, k_pos):
    z = jnp.einsum("bhqd,bhkd->bhqk", q.astype(jnp.float32), k.astype(jnp.float32)) * (HEAD_DIM ** -0.5)
    mask = k_pos[None, :] < q_pos[:, None]
    log_1m = jnp.where(mask, jax.nn.log_sigmoid(-z), 0.0)
    tail = lax.cumsum(log_1m, axis=3, reverse=True) - log_1m
    w = jnp.where(mask, jnp.exp(jax.nn.log_sigmoid(z) + tail), 0.0)
    return jnp.einsum("bhqk,bhkd->bhqd", w, v.astype(jnp.float32)).astype(v.dtype)


def stick_breaking_prompt(q, k, v):
    t = q.shape[2]
    pos = jnp.arange(t, dtype=jnp.int32)
    outs = []
    for i in range(t // SB_QBLOCK):
        lo, hi = i * SB_QBLOCK, (i + 1) * SB_QBLOCK
        outs.append(stick_breaking_block(q[:, :, lo:hi], k[:, :, :hi], v[:, :, :hi], pos[lo:hi], pos[:hi]))
    return jnp.concatenate(outs, axis=2)


def t5_bucket(rel):
    nb = NUM_BUCKETS // 2
    max_exact = nb // 2
    n = jnp.abs(rel)
    large = max_exact + (jnp.log(jnp.maximum(n, 1).astype(jnp.float32) / max_exact)
                         / math.log(MAX_DISTANCE / max_exact) * (nb - max_exact)).astype(jnp.int32)
    large = jnp.minimum(large, nb - 1)
    return jnp.where(rel > 0, nb, 0) + jnp.where(n < max_exact, n, large)


def swa_attend(q, k, v, q_pos, k_pos, rel_bias, sinks):
    nb, tq = q_pos.shape
    tk = k_pos.shape[1]
    logits = jnp.einsum("bhgnqd,bhnkd->bhgnqk", q.astype(jnp.float32), k.astype(jnp.float32)) * (HEAD_DIM ** -0.5)
    qc = q_pos[:, :, None] // CHUNK
    kc = k_pos[:, None, :] // CHUNK
    valid = (k_pos[:, None, :] >= 0) & (kc <= qc) & (kc >= qc - WIN_CHUNKS)
    bias = rel_bias[t5_bucket(k_pos[:, None, :] - q_pos[:, :, None])]
    bias = bias.astype(jnp.float32).transpose(3, 0, 1, 2).reshape(B_KV_HEADS, B_GROUP, nb, tq, tk)
    logits = jnp.where(valid, logits + bias, -jnp.inf)
    sink = sinks.astype(jnp.float32).reshape(B_KV_HEADS, B_GROUP, 1, 1, 1)
    m = jnp.maximum(jnp.max(logits, axis=-1, keepdims=True), sink)
    e = jnp.exp(logits - m)
    w = e / (jnp.sum(e, axis=-1, keepdims=True) + jnp.exp(sink - m))
    return jnp.einsum("bhgnqk,bhnkd->bhgnqd", w, v.astype(jnp.float32)).astype(v.dtype)


def swa_prompt(q, k, v, rel_bias, sinks):
    b, hq, t, dh = q.shape
    nc = t // CHUNK
    qb = q.reshape(b, B_KV_HEADS, B_GROUP, nc, CHUNK, dh)
    pad = ((0, 0), (0, 0), (WIN_CHUNKS, 0), (0, 0), (0, 0))
    kp = jnp.pad(k.reshape(b, B_KV_HEADS, nc, CHUNK, dh), pad)
    vp = jnp.pad(v.reshape(b, B_KV_HEADS, nc, CHUNK, dh), pad)
    kband = jnp.concatenate([kp[:, :, w:w + nc] for w in range(WIN_CHUNKS + 1)], axis=3)
    vband = jnp.concatenate([vp[:, :, w:w + nc] for w in range(WIN_CHUNKS + 1)], axis=3)
    q_pos = jnp.arange(t, dtype=jnp.int32).reshape(nc, CHUNK)
    k_pos = ((jnp.arange(nc, dtype=jnp.int32)[:, None] - WIN_CHUNKS) * CHUNK
             + jnp.arange((WIN_CHUNKS + 1) * CHUNK, dtype=jnp.int32)[None, :])
    o = swa_attend(qb, kband, vband, q_pos, k_pos, rel_bias, sinks)
    return o.reshape(b, hq, t, dh)


def swa_sample(q, k_win, v_win, past_len, rel_bias, sinks):
    b, hq, s, dh = q.shape
    tk = k_win.shape[2]
    qg = q.reshape(b, B_KV_HEADS, B_GROUP, 1, s, dh)
    q_pos = (past_len + jnp.arange(s, dtype=jnp.int32))[None]
    k_pos = (past_len + s - tk + jnp.arange(tk, dtype=jnp.int32))[None]
    o = swa_attend(qg, k_win[:, :, None], v_win[:, :, None], q_pos, k_pos, rel_bias, sinks)
    return o.reshape(b, hq, s, dh)


def hier_moe(x, w_group, w_router, w_gate, w_up, w_down):
    b, t, d = x.shape
    xt = x.reshape(b * t, d)
    n = xt.shape[0]
    g_prob = jax.nn.softmax((xt @ w_group).astype(jnp.float32), axis=-1)
    g_top, g_idx = lax.top_k(g_prob, 1)
    e_logits = jnp.einsum("nd,dge->nge", xt, w_router).astype(jnp.float32)
    e_sel = e_logits[jnp.arange(n), g_idx[:, 0]]
    e_top, e_idx = lax.top_k(e_sel, TOP_K_IN_GROUP)
    w_pair = jax.nn.softmax(e_top, axis=-1) * g_top
    expert_id = g_idx * EXPERTS_PER_GROUP + e_idx
    gates = jnp.einsum("nk,nke->ne", w_pair,
                       jax.nn.one_hot(expert_id, N_EXPERTS, dtype=jnp.float32)).astype(x.dtype)
    hid = jax.nn.silu(jnp.einsum("nd,edf->nef", xt, w_gate)) * jnp.einsum("nd,edf->nef", xt, w_up)
    y = jnp.einsum("nef,efd->nd", hid * gates[:, :, None], w_down)
    return y.reshape(b, t, d)


def channel_and_ple(h, p_i, i, prm):
    h = h + hier_moe(rms_norm(h, prm["norm_ffn"][i]), prm["moe_w_group"][i], prm["moe_w_router"][i],
                     prm["moe_w_gate"][i], prm["moe_w_up"][i], prm["moe_w_down"][i])
    gate = jax.nn.sigmoid(rms_norm(h, prm["norm_ple"][i]) @ prm["ple_w_gate"][i])
    return h + (p_i @ prm["ple_w_proj"][i]) * gate


def run_trunk(x, p, prm, sb_past_k=None, sb_past_v=None, swa_past_k=None, swa_past_v=None):
    t = x.shape[1]
    past_len = 0 if sb_past_k is None else sb_past_k.shape[3]
    h = x
    sb_k_rows, sb_v_rows = [], []
    k_win = v_win = None
    for i in range(DEPTH):
        a = rms_norm(h, prm["norm_attn"][i])
        if i < N_A_LAYERS:
            q, k, v = jnp.split(a @ prm["a_w_qkv"][i], 3, axis=-1)
            q, k, v = split_heads(q, A_HEADS), split_heads(k, A_HEADS), split_heads(v, A_HEADS)
            sb_k_rows.append(k)
            sb_v_rows.append(v)
            if sb_past_k is None:
                o = stick_breaking_prompt(q, k, v)
            else:
                kk = jnp.concatenate([sb_past_k[i], k], axis=2)
                vv = jnp.concatenate([sb_past_v[i], v], axis=2)
                o = stick_breaking_block(q, kk, vv, past_len + jnp.arange(t, dtype=jnp.int32),
                                         jnp.arange(past_len + t, dtype=jnp.int32))
            h = h + merge_heads(o) @ prm["a_w_o"][i]
        else:
            j = i - N_A_LAYERS
            q = rms_norm(split_heads(a @ prm["b_w_q"][j], B_Q_HEADS), prm["b_q_norm"][j])
            if swa_past_k is None:
                o = swa_prompt(q, k_win, v_win, prm["rel_bias"], prm["b_sinks"][j])
            else:
                o = swa_sample(q, k_win, v_win, past_len, prm["rel_bias"], prm["b_sinks"][j])
            h = h + merge_heads(o) @ prm["b_w_o"][j]
        h = channel_and_ple(h, p[i], i, prm)
        if i == N_A_LAYERS - 1:
            k_s, v_s = jnp.split(rms_norm(h, prm["kv_norm"]) @ prm["b_w_kv"], 2, axis=-1)
            k_s = rms_norm(split_heads(k_s, B_KV_HEADS), prm["b_k_norm"])
            v_s = split_heads(v_s, B_KV_HEADS)
            if swa_past_k is None:
                k_win, v_win = k_s, v_s
            else:
                k_win = jnp.concatenate([swa_past_k, k_s], axis=2)
                v_win = jnp.concatenate([swa_past_v, v_s], axis=2)
    return h, jnp.stack(sb_k_rows), jnp.stack(sb_v_rows), k_win[:, :, -WINDOW:], v_win[:, :, -WINDOW:]


def setup_inputs(seed: int = 0) -> dict:
    key = jax.random.key(seed)
    ks = jax.random.split(key, 32)

    def nrm(k, shape, scale):
        return jax.random.normal(k, shape, dtype=jnp.float32) * scale

    def gain(k, shape):
        return 1.0 + nrm(k, shape, 0.02)

    hd_a = A_HEADS * HEAD_DIM
    hd_b = B_Q_HEADS * HEAD_DIM
    kvd = B_KV_HEADS * HEAD_DIM
    return {
        "x_prompt": nrm(ks[0], (BATCH, SEQ, D_MODEL), 1.0),
        "x_sample": nrm(ks[1], (DEC_BATCH, DEC_SEQ, D_MODEL), 1.0),
        "p_prompt": nrm(ks[2], (DEPTH, BATCH, SEQ, D_PLE), 1.0),
        "p_sample": nrm(ks[3], (DEPTH, DEC_BATCH, DEC_SEQ, D_PLE), 1.0),
        "cache_sb_k": nrm(ks[4], (N_A_LAYERS, DEC_BATCH, A_HEADS, PAST_LEN, HEAD_DIM), 1.0),
        "cache_sb_v": nrm(ks[5], (N_A_LAYERS, DEC_BATCH, A_HEADS, PAST_LEN, HEAD_DIM), 1.0),
        "cache_swa_k": nrm(ks[6], (DEC_BATCH, B_KV_HEADS, WINDOW, HEAD_DIM), 1.0),
        "cache_swa_v": nrm(ks[7], (DEC_BATCH, B_KV_HEADS, WINDOW, HEAD_DIM), 1.0),
        "norm_attn": gain(ks[8], (DEPTH, D_MODEL)),
        "norm_ffn": gain(ks[9], (DEPTH, D_MODEL)),
        "norm_ple": gain(ks[10], (DEPTH, D_MODEL)),
        "a_w_qkv": nrm(ks[11], (N_A_LAYERS, D_MODEL, 3 * hd_a), D_MODEL ** -0.5),
        "a_w_o": nrm(ks[12], (N_A_LAYERS, hd_a, D_MODEL), hd_a ** -0.5),
        "kv_norm": gain(ks[13], (D_MODEL,)),
        "b_w_kv": nrm(ks[14], (D_MODEL, 2 * kvd), D_MODEL ** -0.5),
        "b_k_norm": gain(ks[15], (HEAD_DIM,)),
        "b_w_q": nrm(ks[16], (N_B_LAYERS, D_MODEL, hd_b), D_MODEL ** -0.5),
        "b_q_norm": gain(ks[17], (N_B_LAYERS, HEAD_DIM)),
        "b_sinks": nrm(ks[18], (N_B_LAYERS, B_Q_HEADS), 0.5),
        "b_w_o": nrm(ks[19], (N_B_LAYERS, hd_b, D_MODEL), hd_b ** -0.5),
        "rel_bias": nrm(ks[20], (NUM_BUCKETS, B_Q_HEADS), 0.5),
        "moe_w_group": nrm(ks[21], (DEPTH, D_MODEL, N_GROUPS), D_MODEL ** -0.5),
        "moe_w_router": nrm(ks[22], (DEPTH, D_MODEL, N_GROUPS, EXPERTS_PER_GROUP), D_MODEL ** -0.5),
        "moe_w_gate": nrm(ks[23], (DEPTH, N_EXPERTS, D_MODEL, D_EXPERT), D_MODEL ** -0.5),
        "moe_w_up": nrm(ks[24], (DEPTH, N_EXPERTS, D_MODEL, D_EXPERT), D_MODEL ** -0.5),
        "moe_w_down": nrm(ks[25], (DEPTH, N_EXPERTS, D_EXPERT, D_MODEL), D_EXPERT ** -0.5),
        "ple_w_proj": nrm(ks[26], (DEPTH, D_PLE, D_MODEL), D_PLE ** -0.5),
        "ple_w_gate": nrm(ks[27], (DEPTH, D_MODEL, D_MODEL), D_MODEL ** -0.5),
    }


def reference(x_prompt, x_sample, p_prompt, p_sample, cache_sb_k, cache_sb_v, cache_swa_k, cache_swa_v,
              norm_attn, norm_ffn, norm_ple, a_w_qkv, a_w_o, kv_norm, b_w_kv, b_k_norm, b_w_q, b_q_norm,
              b_sinks, b_w_o, rel_bias, moe_w_group, moe_w_router, moe_w_gate, moe_w_up, moe_w_down,
              ple_w_proj, ple_w_gate):
    prm = {
        "norm_attn": norm_attn, "norm_ffn": norm_ffn, "norm_ple": norm_ple,
        "a_w_qkv": a_w_qkv, "a_w_o": a_w_o, "kv_norm": kv_norm, "b_w_kv": b_w_kv, "b_k_norm": b_k_norm,
        "b_w_q": b_w_q, "b_q_norm": b_q_norm, "b_sinks": b_sinks, "b_w_o": b_w_o, "rel_bias": rel_bias,
        "moe_w_group": moe_w_group, "moe_w_router": moe_w_router, "moe_w_gate": moe_w_gate,
        "moe_w_up": moe_w_up, "moe_w_down": moe_w_down, "ple_w_proj": ple_w_proj, "ple_w_gate": ple_w_gate,
    }
    y_prompt, sb_k_p, sb_v_p, swa_k_p, swa_v_p = run_trunk(x_prompt, p_prompt, prm)
    y_sample, sb_k_s, sb_v_s, swa_k_s, swa_v_s = run_trunk(x_sample, p_sample, prm, cache_sb_k, cache_sb_v,
                                                           cache_swa_k, cache_swa_v)
    return (y_prompt, y_sample, sb_k_p, sb_v_p, swa_k_p, swa_v_p, sb_k_s, sb_v_s, swa_k_s, swa_v_s)
```

```python
import functools
import math

import jax
import jax.numpy as jnp
from jax import lax
from jax.experimental import pallas as pl
from jax.experimental.pallas import tpu as pltpu

F32 = jnp.float32
BF16 = jnp.bfloat16

HEAD_DIM = 64
CHUNK = 64
WINDOW = 128
WIN_CHUNKS = WINDOW // CHUNK
NUM_BUCKETS = 32
MAX_DISTANCE = 128
N_GROUPS = 4
EXPERTS_PER_GROUP = 4
N_EXPERTS = N_GROUPS * EXPERTS_PER_GROUP
EPS = 1e-6
SB_BLOCK = 128
ROUTER_LANES = 128
VMEM_LIMIT = 56 * 1024 * 1024

_NT = (((1,), (1,)), ((), ()))


def _cparams(sem):
    return pltpu.CompilerParams(dimension_semantics=sem, vmem_limit_bytes=VMEM_LIMIT)


def _rms_unit(x):
    return x * lax.rsqrt(jnp.mean(x * x, axis=-1, keepdims=True) + EPS)


def _split_hl(a):
    hi = a.astype(BF16)
    lo = (a - hi.astype(F32)).astype(BF16)
    return jnp.concatenate([hi, lo], axis=-1)


def _resident(shape):
    nd = len(shape)
    return pl.BlockSpec(shape, lambda *_: (0,) * nd, pipeline_mode=pl.Buffered(1))


def _proj_a_kernel(x_ref, g_ref, w_ref, q_ref, k_ref, v_ref, *, n_heads):
    xn = (_rms_unit(x_ref[0]) * g_ref[...]).astype(BF16)
    heads_per_dot = 4
    width = heads_per_dot * HEAD_DIM
    for c in range(3 * n_heads // heads_per_dot):
        r = jnp.dot(xn, w_ref[:, c * width:(c + 1) * width], preferred_element_type=F32)
        for hh in range(heads_per_dot):
            h = c * heads_per_dot + hh
            piece = r[:, hh * HEAD_DIM:(hh + 1) * HEAD_DIM]
            if h < n_heads:
                q_ref[0, h] = piece.astype(BF16)
            elif h < 2 * n_heads:
                k_ref[0, h - n_heads] = piece
            else:
                v_ref[0, h - 2 * n_heads] = piece


def _proj_a(x, g, w_qkv):
    bx, t, d = x.shape
    n_heads = w_qkv.shape[1] // (3 * HEAD_DIM)
    tm = min(t, 256)
    hm = lambda b, i: (b, 0, i, 0)
    out_block = pl.BlockSpec((1, n_heads, tm, HEAD_DIM), hm)
    return pl.pallas_call(
        functools.partial(_proj_a_kernel, n_heads=n_heads),
        grid=(bx, t // tm),
        in_specs=[pl.BlockSpec((1, tm, d), lambda b, i: (b, i, 0)),
                  _resident((1, d)), _resident(w_qkv.shape)],
        out_specs=[out_block, out_block, out_block],
        out_shape=[jax.ShapeDtypeStruct((bx, n_heads, t, HEAD_DIM), BF16),
                   jax.ShapeDtypeStruct((bx, n_heads, t, HEAD_DIM), F32),
                   jax.ShapeDtypeStruct((bx, n_heads, t, HEAD_DIM), F32)],
        compiler_params=_cparams(("parallel", "parallel")),
        name="proj_a",
    )(x, g, w_qkv)


def _softplus(z):
    return jnp.maximum(z, 0.0) + jnp.log1p(jnp.exp(-jnp.abs(z)))


def _suffix_ones(n):
    r = lax.broadcasted_iota(jnp.int32, (2 * n, n), 0)
    c = lax.broadcasted_iota(jnp.int32, (2 * n, n), 1)
    r = jnp.where(r >= n, r - n, r)
    return jnp.where(r >= c, 1.0, 0.0).astype(BF16)


def _sb_block(q, k, v, carry, acc, uu, mask):
    z = lax.dot_general(q, k, _NT, preferred_element_type=F32)
    log1m = -_softplus(z)
    if mask is not None:
        log1m = jnp.where(mask, log1m, 0.0)
    suffix = jnp.dot(_split_hl(log1m), uu, preferred_element_type=F32)
    w = jnp.exp(z + suffix + carry)
    if mask is not None:
        w = jnp.where(mask, w, 0.0)
    acc = acc + jnp.dot(w.astype(BF16), v, preferred_element_type=F32)
    return carry + suffix[:, 0:1], acc


def _sb_prompt_kernel(q_ref, k_ref, v_ref, o_ref, *, heads):
    t = q_ref.shape[2]
    blk = SB_BLOCK
    uu = _suffix_ones(blk)
    row = lax.broadcasted_iota(jnp.int32, (blk, blk), 0)
    col = lax.broadcasted_iota(jnp.int32, (blk, blk), 1)
    strict = col < row

    def q_block(i, _):
        q0 = pl.multiple_of(i * blk, blk)
        qs = [q_ref[0, h, pl.ds(q0, blk), :] for h in range(heads)]
        state = []
        for h in range(heads):
            kd = k_ref[0, h, pl.ds(q0, blk), :].astype(BF16)
            vd = v_ref[0, h, pl.ds(q0, blk), :].astype(BF16)
            state.extend(_sb_block(qs[h], kd, vd, jnp.zeros((blk, 1), F32),
                                   jnp.zeros((blk, HEAD_DIM), F32), uu, strict))

        def k_block(jj, st):
            k0 = pl.multiple_of((i - 1 - jj) * blk, blk)
            new = []
            for h in range(heads):
                kb = k_ref[0, h, pl.ds(k0, blk), :].astype(BF16)
                vb = v_ref[0, h, pl.ds(k0, blk), :].astype(BF16)
                new.extend(_sb_block(qs[h], kb, vb, st[2 * h], st[2 * h + 1], uu, None))
            return tuple(new)

        state = lax.fori_loop(0, i, k_block, tuple(state))
        o_ref[0, pl.ds(q0, blk), :] = jnp.concatenate(
            [state[2 * h + 1] for h in range(heads)], axis=-1).astype(BF16)
        return 0

    lax.fori_loop(0, t // blk, q_block, 0)


def _sb_prompt(q, k, v):
    b, n_heads, t, _ = q.shape
    heads = 2
    blk = pl.BlockSpec((1, heads, t, HEAD_DIM), lambda bi, hp: (bi, hp, 0, 0))
    return pl.pallas_call(
        functools.partial(_sb_prompt_kernel, heads=heads),
        grid=(b, n_heads // heads),
        in_specs=[blk, blk, blk],
        out_specs=pl.BlockSpec((1, t, heads * HEAD_DIM), lambda bi, hp: (bi, 0, hp)),
        out_shape=jax.ShapeDtypeStruct((b, t, n_heads * HEAD_DIM), BF16),
        compiler_params=_cparams(("parallel", "parallel")),
        name="sb_prompt",
    )(q, k, v)


def _sb_sample_kernel(q_ref, kn_ref, vn_ref, kc_ref, vc_ref, o_ref, *, heads):
    s = q_ref.shape[2]
    past = kc_ref.shape[3]
    blk = SB_BLOCK
    uu = _suffix_ones(blk)
    uu_new = _suffix_ones(s)
    row = lax.broadcasted_iota(jnp.int32, (s, s), 0)
    col = lax.broadcasted_iota(jnp.int32, (s, s), 1)
    strict = col < row

    qs = [q_ref[0, h] for h in range(heads)]
    state = []
    for h in range(heads):
        state.extend(_sb_block(qs[h], kn_ref[0, h].astype(BF16), vn_ref[0, h].astype(BF16),
                               jnp.zeros((s, 1), F32), jnp.zeros((s, HEAD_DIM), F32), uu_new, strict))

    def k_block(jj, st):
        k0 = pl.multiple_of(past - (jj + 1) * blk, blk)
        new = []
        for h in range(heads):
            kb = kc_ref[0, 0, h, pl.ds(k0, blk), :].astype(BF16)
            vb = vc_ref[0, 0, h, pl.ds(k0, blk), :].astype(BF16)
            new.extend(_sb_block(qs[h], kb, vb, st[2 * h], st[2 * h + 1], uu, None))
        return tuple(new)

    state = lax.fori_loop(0, past // blk, k_block, tuple(state))
    o_ref[0] = jnp.concatenate([state[2 * h + 1] for h in range(heads)], axis=-1).astype(BF16)


def _sb_sample(q, k_new, v_new, cache_k, cache_v, layer):
    b, n_heads, s, _ = q.shape
    past = cache_k.shape[3]
    heads = 4
    new_blk = pl.BlockSpec((1, heads, s, HEAD_DIM), lambda bi, hg: (bi, hg, 0, 0))
    cache_blk = pl.BlockSpec((1, 1, heads, past, HEAD_DIM), lambda bi, hg: (layer, bi, hg, 0, 0))
    return pl.pallas_call(
        functools.partial(_sb_sample_kernel, heads=heads),
        grid=(b, n_heads // heads),
        in_specs=[new_blk, new_blk, new_blk, cache_blk, cache_blk],
        out_specs=pl.BlockSpec((1, s, heads * HEAD_DIM), lambda bi, hg: (bi, 0, hg)),
        out_shape=jax.ShapeDtypeStruct((b, s, n_heads * HEAD_DIM), BF16),
        compiler_params=_cparams(("parallel", "parallel")),
        name="sb_sample",
    )(q, k_new, v_new, cache_k, cache_v)


def _route(logits):
    tm = logits.shape[0]
    lane = lax.broadcasted_iota(jnp.int32, (tm, ROUTER_LANES), 1)
    lane_f = lane.astype(F32)
    neg = -jnp.inf
    first = lambda hit: jnp.min(jnp.where(hit, lane_f, float(ROUTER_LANES)), axis=-1, keepdims=True)

    gl = jnp.where(lane < N_GROUPS, logits, neg)
    g_max = jnp.max(gl, axis=-1, keepdims=True)
    g_top = 1.0 / jnp.sum(jnp.exp(gl - g_max), axis=-1, keepdims=True)
    g_idx = first(gl == g_max)

    lo = N_GROUPS + g_idx * EXPERTS_PER_GROUP
    in_group = (lane_f >= lo) & (lane_f < lo + EXPERTS_PER_GROUP)
    sel = jnp.where(in_group, logits, neg)
    t1 = jnp.max(sel, axis=-1, keepdims=True)
    i1 = first(sel == t1)
    sel2 = jnp.where(lane_f == i1, neg, sel)
    t2 = jnp.max(sel2, axis=-1, keepdims=True)
    i2 = first(sel2 == t2)
    e2 = jnp.exp(t2 - t1)
    den = 1.0 + e2
    w1 = (1.0 / den) * g_top
    w2 = (e2 / den) * g_top
    return jnp.where(lane_f == i1, w1, 0.0) + jnp.where(lane_f == i2, w2, 0.0)


def _channel_kernel(x_ref, o_ref, p_ref, wo_ref, gffn_ref, wr_ref, wgu_ref, wd_ref, gple_ref,
                    wpg_ref, wpp_ref, out_ref, acc_ref):
    d_expert = wd_ref.shape[1]
    h1 = x_ref[...] + jnp.dot(o_ref[...], wo_ref[...], preferred_element_type=F32)
    xn = _rms_unit(h1) * gffn_ref[...]
    logits = jnp.dot(xn, wr_ref[...], preferred_element_type=F32, precision=lax.Precision.HIGHEST)
    gates = _route(logits)
    lane = lax.broadcasted_iota(jnp.int32, gates.shape, 1)
    xnb = xn.astype(BF16)
    acc_ref[...] = h1

    def expert(e, _):
        gate_e = jnp.sum(jnp.where(lane == N_GROUPS + e, gates, 0.0), axis=-1, keepdims=True)
        gu = jnp.dot(xnb, wgu_ref[e], preferred_element_type=F32)
        g = gu[:, :d_expert]
        u = gu[:, d_expert:]
        hid = (g * jax.nn.sigmoid(g)) * u * gate_e
        acc_ref[...] += jnp.dot(hid.astype(BF16), wd_ref[e], preferred_element_type=F32)
        return 0

    lax.fori_loop(0, N_EXPERTS, expert, 0)
    h2 = acc_ref[...]
    x3 = (_rms_unit(h2) * gple_ref[...]).astype(BF16)
    gate = jax.nn.sigmoid(jnp.dot(x3, wpg_ref[...], preferred_element_type=F32))
    proj = jnp.dot(p_ref[...].astype(BF16), wpp_ref[...], preferred_element_type=F32)
    out_ref[...] = h2 + proj * gate


def _channel(x, o, p, w):
    n, d = x.shape
    tm = min(n, 512)
    row = lambda cols: pl.BlockSpec((tm, cols), lambda i: (i, 0))
    weights = [w["wo"], w["gffn"], w["wr"], w["wgu"], w["wd"], w["gple"], w["wpg"], w["wpp"]]
    return pl.pallas_call(
        _channel_kernel,
        grid=(n // tm,),
        in_specs=[row(d), row(d), row(p.shape[1])] + [_resident(a.shape) for a in weights],
        out_specs=row(d),
        out_shape=jax.ShapeDtypeStruct((n, d), F32),
        scratch_shapes=[pltpu.VMEM((tm, d), F32)],
        compiler_params=_cparams(("parallel",)),
        name="channel",
    )(x, o, p, *weights)


def _proj_b_kernel(h_ref, ga_ref, wq_ref, gq_ref, gkv_ref, wkv_ref, gk_ref, gsum_ref, gexp_ref,
                   q_ref, k_ref, v_ref, *, n_heads, n_kv):
    y = _rms_unit(h_ref[0])
    q = jnp.dot((y * ga_ref[...]).astype(BF16), wq_ref[...], preferred_element_type=F32)
    ms = jnp.dot(_split_hl(q * q), gsum_ref[...], preferred_element_type=F32) * (1.0 / HEAD_DIM)
    inv = lax.rsqrt(ms + EPS)
    inv_full = jnp.dot(_split_hl(inv), gexp_ref[...], preferred_element_type=F32)
    qn = (q * inv_full * gq_ref[...]) * (HEAD_DIM ** -0.5)
    for h in range(n_heads):
        q_ref[0, h] = qn[:, h * HEAD_DIM:(h + 1) * HEAD_DIM].astype(BF16)
    kv = jnp.dot((y * gkv_ref[...]).astype(BF16), wkv_ref[...], preferred_element_type=F32)
    for h in range(n_kv):
        kh = kv[:, h * HEAD_DIM:(h + 1) * HEAD_DIM]
        k_ref[0, h] = _rms_unit(kh) * gk_ref[...]
        v_ref[0, h] = kv[:, (n_kv + h) * HEAD_DIM:(n_kv + h + 1) * HEAD_DIM]


def _proj_b(h, ga, wq, gq, gkv, wkv, gk, gsum, gexp):
    bx, t, d = h.shape
    n_heads = wq.shape[1] // HEAD_DIM
    n_kv = wkv.shape[1] // (2 * HEAD_DIM)
    tm = min(t, 256)
    hm = lambda b, i: (b, 0, i, 0)
    ins = [ga, wq, gq, gkv, wkv, gk, gsum, gexp]
    return pl.pallas_call(
        functools.partial(_proj_b_kernel, n_heads=n_heads, n_kv=n_kv),
        grid=(bx, t // tm),
        in_specs=[pl.BlockSpec((1, tm, d), lambda b, i: (b, i, 0))] + [_resident(a.shape) for a in ins],
        out_specs=[pl.BlockSpec((1, n_heads, tm, HEAD_DIM), hm),
                   pl.BlockSpec((1, n_kv, tm, HEAD_DIM), hm),
                   pl.BlockSpec((1, n_kv, tm, HEAD_DIM), hm)],
        out_shape=[jax.ShapeDtypeStruct((bx, n_heads, t, HEAD_DIM), BF16),
                   jax.ShapeDtypeStruct((bx, n_kv, t, HEAD_DIM), F32),
                   jax.ShapeDtypeStruct((bx, n_kv, t, HEAD_DIM), F32)],
        compiler_params=_cparams(("parallel", "parallel")),
        name="proj_b",
    )(h, *ins)


def _t5_bucket(rel):
    nb = NUM_BUCKETS // 2
    max_exact = nb // 2
    n = jnp.abs(rel)
    large = max_exact + (jnp.log(jnp.maximum(n, 1).astype(jnp.float32) / max_exact)
                         / math.log(MAX_DISTANCE / max_exact) * (nb - max_exact)).astype(jnp.int32)
    large = jnp.minimum(large, nb - 1)
    return jnp.where(rel > 0, nb, 0) + jnp.where(n < max_exact, n, large)


def _bias_kernel(bucket_ref, rbt_ref, out_ref):
    bucket = bucket_ref[...]
    acc = jnp.zeros(out_ref.shape, F32)
    for b in range(NUM_BUCKETS):
        acc = acc + jnp.where(bucket == b, rbt_ref[:, b:b + 1], 0.0)
    out_ref[...] = acc


def _bias_table(rel_bias, nq, nk, key_offset):
    rel = (jnp.arange(nk, dtype=jnp.int32)[None, :] - key_offset) - jnp.arange(nq, dtype=jnp.int32)[:, None]
    bucket = _t5_bucket(rel).reshape(1, nq * nk)
    n_heads = rel_bias.shape[1]
    out = pl.pallas_call(
        _bias_kernel,
        out_shape=jax.ShapeDtypeStruct((n_heads, nq * nk), F32),
        name="bias_table",
    )(bucket, rel_bias.T)
    return out.reshape(n_heads, nq, nk)


def _swa_kernel(sink_ref, q_ref, k_ref, v_ref, bias_ref, o_ref, *, cq, wl, pad, n_chunks, group):
    n_kv = k_ref.shape[1]

    def chunk(c, _):
        r0 = pl.multiple_of(c * cq, cq)
        col = lax.broadcasted_iota(jnp.int32, (cq, wl), 1) + r0
        valid = col >= pad
        for kv in range(n_kv):
            kw = k_ref[0, kv, pl.ds(r0, wl), :].astype(BF16)
            vw = v_ref[0, kv, pl.ds(r0, wl), :].astype(BF16)
            qg = q_ref[0, kv * group:(kv + 1) * group, pl.ds(r0, cq), :].reshape(group * cq, HEAD_DIM)
            logits = lax.dot_general(qg, kw, _NT, preferred_element_type=F32)
            es, dens = [], []
            for g in range(group):
                h = kv * group + g
                l = logits[g * cq:(g + 1) * cq] + bias_ref[h]
                if pad:
                    l = jnp.where(valid, l, -jnp.inf)
                sink = sink_ref[h]
                m = jnp.maximum(jnp.max(l, axis=-1, keepdims=True), sink)
                e = jnp.exp(l - m)
                dens.append(jnp.sum(e, axis=-1, keepdims=True) + jnp.exp(sink - m))
                es.append(e.astype(BF16))
            pv = jnp.dot(jnp.concatenate(es, axis=0), vw, preferred_element_type=F32)
            outs = [pv[g * cq:(g + 1) * cq] / dens[g] for g in range(group)]
            o_ref[0, pl.ds(r0, cq), kv * group * HEAD_DIM:(kv + 1) * group * HEAD_DIM] = (
                jnp.concatenate(outs, axis=-1).astype(BF16))
        return 0

    lax.fori_loop(0, n_chunks, chunk, 0)


def _swa(q, k_win, v_win, bias, sinks, *, cq, wl, pad):
    b, n_heads, tq, _ = q.shape
    n_kv, tk = k_win.shape[1], k_win.shape[2]
    n_chunks = tq // cq
    assert (n_chunks - 1) * cq + wl == tk
    kv_blk = pl.BlockSpec((1, n_kv, tk, HEAD_DIM), lambda bi: (bi, 0, 0, 0))
    return pl.pallas_call(
        functools.partial(_swa_kernel, cq=cq, wl=wl, pad=pad, n_chunks=n_chunks, group=n_heads // n_kv),
        grid=(b,),
        in_specs=[pl.BlockSpec(memory_space=pltpu.SMEM),
                  pl.BlockSpec((1, n_heads, tq, HEAD_DIM), lambda bi: (bi, 0, 0, 0)),
                  kv_blk, kv_blk, _resident(bias.shape)],
        out_specs=pl.BlockSpec((1, tq, n_heads * HEAD_DIM), lambda bi: (bi, 0, 0)),
        out_shape=jax.ShapeDtypeStruct((b, tq, n_heads * HEAD_DIM), BF16),
        compiler_params=_cparams(("parallel",)),
        name="swa",
    )(sinks, q, k_win, v_win, bias)


def _row(v):
    return v.reshape(1, -1).astype(F32)


def _prep_weights(prm):
    d = prm["a_w_o"].shape[1]
    depth = prm["norm_ffn"].shape[0]
    n_a = prm["a_w_qkv"].shape[0]
    scale = HEAD_DIM ** -0.5
    w = {"channel": [], "n_a": n_a, "depth": depth}
    for i in range(depth):
        wo = prm["a_w_o"][i] if i < n_a else prm["b_w_o"][i - n_a]
        pad = ROUTER_LANES - N_GROUPS - N_EXPERTS
        wr = jnp.concatenate([prm["moe_w_group"][i], prm["moe_w_router"][i].reshape(d, N_EXPERTS),
                              jnp.zeros((d, pad), F32)], axis=1)
        w["channel"].append({
            "wo": wo.astype(BF16), "gffn": _row(prm["norm_ffn"][i]), "wr": wr,
            "wgu": jnp.concatenate([prm["moe_w_gate"][i], prm["moe_w_up"][i]], axis=-1).astype(BF16),
            "wd": prm["moe_w_down"][i].astype(BF16), "gple": _row(prm["norm_ple"][i]),
            "wpg": prm["ple_w_gate"][i].astype(BF16), "wpp": prm["ple_w_proj"][i].astype(BF16)})
    w["qkv"] = []
    for i in range(n_a):
        wq = prm["a_w_qkv"][i]
        hd = wq.shape[1] // 3
        col_scale = jnp.concatenate([jnp.full((hd,), scale, F32), jnp.ones((2 * hd,), F32)])
        w["qkv"].append((wq * col_scale[None, :]).astype(BF16))
    n_heads = prm["b_w_q"].shape[2] // HEAD_DIM
    head_of = jnp.arange(n_heads * HEAD_DIM, dtype=jnp.int32) // HEAD_DIM
    lanes = jnp.arange(ROUTER_LANES, dtype=jnp.int32)
    member = (head_of[:, None] == lanes[None, :]).astype(BF16)
    w["gsum"] = jnp.concatenate([member, member], axis=0)
    w["gexp"] = jnp.concatenate([member.T, member.T], axis=0)
    w["wq_b"] = [prm["b_w_q"][j].astype(BF16) for j in range(depth - n_a)]
    w["gq_b"] = [_row(jnp.tile(prm["b_q_norm"][j], n_heads)) for j in range(depth - n_a)]
    w["wkv"] = prm["b_w_kv"].astype(BF16)
    return w


def _assert_sample_window_visible(past_len, s, tk):
    q_chunk = [(past_len + i) // CHUNK for i in range(s)]
    k_pos = [past_len + s - tk + j for j in range(tk)]
    ok = all(kp >= 0 and qc - WIN_CHUNKS <= kp // CHUNK <= qc for qc in q_chunk for kp in k_pos)
    if not ok:
        raise NotImplementedError("sample window with masked keys")


def _run_trunk(x, p, prm, w,sb_cache_k=None, sb_cache_v=None, swa_cache_k=None, swa_cache_v=None):
    bx, t, d = x.shape
    n_a, depth = w["n_a"], w["depth"]
    sample = sb_cache_k is not None
    h = x
    sb_k, sb_v = [], []
    k_win = v_win = None
    q_b = None
    for i in range(depth):
        if i < n_a:
            q, k, v = _proj_a(h, _row(prm["norm_attn"][i]), w["qkv"][i])
            sb_k.append(k)
            sb_v.append(v)
            o = _sb_sample(q, k, v, sb_cache_k, sb_cache_v, i) if sample else _sb_prompt(q, k, v)
        else:
            j = i - n_a
            if j > 0:
                raise NotImplementedError("one B layer supported")
            if sample:
                tk = k_win.shape[2]
                _assert_sample_window_visible(sb_cache_k.shape[3], t, tk)
                o = _swa(q_b, k_win, v_win, _bias_table(prm["rel_bias"], t, tk, tk - t),
                         prm["b_sinks"][j], cq=t, wl=tk, pad=0)
            else:
                front = ((0, 0), (0, 0), (WINDOW, 0), (0, 0))
                wl = WINDOW + CHUNK
                o = _swa(q_b, jnp.pad(k_win, front), jnp.pad(v_win, front),
                         _bias_table(prm["rel_bias"], CHUNK, wl, WINDOW), prm["b_sinks"][j],
                         cq=CHUNK, wl=wl, pad=WINDOW)
        h = _channel(h.reshape(bx * t, d), o.reshape(bx * t, d), p[i].reshape(bx * t, -1),
                     w["channel"][i]).reshape(bx, t, d)
        if i == n_a - 1:
            q_b, k_s, v_s = _proj_b(h, _row(prm["norm_attn"][n_a]), w["wq_b"][0], w["gq_b"][0],
                                    _row(prm["kv_norm"]), w["wkv"], _row(prm["b_k_norm"]),
                                    w["gsum"], w["gexp"])
            if sample:
                k_win = jnp.concatenate([swa_cache_k, k_s], axis=2)
                v_win = jnp.concatenate([swa_cache_v, v_s], axis=2)
            else:
                k_win, v_win = k_s, v_s
    return h, jnp.stack(sb_k), jnp.stack(sb_v), k_win[:, :, -WINDOW:], v_win[:, :, -WINDOW:]


def kernel(x_prompt, x_sample, p_prompt, p_sample, cache_sb_k, cache_sb_v, cache_swa_k, cache_swa_v, norm_attn, norm_ffn, norm_ple, a_w_qkv, a_w_o, kv_norm, b_w_kv, b_k_norm, b_w_q, b_q_norm, b_sinks, b_w_o, rel_bias, moe_w_group, moe_w_router, moe_w_gate, moe_w_up, moe_w_down, ple_w_proj, ple_w_gate):
    prm = {
        "norm_attn": norm_attn, "norm_ffn": norm_ffn, "norm_ple": norm_ple,
        "a_w_qkv": a_w_qkv, "a_w_o": a_w_o, "kv_norm": kv_norm, "b_w_kv": b_w_kv, "b_k_norm": b_k_norm,
        "b_w_q": b_w_q, "b_q_norm": b_q_norm, "b_sinks": b_sinks, "b_w_o": b_w_o, "rel_bias": rel_bias,
        "moe_w_group": moe_w_group, "moe_w_router": moe_w_router, "moe_w_gate": moe_w_gate,
        "moe_w_up": moe_w_up, "moe_w_down": moe_w_down, "ple_w_proj": ple_w_proj, "ple_w_gate": ple_w_gate,
    }
    w = _prep_weights(prm)
    y_p, sb_k_p, sb_v_p, swa_k_p, swa_v_p = _run_trunk(x_prompt, p_prompt, prm, w)
    y_s, sb_k_s, sb_v_s, swa_k_s, swa_v_s = _run_trunk(x_sample, p_sample, prm, w, cache_sb_k, cache_sb_v,
                                                       cache_swa_k, cache_swa_v)
    return (y_p, y_s, sb_k_p, sb_v_p, swa_k_p, swa_v_p, sb_k_s, sb_v_s, swa_k_s, swa_v_s)
```

```python
import functools
import math

import jax
import jax.numpy as jnp
from jax import lax
from jax.experimental import pallas as pl
from jax.experimental.pallas import tpu as pltpu

F32 = jnp.float32
BF16 = jnp.bfloat16

HEAD_DIM = 64
CHUNK = 64
WINDOW = 128
WIN_CHUNKS = WINDOW // CHUNK
NUM_BUCKETS = 32
MAX_DISTANCE = 128
N_GROUPS = 4
EXPERTS_PER_GROUP = 4
N_EXPERTS = N_GROUPS * EXPERTS_PER_GROUP
EPS = 1e-6
SB_BLOCK = 128
ROUTER_LANES = 128
VMEM_LIMIT = 56 * 1024 * 1024

_NT = (((1,), (1,)), ((), ()))


def _cparams(sem):
    return pltpu.CompilerParams(dimension_semantics=sem, vmem_limit_bytes=VMEM_LIMIT)


def _rms_unit(x):
    return x * lax.rsqrt(jnp.mean(x * x, axis=-1, keepdims=True) + EPS)


def _split_hl(a):
    hi = a.astype(BF16)
    lo = (a - hi.astype(F32)).astype(BF16)
    return jnp.concatenate([hi, lo], axis=-1)


def _resident(shape):
    nd = len(shape)
    return pl.BlockSpec(shape, lambda *_: (0,) * nd, pipeline_mode=pl.Buffered(1))


def _proj_a_kernel(x_ref, g_ref, wq_ref, wkvt_ref, q_ref, kt_ref, vt_ref, *, n_heads):
    xn = (_rms_unit(x_ref[0]) * g_ref[...]).astype(BF16)
    heads_per_dot = 4
    width = heads_per_dot * HEAD_DIM
    for c in range(n_heads // heads_per_dot):
        r = jnp.dot(xn, wq_ref[:, c * width:(c + 1) * width], preferred_element_type=F32)
        for hh in range(heads_per_dot):
            q_ref[0, c * heads_per_dot + hh] = r[:, hh * HEAD_DIM:(hh + 1) * HEAD_DIM].astype(BF16)
    for c in range(2 * n_heads // heads_per_dot):
        r = lax.dot_general(wkvt_ref[c * width:(c + 1) * width, :], xn, _NT, preferred_element_type=F32)
        for hh in range(heads_per_dot):
            h = c * heads_per_dot + hh
            piece = r[hh * HEAD_DIM:(hh + 1) * HEAD_DIM, :]
            if h < n_heads:
                kt_ref[0, h] = piece
            else:
                vt_ref[0, h - n_heads] = piece


def _proj_a(x, g, wq, wkvt):
    bx, t, d = x.shape
    n_heads = wq.shape[1] // HEAD_DIM
    tm = min(t, 256)
    return pl.pallas_call(
        functools.partial(_proj_a_kernel, n_heads=n_heads),
        grid=(bx, t // tm),
        in_specs=[pl.BlockSpec((1, tm, d), lambda b, i: (b, i, 0)),
                  _resident((1, d)), _resident(wq.shape), _resident(wkvt.shape)],
        out_specs=[pl.BlockSpec((1, n_heads, tm, HEAD_DIM), lambda b, i: (b, 0, i, 0)),
                   pl.BlockSpec((1, n_heads, HEAD_DIM, tm), lambda b, i: (b, 0, 0, i)),
                   pl.BlockSpec((1, n_heads, HEAD_DIM, tm), lambda b, i: (b, 0, 0, i))],
        out_shape=[jax.ShapeDtypeStruct((bx, n_heads, t, HEAD_DIM), BF16),
                   jax.ShapeDtypeStruct((bx, n_heads, HEAD_DIM, t), F32),
                   jax.ShapeDtypeStruct((bx, n_heads, HEAD_DIM, t), F32)],
        compiler_params=_cparams(("parallel", "parallel")),
        name="proj_a",
    )(x, g, wq, wkvt)


SB_DEAD = -104.0
SB_WINDOW_BLOCKS = 3
SB_SAMPLE_WINDOW = 256


def _suffix_neg_ones(n):
    r = lax.broadcasted_iota(jnp.int32, (2 * n, n), 0)
    c = lax.broadcasted_iota(jnp.int32, (2 * n, n), 1)
    r = jnp.where(r >= n, r - n, r)
    return jnp.where(r >= c, -1.0, 0.0).astype(BF16)


def _strict_mask(n):
    row = lax.broadcasted_iota(jnp.int32, (n, n), 0)
    col = lax.broadcasted_iota(jnp.int32, (n, n), 1)
    return col < row


def _sb_strip(q, kt, vt, carry, acc, uu, masks):
    blk = uu.shape[1]
    nb = kt.shape[1] // blk
    z = jnp.dot(q, kt, preferred_element_type=F32)
    sp = jnp.maximum(z, 0.0) + jnp.log(1.0 + jnp.exp(-jnp.abs(z)))
    ws = [None] * nb
    for b in reversed(range(nb)):
        sl = slice(b * blk, (b + 1) * blk)
        spb = sp[:, sl]
        if masks[b] is not None:
            spb = jnp.where(masks[b], spb, 0.0)
        suffix = jnp.dot(_split_hl(spb), uu, preferred_element_type=F32)
        wb = jnp.exp(z[:, sl] + suffix + carry)
        if masks[b] is not None:
            wb = jnp.where(masks[b], wb, 0.0)
        ws[b] = wb.astype(BF16)
        carry = carry + suffix[:, 0:1]
    w = ws[0] if nb == 1 else jnp.concatenate(ws, axis=-1)
    acc = acc + lax.dot_general(w, vt, _NT, preferred_element_type=F32)
    return carry, acc


def _sb_prompt_fast_kernel(q_ref, kt_ref, vt_ref, o_ref, flag_ref, *, heads):
    t = q_ref.shape[2]
    blk = SB_BLOCK
    nq = t // blk
    win = SB_WINDOW_BLOCKS
    uu = _suffix_neg_ones(blk)
    strict = _strict_mask(blk)
    zero_c = jnp.zeros((blk, 1), F32)
    zero_a = jnp.zeros((blk, HEAD_DIM), F32)

    def tile(q0, k0, nb):
        res = []
        for h in range(heads):
            q = q_ref[0, h, pl.ds(q0, blk), :]
            kt = kt_ref[0, h, :, pl.ds(k0, nb * blk)].astype(BF16)
            vt = vt_ref[0, h, :, pl.ds(k0, nb * blk)].astype(BF16)
            res.append(_sb_strip(q, kt, vt, zero_c, zero_a, uu, [None] * (nb - 1) + [strict]))
        o_ref[0, pl.ds(q0, blk), :] = jnp.concatenate([a for _, a in res], axis=-1).astype(BF16)
        return [c for c, _ in res]

    for i in range(min(win - 1, nq)):
        tile(i * blk, 0, i + 1)

    def body(i, worst):
        q0 = pl.multiple_of(i * blk, blk)
        k0 = pl.multiple_of((i - (win - 1)) * blk, blk)
        for c in tile(q0, k0, win):
            worst = jnp.maximum(worst, c)
        return worst

    worst = lax.fori_loop(win - 1, nq, body, jnp.full((blk, 1), -jnp.inf, F32))
    flag_ref[...] = jnp.broadcast_to(jnp.max(worst), flag_ref.shape)


def _sb_prompt_full_kernel(q_ref, kt_ref, vt_ref, o_ref, *, heads):
    t = q_ref.shape[2]
    blk = SB_BLOCK
    uu = _suffix_neg_ones(blk)
    strict = _strict_mask(blk)

    def q_block(i, _):
        q0 = pl.multiple_of(i * blk, blk)
        qs = [q_ref[0, h, pl.ds(q0, blk), :] for h in range(heads)]
        state = []
        for h in range(heads):
            kd = kt_ref[0, h, :, pl.ds(q0, blk)].astype(BF16)
            vd = vt_ref[0, h, :, pl.ds(q0, blk)].astype(BF16)
            state.extend(_sb_strip(qs[h], kd, vd, jnp.zeros((blk, 1), F32),
                                   jnp.zeros((blk, HEAD_DIM), F32), uu, [strict]))

        def k_block(jj, st):
            k0 = pl.multiple_of((i - 1 - jj) * blk, blk)
            new = []
            for h in range(heads):
                kb = kt_ref[0, h, :, pl.ds(k0, blk)].astype(BF16)
                vb = vt_ref[0, h, :, pl.ds(k0, blk)].astype(BF16)
                new.extend(_sb_strip(qs[h], kb, vb, st[2 * h], st[2 * h + 1], uu, [None]))
            return tuple(new)

        state = lax.fori_loop(0, i, k_block, tuple(state))
        o_ref[0, pl.ds(q0, blk), :] = jnp.concatenate(
            [state[2 * h + 1] for h in range(heads)], axis=-1).astype(BF16)
        return 0

    lax.fori_loop(0, t // blk, q_block, 0)


def _sb_prompt(q, kt, vt):
    b, n_heads, t, _ = q.shape
    heads = 4
    q_blk = pl.BlockSpec((1, heads, t, HEAD_DIM), lambda bi, hg: (bi, hg, 0, 0))
    kv_blk = pl.BlockSpec((1, heads, HEAD_DIM, t), lambda bi, hg: (bi, hg, 0, 0))
    o_blk = pl.BlockSpec((1, t, heads * HEAD_DIM), lambda bi, hg: (bi, 0, hg))
    o_shape = jax.ShapeDtypeStruct((b, t, n_heads * HEAD_DIM), BF16)
    grid = (b, n_heads // heads)
    o_fast, flags = pl.pallas_call(
        functools.partial(_sb_prompt_fast_kernel, heads=heads),
        grid=grid,
        in_specs=[q_blk, kv_blk, kv_blk],
        out_specs=[o_blk, pl.BlockSpec((1, 1, 8, 128), lambda bi, hg: (bi, hg, 0, 0))],
        out_shape=[o_shape, jax.ShapeDtypeStruct((b, n_heads // heads, 8, 128), F32)],
        compiler_params=_cparams(("parallel", "parallel")),
        name="sb_prompt_fast",
    )(q, kt, vt)
    if t // SB_BLOCK <= SB_WINDOW_BLOCKS:
        return o_fast

    def full():
        return pl.pallas_call(
            functools.partial(_sb_prompt_full_kernel, heads=heads),
            grid=grid,
            in_specs=[q_blk, kv_blk, kv_blk],
            out_specs=o_blk,
            out_shape=o_shape,
            compiler_params=_cparams(("parallel", "parallel")),
            name="sb_prompt_full",
        )(q, kt, vt)

    return lax.cond(jnp.max(flags) > SB_DEAD, full, lambda: o_fast)


def _sb_sample_kernel(q_ref, ktn_ref, vtn_ref, ktc_ref, vtc_ref, o_ref, *maybe_flag, heads, strip):
    s = q_ref.shape[2]
    width = ktc_ref.shape[4]
    blk = SB_BLOCK
    uu = _suffix_neg_ones(blk)
    uu_new = _suffix_neg_ones(s)
    strict = _strict_mask(s)

    qs = [q_ref[0, h] for h in range(heads)]
    state = []
    for h in range(heads):
        state.extend(_sb_strip(qs[h], ktn_ref[0, h].astype(BF16), vtn_ref[0, h].astype(BF16),
                               jnp.zeros((s, 1), F32), jnp.zeros((s, HEAD_DIM), F32), uu_new, [strict]))

    def k_strip(jj, st):
        k0 = pl.multiple_of(width - (jj + 1) * strip * blk, blk)
        new = []
        for h in range(heads):
            kb = ktc_ref[0, 0, h, :, pl.ds(k0, strip * blk)].astype(BF16)
            vb = vtc_ref[0, 0, h, :, pl.ds(k0, strip * blk)].astype(BF16)
            new.extend(_sb_strip(qs[h], kb, vb, st[2 * h], st[2 * h + 1], uu, [None] * strip))
        return tuple(new)

    state = lax.fori_loop(0, width // (strip * blk), k_strip, tuple(state))
    o_ref[0] = jnp.concatenate([state[2 * h + 1] for h in range(heads)], axis=-1).astype(BF16)
    if maybe_flag:
        worst = state[0]
        for h in range(1, heads):
            worst = jnp.maximum(worst, state[2 * h])
        maybe_flag[0][...] = jnp.broadcast_to(jnp.max(worst), maybe_flag[0].shape)


def _sb_sample(q, kt_new, vt_new, cache_kt, cache_vt, layer):
    b, n_heads, s, _ = q.shape
    past = cache_kt.shape[4]
    heads = 4
    width = min(past, SB_SAMPLE_WINDOW)
    assert past % width == 0 and width % SB_BLOCK == 0
    grid = (b, n_heads // heads)
    q_blk = pl.BlockSpec((1, heads, s, HEAD_DIM), lambda bi, hg: (bi, hg, 0, 0))
    new_blk = pl.BlockSpec((1, heads, HEAD_DIM, s), lambda bi, hg: (bi, hg, 0, 0))
    last = past // width - 1
    win_blk = pl.BlockSpec((1, 1, heads, HEAD_DIM, width), lambda bi, hg: (layer, bi, hg, 0, last))
    o_blk = pl.BlockSpec((1, s, heads * HEAD_DIM), lambda bi, hg: (bi, 0, hg))
    o_shape = jax.ShapeDtypeStruct((b, s, n_heads * HEAD_DIM), BF16)
    o_fast, flags = pl.pallas_call(
        functools.partial(_sb_sample_kernel, heads=heads, strip=width // SB_BLOCK),
        grid=grid,
        in_specs=[q_blk, new_blk, new_blk, win_blk, win_blk],
        out_specs=[o_blk, pl.BlockSpec((1, 1, 8, 128), lambda bi, hg: (bi, hg, 0, 0))],
        out_shape=[o_shape, jax.ShapeDtypeStruct((b, n_heads // heads, 8, 128), F32)],
        compiler_params=_cparams(("parallel", "parallel")),
        name="sb_sample_fast",
    )(q, kt_new, vt_new, cache_kt, cache_vt)
    if width == past:
        return o_fast

    def full():
        full_blk = pl.BlockSpec((1, 1, heads, HEAD_DIM, past), lambda bi, hg: (layer, bi, hg, 0, 0))
        return pl.pallas_call(
            functools.partial(_sb_sample_kernel, heads=heads, strip=1),
            grid=grid,
            in_specs=[q_blk, new_blk, new_blk, full_blk, full_blk],
            out_specs=o_blk,
            out_shape=o_shape,
            compiler_params=_cparams(("parallel", "parallel")),
            name="sb_sample_full",
        )(q, kt_new, vt_new, cache_kt, cache_vt)

    return lax.cond(jnp.max(flags) > SB_DEAD, full, lambda: o_fast)


def _route(logits):
    tm = logits.shape[0]
    lane = lax.broadcasted_iota(jnp.int32, (tm, ROUTER_LANES), 1)
    lane_f = lane.astype(F32)
    neg = -jnp.inf
    first = lambda hit: jnp.min(jnp.where(hit, lane_f, float(ROUTER_LANES)), axis=-1, keepdims=True)

    gl = jnp.where(lane < N_GROUPS, logits, neg)
    g_max = jnp.max(gl, axis=-1, keepdims=True)
    g_top = 1.0 / jnp.sum(jnp.exp(gl - g_max), axis=-1, keepdims=True)
    g_idx = first(gl == g_max)

    lo = N_GROUPS + g_idx * EXPERTS_PER_GROUP
    in_group = (lane_f >= lo) & (lane_f < lo + EXPERTS_PER_GROUP)
    sel = jnp.where(in_group, logits, neg)
    t1 = jnp.max(sel, axis=-1, keepdims=True)
    i1 = first(sel == t1)
    sel2 = jnp.where(lane_f == i1, neg, sel)
    t2 = jnp.max(sel2, axis=-1, keepdims=True)
    i2 = first(sel2 == t2)
    e2 = jnp.exp(t2 - t1)
    den = 1.0 + e2
    w1 = (1.0 / den) * g_top
    w2 = (e2 / den) * g_top
    return jnp.where(lane_f == i1, w1, 0.0) + jnp.where(lane_f == i2, w2, 0.0)


def _channel_kernel(x_ref, o_ref, p_ref, wo_ref, gffn_ref, wr_ref, wgu_ref, wd_ref, gple_ref,
                    wpg_ref, wpp_ref, out_ref, acc_ref):
    d_expert = wd_ref.shape[1]
    h1 = x_ref[...] + jnp.dot(o_ref[...], wo_ref[...], preferred_element_type=F32)
    xn = _rms_unit(h1) * gffn_ref[...]
    logits = jnp.dot(xn, wr_ref[...], preferred_element_type=F32, precision=lax.Precision.HIGHEST)
    gates = _route(logits)
    lane = lax.broadcasted_iota(jnp.int32, gates.shape, 1)
    xnb = xn.astype(BF16)
    acc_ref[...] = h1

    def expert(e, _):
        gate_e = jnp.sum(jnp.where(lane == N_GROUPS + e, gates, 0.0), axis=-1, keepdims=True)
        gu = jnp.dot(xnb, wgu_ref[e], preferred_element_type=F32)
        g = gu[:, :d_expert]
        u = gu[:, d_expert:]
        hid = (g * jax.nn.sigmoid(g)) * u * gate_e
        acc_ref[...] += jnp.dot(hid.astype(BF16), wd_ref[e], preferred_element_type=F32)
        return 0

    lax.fori_loop(0, N_EXPERTS, expert, 0)
    h2 = acc_ref[...]
    x3 = (_rms_unit(h2) * gple_ref[...]).astype(BF16)
    gate = jax.nn.sigmoid(jnp.dot(x3, wpg_ref[...], preferred_element_type=F32))
    proj = jnp.dot(p_ref[...].astype(BF16), wpp_ref[...], preferred_element_type=F32)
    out_ref[...] = h2 + proj * gate


def _channel(x, o, p, w):
    n, d = x.shape
    tm = min(n, 512)
    row = lambda cols: pl.BlockSpec((tm, cols), lambda i: (i, 0))
    weights = [w["wo"], w["gffn"], w["wr"], w["wgu"], w["wd"], w["gple"], w["wpg"], w["wpp"]]
    return pl.pallas_call(
        _channel_kernel,
        grid=(n // tm,),
        in_specs=[row(d), row(d), row(p.shape[1])] + [_resident(a.shape) for a in weights],
        out_specs=row(d),
        out_shape=jax.ShapeDtypeStruct((n, d), F32),
        scratch_shapes=[pltpu.VMEM((tm, d), F32)],
        compiler_params=_cparams(("parallel",)),
        name="channel",
    )(x, o, p, *weights)


def _proj_b_kernel(h_ref, ga_ref, wq_ref, gq_ref, gkv_ref, wkv_ref, gk_ref, gsum_ref, gexp_ref,
                   q_ref, k_ref, v_ref, *, n_heads, n_kv):
    y = _rms_unit(h_ref[0])
    q = jnp.dot((y * ga_ref[...]).astype(BF16), wq_ref[...], preferred_element_type=F32)
    ms = jnp.dot(_split_hl(q * q), gsum_ref[...], preferred_element_type=F32) * (1.0 / HEAD_DIM)
    inv = lax.rsqrt(ms + EPS)
    inv_full = jnp.dot(_split_hl(inv), gexp_ref[...], preferred_element_type=F32)
    qn = (q * inv_full * gq_ref[...]) * (HEAD_DIM ** -0.5)
    for h in range(n_heads):
        q_ref[0, h] = qn[:, h * HEAD_DIM:(h + 1) * HEAD_DIM].astype(BF16)
    kv = jnp.dot((y * gkv_ref[...]).astype(BF16), wkv_ref[...], preferred_element_type=F32)
    for h in range(n_kv):
        kh = kv[:, h * HEAD_DIM:(h + 1) * HEAD_DIM]
        k_ref[0, h] = _rms_unit(kh) * gk_ref[...]
        v_ref[0, h] = kv[:, (n_kv + h) * HEAD_DIM:(n_kv + h + 1) * HEAD_DIM]


def _proj_b(h, ga, wq, gq, gkv, wkv, gk, gsum, gexp):
    bx, t, d = h.shape
    n_heads = wq.shape[1] // HEAD_DIM
    n_kv = wkv.shape[1] // (2 * HEAD_DIM)
    tm = min(t, 256)
    hm = lambda b, i: (b, 0, i, 0)
    ins = [ga, wq, gq, gkv, wkv, gk, gsum, gexp]
    return pl.pallas_call(
        functools.partial(_proj_b_kernel, n_heads=n_heads, n_kv=n_kv),
        grid=(bx, t // tm),
        in_specs=[pl.BlockSpec((1, tm, d), lambda b, i: (b, i, 0))] + [_resident(a.shape) for a in ins],
        out_specs=[pl.BlockSpec((1, n_heads, tm, HEAD_DIM), hm),
                   pl.BlockSpec((1, n_kv, tm, HEAD_DIM), hm),
                   pl.BlockSpec((1, n_kv, tm, HEAD_DIM), hm)],
        out_shape=[jax.ShapeDtypeStruct((bx, n_heads, t, HEAD_DIM), BF16),
                   jax.ShapeDtypeStruct((bx, n_kv, t, HEAD_DIM), F32),
                   jax.ShapeDtypeStruct((bx, n_kv, t, HEAD_DIM), F32)],
        compiler_params=_cparams(("parallel", "parallel")),
        name="proj_b",
    )(h, *ins)


def _t5_bucket(rel):
    nb = NUM_BUCKETS // 2
    max_exact = nb // 2
    n = jnp.abs(rel)
    large = max_exact + (jnp.log(jnp.maximum(n, 1).astype(jnp.float32) / max_exact)
                         / math.log(MAX_DISTANCE / max_exact) * (nb - max_exact)).astype(jnp.int32)
    large = jnp.minimum(large, nb - 1)
    return jnp.where(rel > 0, nb, 0) + jnp.where(n < max_exact, n, large)


def _bias_kernel(bucket_ref, rbt_ref, out_ref):
    bucket = bucket_ref[...]
    acc = jnp.zeros(out_ref.shape, F32)
    for b in range(NUM_BUCKETS):
        acc = acc + jnp.where(bucket == b, rbt_ref[:, b:b + 1], 0.0)
    out_ref[...] = acc


def _bias_table(rel_bias, nq, nk, key_offset):
    rel = (jnp.arange(nk, dtype=jnp.int32)[None, :] - key_offset) - jnp.arange(nq, dtype=jnp.int32)[:, None]
    bucket = _t5_bucket(rel).reshape(1, nq * nk)
    n_heads = rel_bias.shape[1]
    out = pl.pallas_call(
        _bias_kernel,
        out_shape=jax.ShapeDtypeStruct((n_heads, nq * nk), F32),
        name="bias_table",
    )(bucket, rel_bias.T)
    return out.reshape(n_heads, nq, nk)


def _swa_kernel(sink_ref, q_ref, k_ref, v_ref, bias_ref, o_ref, *, cq, wl, pad, n_chunks, group):
    n_kv = k_ref.shape[1]

    def chunk(c, _):
        r0 = pl.multiple_of(c * cq, cq)
        col = lax.broadcasted_iota(jnp.int32, (cq, wl), 1) + r0
        valid = col >= pad
        for kv in range(n_kv):
            kw = k_ref[0, kv, pl.ds(r0, wl), :].astype(BF16)
            vw = v_ref[0, kv, pl.ds(r0, wl), :].astype(BF16)
            qg = q_ref[0, kv * group:(kv + 1) * group, pl.ds(r0, cq), :].reshape(group * cq, HEAD_DIM)
            logits = lax.dot_general(qg, kw, _NT, preferred_element_type=F32)
            es, dens = [], []
            for g in range(group):
                h = kv * group + g
                l = logits[g * cq:(g + 1) * cq] + bias_ref[h]
                if pad:
                    l = jnp.where(valid, l, -jnp.inf)
                sink = sink_ref[h]
                m = jnp.maximum(jnp.max(l, axis=-1, keepdims=True), sink)
                e = jnp.exp(l - m)
                dens.append(jnp.sum(e, axis=-1, keepdims=True) + jnp.exp(sink - m))
                es.append(e.astype(BF16))
            pv = jnp.dot(jnp.concatenate(es, axis=0), vw, preferred_element_type=F32)
            outs = [pv[g * cq:(g + 1) * cq] / dens[g] for g in range(group)]
            o_ref[0, pl.ds(r0, cq), kv * group * HEAD_DIM:(kv + 1) * group * HEAD_DIM] = (
                jnp.concatenate(outs, axis=-1).astype(BF16))
        return 0

    lax.fori_loop(0, n_chunks, chunk, 0)


def _swa(q, k_win, v_win, bias, sinks, *, cq, wl, pad):
    b, n_heads, tq, _ = q.shape
    n_kv, tk = k_win.shape[1], k_win.shape[2]
    n_chunks = tq // cq
    assert (n_chunks - 1) * cq + wl == tk
    kv_blk = pl.BlockSpec((1, n_kv, tk, HEAD_DIM), lambda bi: (bi, 0, 0, 0))
    return pl.pallas_call(
        functools.partial(_swa_kernel, cq=cq, wl=wl, pad=pad, n_chunks=n_chunks, group=n_heads // n_kv),
        grid=(b,),
        in_specs=[pl.BlockSpec(memory_space=pltpu.SMEM),
                  pl.BlockSpec((1, n_heads, tq, HEAD_DIM), lambda bi: (bi, 0, 0, 0)),
                  kv_blk, kv_blk, _resident(bias.shape)],
        out_specs=pl.BlockSpec((1, tq, n_heads * HEAD_DIM), lambda bi: (bi, 0, 0)),
        out_shape=jax.ShapeDtypeStruct((b, tq, n_heads * HEAD_DIM), BF16),
        compiler_params=_cparams(("parallel",)),
        name="swa",
    )(sinks, q, k_win, v_win, bias)


def _row(v):
    return v.reshape(1, -1).astype(F32)


def _prep_weights(prm):
    d = prm["a_w_o"].shape[1]
    depth = prm["norm_ffn"].shape[0]
    n_a = prm["a_w_qkv"].shape[0]
    scale = HEAD_DIM ** -0.5
    w = {"channel": [], "n_a": n_a, "depth": depth}
    for i in range(depth):
        wo = prm["a_w_o"][i] if i < n_a else prm["b_w_o"][i - n_a]
        pad = ROUTER_LANES - N_GROUPS - N_EXPERTS
        wr = jnp.concatenate([prm["moe_w_group"][i], prm["moe_w_router"][i].reshape(d, N_EXPERTS),
                              jnp.zeros((d, pad), F32)], axis=1)
        w["channel"].append({
            "wo": wo.astype(BF16), "gffn": _row(prm["norm_ffn"][i]), "wr": wr,
            "wgu": jnp.concatenate([prm["moe_w_gate"][i], prm["moe_w_up"][i]], axis=-1).astype(BF16),
            "wd": prm["moe_w_down"][i].astype(BF16), "gple": _row(prm["norm_ple"][i]),
            "wpg": prm["ple_w_gate"][i].astype(BF16), "wpp": prm["ple_w_proj"][i].astype(BF16)})
    w["qkv"] = []
    for i in range(n_a):
        wq = prm["a_w_qkv"][i]
        hd = wq.shape[1] // 3
        w["qkv"].append(((wq[:, :hd] * scale).astype(BF16),
                         wq[:, hd:].T.astype(BF16)))
    n_heads = prm["b_w_q"].shape[2] // HEAD_DIM
    head_of = jnp.arange(n_heads * HEAD_DIM, dtype=jnp.int32) // HEAD_DIM
    lanes = jnp.arange(ROUTER_LANES, dtype=jnp.int32)
    member = (head_of[:, None] == lanes[None, :]).astype(BF16)
    w["gsum"] = jnp.concatenate([member, member], axis=0)
    w["gexp"] = jnp.concatenate([member.T, member.T], axis=0)
    w["wq_b"] = [prm["b_w_q"][j].astype(BF16) for j in range(depth - n_a)]
    w["gq_b"] = [_row(jnp.tile(prm["b_q_norm"][j], n_heads)) for j in range(depth - n_a)]
    w["wkv"] = prm["b_w_kv"].astype(BF16)
    return w


def _assert_sample_window_visible(past_len, s, tk):
    q_chunk = [(past_len + i) // CHUNK for i in range(s)]
    k_pos = [past_len + s - tk + j for j in range(tk)]
    ok = all(kp >= 0 and qc - WIN_CHUNKS <= kp // CHUNK <= qc for qc in q_chunk for kp in k_pos)
    if not ok:
        raise NotImplementedError("sample window with masked keys")


def _run_trunk(x, p, prm, w, sb_cache_k=None, sb_cache_v=None, swa_cache_k=None, swa_cache_v=None):
    bx, t, d = x.shape
    n_a, depth = w["n_a"], w["depth"]
    sample = sb_cache_k is not None
    h = x
    sb_k, sb_v = [], []
    k_win = v_win = None
    q_b = None
    for i in range(depth):
        if i < n_a:
            q, kt, vt = _proj_a(h, _row(prm["norm_attn"][i]), *w["qkv"][i])
            sb_k.append(kt)
            sb_v.append(vt)
            if sample:
                o = _sb_sample(q, kt, vt, jnp.swapaxes(sb_cache_k, -1, -2), jnp.swapaxes(sb_cache_v, -1, -2), i)
            else:
                o = _sb_prompt(q, kt, vt)
        else:
            j = i - n_a
            if j > 0:
                raise NotImplementedError("one B layer supported")
            if sample:
                tk = k_win.shape[2]
                _assert_sample_window_visible(sb_cache_k.shape[3], t, tk)
                o = _swa(q_b, k_win, v_win, _bias_table(prm["rel_bias"], t, tk, tk - t),
                         prm["b_sinks"][j], cq=t, wl=tk, pad=0)
            else:
                front = ((0, 0), (0, 0), (WINDOW, 0), (0, 0))
                wl = WINDOW + CHUNK
                o = _swa(q_b, jnp.pad(k_win, front), jnp.pad(v_win, front),
                         _bias_table(prm["rel_bias"], CHUNK, wl, WINDOW), prm["b_sinks"][j],
                         cq=CHUNK, wl=wl, pad=WINDOW)
        h = _channel(h.reshape(bx * t, d), o.reshape(bx * t, d), p[i].reshape(bx * t, -1),
                     w["channel"][i]).reshape(bx, t, d)
        if i == n_a - 1:
            q_b, k_s, v_s = _proj_b(h, _row(prm["norm_attn"][n_a]), w["wq_b"][0], w["gq_b"][0],
                                    _row(prm["kv_norm"]), w["wkv"], _row(prm["b_k_norm"]),
                                    w["gsum"], w["gexp"])
            if sample:
                k_win = jnp.concatenate([swa_cache_k, k_s], axis=2)
                v_win = jnp.concatenate([swa_cache_v, v_s], axis=2)
            else:
                k_win, v_win = k_s, v_s
    sb_k = jnp.swapaxes(jnp.stack(sb_k), -1, -2)
    sb_v = jnp.swapaxes(jnp.stack(sb_v), -1, -2)
    return h, sb_k, sb_v, k_win[:, :, -WINDOW:], v_win[:, :, -WINDOW:]


def kernel(x_prompt, x_sample, p_prompt, p_sample, cache_sb_k, cache_sb_v, cache_swa_k, cache_swa_v, norm_attn, norm_ffn, norm_ple, a_w_qkv, a_w_o, kv_norm, b_w_kv, b_k_norm, b_w_q, b_q_norm, b_sinks, b_w_o, rel_bias, moe_w_group, moe_w_router, moe_w_gate, moe_w_up, moe_w_down, ple_w_proj, ple_w_gate):
    prm = {
        "norm_attn": norm_attn, "norm_ffn": norm_ffn, "norm_ple": norm_ple,
        "a_w_qkv": a_w_qkv, "a_w_o": a_w_o, "kv_norm": kv_norm, "b_w_kv": b_w_kv, "b_k_norm": b_k_norm,
        "b_w_q": b_w_q, "b_q_norm": b_q_norm, "b_sinks": b_sinks, "b_w_o": b_w_o, "rel_bias": rel_bias,
        "moe_w_group": moe_w_group, "moe_w_router": moe_w_router, "moe_w_gate": moe_w_gate,
        "moe_w_up": moe_w_up, "moe_w_down": moe_w_down, "ple_w_proj": ple_w_proj, "ple_w_gate": ple_w_gate,
    }
    w = _prep_weights(prm)
    y_p, sb_k_p, sb_v_p, swa_k_p, swa_v_p = _run_trunk(x_prompt, p_prompt, prm, w)
    y_s, sb_k_s, sb_v_s, swa_k_s, swa_v_s = _run_trunk(x_sample, p_sample, prm, w, cache_sb_k, cache_sb_v,
                                                       cache_swa_k, cache_swa_v)
    return (y_p, y_s, sb_k_p, sb_v_p, swa_k_p, swa_v_p, sb_k_s, sb_v_s, swa_k_s, swa_v_s)
```

```python
import functools
import math

import jax
import jax.numpy as jnp
from jax import lax
from jax.experimental import pallas as pl
from jax.experimental.pallas import tpu as pltpu

F32 = jnp.float32
BF16 = jnp.bfloat16

HEAD_DIM = 64
CHUNK = 64
WINDOW = 128
WIN_CHUNKS = WINDOW // CHUNK
NUM_BUCKETS = 32
MAX_DISTANCE = 128
N_GROUPS = 4
EXPERTS_PER_GROUP = 4
N_EXPERTS = N_GROUPS * EXPERTS_PER_GROUP
EPS = 1e-6
LOG2E = 1.4426950408889634
SB_BLOCK = 128
ROUTER_LANES = 128
VMEM_LIMIT = 56 * 1024 * 1024

_NT = (((1,), (1,)), ((), ()))


def _cparams(sem):
    return pltpu.CompilerParams(dimension_semantics=sem, vmem_limit_bytes=VMEM_LIMIT)


def _rms_unit(x):
    return x * lax.rsqrt(jnp.mean(x * x, axis=-1, keepdims=True) + EPS)


def _split_hl(a):
    hi = a.astype(BF16)
    lo = (a - hi.astype(F32)).astype(BF16)
    return jnp.concatenate([hi, lo], axis=-1)


def _resident(shape):
    nd = len(shape)
    return pl.BlockSpec(shape, lambda *_: (0,) * nd, pipeline_mode=pl.Buffered(1))


def _proj_a_kernel(x_ref, g_ref, wq_ref, wkvt_ref, q_ref, kt_ref, vt_ref, *, n_heads):
    xn = (_rms_unit(x_ref[0]) * g_ref[...]).astype(BF16)
    heads_per_dot = 4
    width = heads_per_dot * HEAD_DIM
    for c in range(n_heads // heads_per_dot):
        r = jnp.dot(xn, wq_ref[:, c * width:(c + 1) * width], preferred_element_type=F32)
        for hh in range(heads_per_dot):
            q_ref[0, c * heads_per_dot + hh] = r[:, hh * HEAD_DIM:(hh + 1) * HEAD_DIM].astype(BF16)
    for c in range(2 * n_heads // heads_per_dot):
        r = lax.dot_general(wkvt_ref[c * width:(c + 1) * width, :], xn, _NT, preferred_element_type=F32)
        for hh in range(heads_per_dot):
            h = c * heads_per_dot + hh
            piece = r[hh * HEAD_DIM:(hh + 1) * HEAD_DIM, :]
            if h < n_heads:
                kt_ref[0, h] = piece
            else:
                vt_ref[0, h - n_heads] = piece


def _proj_a(x, g, wq, wkvt):
    bx, t, d = x.shape
    n_heads = wq.shape[1] // HEAD_DIM
    tm = min(t, 256)
    return pl.pallas_call(
        functools.partial(_proj_a_kernel, n_heads=n_heads),
        grid=(bx, t // tm),
        in_specs=[pl.BlockSpec((1, tm, d), lambda b, i: (b, i, 0)),
                  _resident((1, d)), _resident(wq.shape), _resident(wkvt.shape)],
        out_specs=[pl.BlockSpec((1, n_heads, tm, HEAD_DIM), lambda b, i: (b, 0, i, 0)),
                   pl.BlockSpec((1, n_heads, HEAD_DIM, tm), lambda b, i: (b, 0, 0, i)),
                   pl.BlockSpec((1, n_heads, HEAD_DIM, tm), lambda b, i: (b, 0, 0, i))],
        out_shape=[jax.ShapeDtypeStruct((bx, n_heads, t, HEAD_DIM), BF16),
                   jax.ShapeDtypeStruct((bx, n_heads, HEAD_DIM, t), F32),
                   jax.ShapeDtypeStruct((bx, n_heads, HEAD_DIM, t), F32)],
        compiler_params=_cparams(("parallel", "parallel")),
        name="proj_a",
    )(x, g, wq, wkvt)


SB_DEAD = -104.0
SB_WINDOW_BLOCKS = 3
SB_SAMPLE_WINDOW = 256


def _suffix_neg_ones(n):
    r = lax.broadcasted_iota(jnp.int32, (n, n), 0)
    c = lax.broadcasted_iota(jnp.int32, (n, n), 1)
    return jnp.where(r >= c, -1.0, 0.0).astype(BF16)


def _strict_mask(n):
    row = lax.broadcasted_iota(jnp.int32, (n, n), 0)
    col = lax.broadcasted_iota(jnp.int32, (n, n), 1)
    return col < row


def _softplus(z):
    return jnp.maximum(z, 0.0) + jnp.log(1.0 + jnp.exp2(jnp.abs(z) * -LOG2E))


def _sb_strip(q, kt, vt, carry, acc, uu, masks):
    blk = uu.shape[1]
    nb = kt.shape[1] // blk
    z = jnp.dot(q, kt, preferred_element_type=F32)
    sp = _softplus(z)
    ws = [None] * nb
    for b in reversed(range(nb)):
        sl = slice(b * blk, (b + 1) * blk)
        spb = sp[:, sl]
        if masks[b] is not None:
            spb = jnp.where(masks[b], spb, 0.0)
        suffix = jnp.dot(spb.astype(BF16), uu, preferred_element_type=F32)
        wb = jnp.exp(z[:, sl] + suffix + carry)
        if masks[b] is not None:
            wb = jnp.where(masks[b], wb, 0.0)
        ws[b] = wb.astype(BF16)
        carry = carry + suffix[:, 0:1]
    w = ws[0] if nb == 1 else jnp.concatenate(ws, axis=-1)
    acc = acc + lax.dot_general(w, vt, _NT, preferred_element_type=F32)
    return carry, acc


SB_STAGES = 5


def _sb_prompt_fast_kernel(q_ref, kt_ref, vt_ref, o_ref, flag_ref, z_ref, sp_ref, suf_ref, w_ref, *, heads):
    t = q_ref.shape[2]
    blk = SB_BLOCK
    nq = t // blk
    win = SB_WINDOW_BLOCKS
    wl = win * blk
    ring = z_ref.shape[0]
    uu = _suffix_neg_ones(blk)
    col_minus_row = (lax.broadcasted_iota(jnp.int32, (blk, wl), 1)
                     - lax.broadcasted_iota(jnp.int32, (blk, wl), 0))
    z_ref[...] = jnp.zeros(z_ref.shape, F32)
    sp_ref[...] = jnp.zeros(sp_ref.shape, BF16)
    suf_ref[...] = jnp.zeros(suf_ref.shape, F32)
    w_ref[...] = jnp.zeros(w_ref.shape, BF16)

    def where_is(n):
        i = jnp.clip(n, 0, nq - 1)
        k = jnp.maximum(i - (win - 1), 0)
        q0 = pl.multiple_of(i * blk, blk)
        k0 = pl.multiple_of(k * blk, blk)
        visible = col_minus_row < (i - k) * blk
        return q0, k0, visible, lax.rem(n + ring * SB_STAGES, ring)

    def body(n, worst):
        q0, k0, _, _ = where_is(n - 4)
        accs = []
        for h in range(heads):
            vt = vt_ref[0, h, :, pl.ds(k0, wl)].astype(BF16)
            accs.append(lax.dot_general(w_ref[h], vt, _NT, preferred_element_type=F32))
        o_ref[0, pl.ds(q0, blk), :] = jnp.concatenate(accs, axis=-1).astype(BF16)

        _, _, visible, slot = where_is(n - 3)
        counts = n - 3 >= win
        for h in range(heads):
            carry = jnp.zeros((blk, 1), F32)
            for b in reversed(range(win)):
                sl = slice(b * blk, (b + 1) * blk)
                suffix = suf_ref[h, b]
                wb = jnp.exp(z_ref[slot, h, :, sl] + suffix + carry)
                w_ref[h, :, sl] = jnp.where(visible[:, sl], wb, 0.0).astype(BF16)
                carry = carry + suffix[:, 0:1]
            worst = jnp.maximum(worst, jnp.where(counts, carry, -jnp.inf))

        for h in range(heads):
            suf_ref[h] = jnp.dot(sp_ref[h].reshape(win * blk, blk), uu,
                                 preferred_element_type=F32).reshape(win, blk, blk)

        _, _, visible, slot = where_is(n - 1)
        for h in range(heads):
            sp = jnp.where(visible, _softplus(z_ref[slot, h]), 0.0).astype(BF16)
            for b in range(win):
                sp_ref[h, b] = sp[:, b * blk:(b + 1) * blk]

        q0, k0, _, slot = where_is(n)
        for h in range(heads):
            kt = kt_ref[0, h, :, pl.ds(k0, wl)].astype(BF16)
            z_ref[slot, h] = jnp.dot(q_ref[0, h, pl.ds(q0, blk), :], kt, preferred_element_type=F32)
        return worst

    worst = lax.fori_loop(0, nq + SB_STAGES - 1, body, jnp.full((blk, 1), -jnp.inf, F32))
    flag_ref[...] = jnp.broadcast_to(jnp.max(worst), flag_ref.shape)


def _sb_prompt_full_kernel(q_ref, kt_ref, vt_ref, o_ref, *, heads):
    t = q_ref.shape[2]
    blk = SB_BLOCK
    uu = _suffix_neg_ones(blk)
    strict = _strict_mask(blk)

    def q_block(i, _):
        q0 = pl.multiple_of(i * blk, blk)
        qs = [q_ref[0, h, pl.ds(q0, blk), :] for h in range(heads)]
        state = []
        for h in range(heads):
            kd = kt_ref[0, h, :, pl.ds(q0, blk)].astype(BF16)
            vd = vt_ref[0, h, :, pl.ds(q0, blk)].astype(BF16)
            state.extend(_sb_strip(qs[h], kd, vd, jnp.zeros((blk, 1), F32),
                                   jnp.zeros((blk, HEAD_DIM), F32), uu, [strict]))

        def k_block(jj, st):
            k0 = pl.multiple_of((i - 1 - jj) * blk, blk)
            new = []
            for h in range(heads):
                kb = kt_ref[0, h, :, pl.ds(k0, blk)].astype(BF16)
                vb = vt_ref[0, h, :, pl.ds(k0, blk)].astype(BF16)
                new.extend(_sb_strip(qs[h], kb, vb, st[2 * h], st[2 * h + 1], uu, [None]))
            return tuple(new)

        state = lax.fori_loop(0, i, k_block, tuple(state))
        o_ref[0, pl.ds(q0, blk), :] = jnp.concatenate(
            [state[2 * h + 1] for h in range(heads)], axis=-1).astype(BF16)
        return 0

    lax.fori_loop(0, t // blk, q_block, 0)


def _sb_prompt(q, kt, vt):
    b, n_heads, t, _ = q.shape
    heads = 4
    wl = SB_WINDOW_BLOCKS * SB_BLOCK
    assert t >= wl
    q_blk = pl.BlockSpec((1, heads, t, HEAD_DIM), lambda bi, hg: (bi, hg, 0, 0))
    kv_blk = pl.BlockSpec((1, heads, HEAD_DIM, t), lambda bi, hg: (bi, hg, 0, 0))
    o_blk = pl.BlockSpec((1, t, heads * HEAD_DIM), lambda bi, hg: (bi, 0, hg))
    o_shape = jax.ShapeDtypeStruct((b, t, n_heads * HEAD_DIM), BF16)
    grid = (b, n_heads // heads)
    o_fast, flags = pl.pallas_call(
        functools.partial(_sb_prompt_fast_kernel, heads=heads),
        grid=grid,
        in_specs=[q_blk, kv_blk, kv_blk],
        out_specs=[o_blk, pl.BlockSpec((1, 1, 8, 128), lambda bi, hg: (bi, hg, 0, 0))],
        out_shape=[o_shape, jax.ShapeDtypeStruct((b, n_heads // heads, 8, 128), F32)],
        scratch_shapes=[pltpu.VMEM((SB_STAGES - 1, heads, SB_BLOCK, wl), F32),
                        pltpu.VMEM((heads, SB_WINDOW_BLOCKS, SB_BLOCK, SB_BLOCK), BF16),
                        pltpu.VMEM((heads, SB_WINDOW_BLOCKS, SB_BLOCK, SB_BLOCK), F32),
                        pltpu.VMEM((heads, SB_BLOCK, wl), BF16)],
        compiler_params=_cparams(("parallel", "parallel")),
        name="sb_prompt_fast",
    )(q, kt, vt)
    if t // SB_BLOCK <= SB_WINDOW_BLOCKS:
        return o_fast

    def full():
        return pl.pallas_call(
            functools.partial(_sb_prompt_full_kernel, heads=heads),
            grid=grid,
            in_specs=[q_blk, kv_blk, kv_blk],
            out_specs=o_blk,
            out_shape=o_shape,
            compiler_params=_cparams(("parallel", "parallel")),
            name="sb_prompt_full",
        )(q, kt, vt)

    return lax.cond(jnp.max(flags) > SB_DEAD, full, lambda: o_fast)


def _sb_sample_kernel(q_ref, ktn_ref, vtn_ref, ktc_ref, vtc_ref, o_ref, *maybe_flag, heads, strip):
    s = q_ref.shape[2]
    width = ktc_ref.shape[4]
    blk = SB_BLOCK
    uu = _suffix_neg_ones(blk)
    uu_new = _suffix_neg_ones(s)
    strict = _strict_mask(s)

    qs = [q_ref[0, h] for h in range(heads)]
    state = []
    for h in range(heads):
        state.extend(_sb_strip(qs[h], ktn_ref[0, h].astype(BF16), vtn_ref[0, h].astype(BF16),
                               jnp.zeros((s, 1), F32), jnp.zeros((s, HEAD_DIM), F32), uu_new, [strict]))

    def k_strip(jj, st):
        k0 = pl.multiple_of(width - (jj + 1) * strip * blk, blk)
        new = []
        for h in range(heads):
            kb = ktc_ref[0, 0, h, :, pl.ds(k0, strip * blk)].astype(BF16)
            vb = vtc_ref[0, 0, h, :, pl.ds(k0, strip * blk)].astype(BF16)
            new.extend(_sb_strip(qs[h], kb, vb, st[2 * h], st[2 * h + 1], uu, [None] * strip))
        return tuple(new)

    state = lax.fori_loop(0, width // (strip * blk), k_strip, tuple(state))
    o_ref[0] = jnp.concatenate([state[2 * h + 1] for h in range(heads)], axis=-1).astype(BF16)
    if maybe_flag:
        worst = state[0]
        for h in range(1, heads):
            worst = jnp.maximum(worst, state[2 * h])
        maybe_flag[0][...] = jnp.broadcast_to(jnp.max(worst), maybe_flag[0].shape)


def _sb_sample(q, kt_new, vt_new, cache_kt, cache_vt, layer):
    b, n_heads, s, _ = q.shape
    past = cache_kt.shape[4]
    heads = 4
    width = min(past, SB_SAMPLE_WINDOW)
    assert past % width == 0 and width % SB_BLOCK == 0
    grid = (b, n_heads // heads)
    q_blk = pl.BlockSpec((1, heads, s, HEAD_DIM), lambda bi, hg: (bi, hg, 0, 0))
    new_blk = pl.BlockSpec((1, heads, HEAD_DIM, s), lambda bi, hg: (bi, hg, 0, 0))
    last = past // width - 1
    win_blk = pl.BlockSpec((1, 1, heads, HEAD_DIM, width), lambda bi, hg: (layer, bi, hg, 0, last))
    o_blk = pl.BlockSpec((1, s, heads * HEAD_DIM), lambda bi, hg: (bi, 0, hg))
    o_shape = jax.ShapeDtypeStruct((b, s, n_heads * HEAD_DIM), BF16)
    o_fast, flags = pl.pallas_call(
        functools.partial(_sb_sample_kernel, heads=heads, strip=width // SB_BLOCK),
        grid=grid,
        in_specs=[q_blk, new_blk, new_blk, win_blk, win_blk],
        out_specs=[o_blk, pl.BlockSpec((1, 1, 8, 128), lambda bi, hg: (bi, hg, 0, 0))],
        out_shape=[o_shape, jax.ShapeDtypeStruct((b, n_heads // heads, 8, 128), F32)],
        compiler_params=_cparams(("parallel", "parallel")),
        name="sb_sample_fast",
    )(q, kt_new, vt_new, cache_kt, cache_vt)
    if width == past:
        return o_fast

    def full():
        full_blk = pl.BlockSpec((1, 1, heads, HEAD_DIM, past), lambda bi, hg: (layer, bi, hg, 0, 0))
        return pl.pallas_call(
            functools.partial(_sb_sample_kernel, heads=heads, strip=1),
            grid=grid,
            in_specs=[q_blk, new_blk, new_blk, full_blk, full_blk],
            out_specs=o_blk,
            out_shape=o_shape,
            compiler_params=_cparams(("parallel", "parallel")),
            name="sb_sample_full",
        )(q, kt_new, vt_new, cache_kt, cache_vt)

    return lax.cond(jnp.max(flags) > SB_DEAD, full, lambda: o_fast)


def _route(logits):
    tm = logits.shape[0]
    lane = lax.broadcasted_iota(jnp.int32, (tm, ROUTER_LANES), 1)
    lane_f = lane.astype(F32)
    neg = -jnp.inf
    first = lambda hit: jnp.min(jnp.where(hit, lane_f, float(ROUTER_LANES)), axis=-1, keepdims=True)

    gl = jnp.where(lane < N_GROUPS, logits, neg)
    g_max = jnp.max(gl, axis=-1, keepdims=True)
    g_top = 1.0 / jnp.sum(jnp.exp(gl - g_max), axis=-1, keepdims=True)
    g_idx = first(gl == g_max)

    lo = N_GROUPS + g_idx * EXPERTS_PER_GROUP
    in_group = (lane_f >= lo) & (lane_f < lo + EXPERTS_PER_GROUP)
    sel = jnp.where(in_group, logits, neg)
    t1 = jnp.max(sel, axis=-1, keepdims=True)
    i1 = first(sel == t1)
    sel2 = jnp.where(lane_f == i1, neg, sel)
    t2 = jnp.max(sel2, axis=-1, keepdims=True)
    i2 = first(sel2 == t2)
    e2 = jnp.exp(t2 - t1)
    den = 1.0 + e2
    w1 = (1.0 / den) * g_top
    w2 = (e2 / den) * g_top
    return jnp.where(lane_f == i1, w1, 0.0) + jnp.where(lane_f == i2, w2, 0.0)


def _channel_kernel(x_ref, o_ref, p_ref, wo_ref, gffn_ref, wr_ref, wgu_ref, wd_ref, gple_ref,
                    wpg_ref, wpp_ref, out_ref, acc_ref):
    d_expert = wd_ref.shape[1]
    h1 = x_ref[...] + jnp.dot(o_ref[...], wo_ref[...], preferred_element_type=F32)
    xn = _rms_unit(h1) * gffn_ref[...]
    xnb = xn.astype(BF16)
    xlo = (xn - xnb.astype(F32)).astype(BF16)
    logits = jnp.dot(jnp.concatenate([xnb, xlo, xnb], axis=-1), wr_ref[...], preferred_element_type=F32)
    gates = _route(logits)
    lane = lax.broadcasted_iota(jnp.int32, gates.shape, 1)
    acc_ref[...] = h1

    def expert(e, _):
        gate_e = jnp.sum(jnp.where(lane == N_GROUPS + e, gates, 0.0), axis=-1, keepdims=True)
        gu = jnp.dot(xnb, wgu_ref[e], preferred_element_type=F32)
        g = gu[:, :d_expert]
        u = gu[:, d_expert:]
        hid = (g * jax.nn.sigmoid(g)) * u * gate_e
        acc_ref[...] += jnp.dot(hid.astype(BF16), wd_ref[e], preferred_element_type=F32)
        return 0

    lax.fori_loop(0, N_EXPERTS, expert, 0)
    h2 = acc_ref[...]
    x3 = (_rms_unit(h2) * gple_ref[...]).astype(BF16)
    gate = jax.nn.sigmoid(jnp.dot(x3, wpg_ref[...], preferred_element_type=F32))
    proj = jnp.dot(p_ref[...].astype(BF16), wpp_ref[...], preferred_element_type=F32)
    out_ref[...] = h2 + proj * gate


def _channel(x, o, p, w):
    n, d = x.shape
    tm = min(n, 512)
    row = lambda cols: pl.BlockSpec((tm, cols), lambda i: (i, 0))
    weights = [w["wo"], w["gffn"], w["wr"], w["wgu"], w["wd"], w["gple"], w["wpg"], w["wpp"]]
    return pl.pallas_call(
        _channel_kernel,
        grid=(n // tm,),
        in_specs=[row(d), row(d), row(p.shape[1])] + [_resident(a.shape) for a in weights],
        out_specs=row(d),
        out_shape=jax.ShapeDtypeStruct((n, d), F32),
        scratch_shapes=[pltpu.VMEM((tm, d), F32)],
        compiler_params=_cparams(("parallel",)),
        name="channel",
    )(x, o, p, *weights)


def _proj_b_kernel(h_ref, ga_ref, wq_ref, gq_ref, gkv_ref, wkv_ref, gk_ref, gsum_ref, gexp_ref,
                   q_ref, k_ref, v_ref, *, n_heads, n_kv):
    y = _rms_unit(h_ref[0])
    q = jnp.dot((y * ga_ref[...]).astype(BF16), wq_ref[...], preferred_element_type=F32)
    ms = jnp.dot(_split_hl(q * q), gsum_ref[...], preferred_element_type=F32) * (1.0 / HEAD_DIM)
    inv = lax.rsqrt(ms + EPS)
    inv_full = jnp.dot(_split_hl(inv), gexp_ref[...], preferred_element_type=F32)
    qn = (q * inv_full * gq_ref[...]) * (HEAD_DIM ** -0.5)
    for h in range(n_heads):
        q_ref[0, h] = qn[:, h * HEAD_DIM:(h + 1) * HEAD_DIM].astype(BF16)
    kv = jnp.dot((y * gkv_ref[...]).astype(BF16), wkv_ref[...], preferred_element_type=F32)
    for h in range(n_kv):
        kh = kv[:, h * HEAD_DIM:(h + 1) * HEAD_DIM]
        k_ref[0, h] = _rms_unit(kh) * gk_ref[...]
        v_ref[0, h] = kv[:, (n_kv + h) * HEAD_DIM:(n_kv + h + 1) * HEAD_DIM]


def _proj_b(h, ga, wq, gq, gkv, wkv, gk, gsum, gexp):
    bx, t, d = h.shape
    n_heads = wq.shape[1] // HEAD_DIM
    n_kv = wkv.shape[1] // (2 * HEAD_DIM)
    tm = min(t, 256)
    hm = lambda b, i: (b, 0, i, 0)
    ins = [ga, wq, gq, gkv, wkv, gk, gsum, gexp]
    return pl.pallas_call(
        functools.partial(_proj_b_kernel, n_heads=n_heads, n_kv=n_kv),
        grid=(bx, t // tm),
        in_specs=[pl.BlockSpec((1, tm, d), lambda b, i: (b, i, 0))] + [_resident(a.shape) for a in ins],
        out_specs=[pl.BlockSpec((1, n_heads, tm, HEAD_DIM), hm),
                   pl.BlockSpec((1, n_kv, tm, HEAD_DIM), hm),
                   pl.BlockSpec((1, n_kv, tm, HEAD_DIM), hm)],
        out_shape=[jax.ShapeDtypeStruct((bx, n_heads, t, HEAD_DIM), BF16),
                   jax.ShapeDtypeStruct((bx, n_kv, t, HEAD_DIM), F32),
                   jax.ShapeDtypeStruct((bx, n_kv, t, HEAD_DIM), F32)],
        compiler_params=_cparams(("parallel", "parallel")),
        name="proj_b",
    )(h, *ins)


def _t5_bucket(rel):
    nb = NUM_BUCKETS // 2
    max_exact = nb // 2
    n = jnp.abs(rel)
    large = max_exact + (jnp.log(jnp.maximum(n, 1).astype(jnp.float32) / max_exact)
                         / math.log(MAX_DISTANCE / max_exact) * (nb - max_exact)).astype(jnp.int32)
    large = jnp.minimum(large, nb - 1)
    return jnp.where(rel > 0, nb, 0) + jnp.where(n < max_exact, n, large)


def _bias_kernel(bucket_ref, rbt_ref, out_ref):
    bucket = bucket_ref[...]
    acc = jnp.zeros(out_ref.shape, F32)
    for b in range(NUM_BUCKETS):
        acc = acc + jnp.where(bucket == b, rbt_ref[:, b:b + 1], 0.0)
    out_ref[...] = acc


def _bias_table(rel_bias, nq, nk, key_offset):
    rel = (jnp.arange(nk, dtype=jnp.int32)[None, :] - key_offset) - jnp.arange(nq, dtype=jnp.int32)[:, None]
    bucket = _t5_bucket(rel).reshape(1, nq * nk)
    n_heads = rel_bias.shape[1]
    out = pl.pallas_call(
        _bias_kernel,
        out_shape=jax.ShapeDtypeStruct((n_heads, nq * nk), F32),
        name="bias_table",
    )(bucket, rel_bias.T)
    return out.reshape(n_heads, nq, nk)


def _swa_kernel(sink_ref, q_ref, k_ref, v_ref, bias_ref, o_ref, *, cq, wl, pad, n_chunks, group):
    n_kv = k_ref.shape[1]

    def chunk(c, _):
        r0 = pl.multiple_of(c * cq, cq)
        col = lax.broadcasted_iota(jnp.int32, (cq, wl), 1) + r0
        valid = col >= pad
        for kv in range(n_kv):
            kw = k_ref[0, kv, pl.ds(r0, wl), :].astype(BF16)
            vw = v_ref[0, kv, pl.ds(r0, wl), :].astype(BF16)
            qg = q_ref[0, kv * group:(kv + 1) * group, pl.ds(r0, cq), :].reshape(group * cq, HEAD_DIM)
            logits = lax.dot_general(qg, kw, _NT, preferred_element_type=F32)
            es, dens = [], []
            for g in range(group):
                h = kv * group + g
                l = logits[g * cq:(g + 1) * cq] + bias_ref[h]
                if pad:
                    l = jnp.where(valid, l, -jnp.inf)
                sink = sink_ref[h]
                m = jnp.maximum(jnp.max(l, axis=-1, keepdims=True), sink)
                e = jnp.exp(l - m)
                dens.append(jnp.sum(e, axis=-1, keepdims=True) + jnp.exp(sink - m))
                es.append(e.astype(BF16))
            pv = jnp.dot(jnp.concatenate(es, axis=0), vw, preferred_element_type=F32)
            outs = [pv[g * cq:(g + 1) * cq] / dens[g] for g in range(group)]
            o_ref[0, pl.ds(r0, cq), kv * group * HEAD_DIM:(kv + 1) * group * HEAD_DIM] = (
                jnp.concatenate(outs, axis=-1).astype(BF16))
        return 0

    lax.fori_loop(0, n_chunks, chunk, 0)


def _swa(q, k_win, v_win, bias, sinks, *, cq, wl, pad):
    b, n_heads, tq, _ = q.shape
    n_kv, tk = k_win.shape[1], k_win.shape[2]
    n_chunks = tq // cq
    assert (n_chunks - 1) * cq + wl == tk
    kv_blk = pl.BlockSpec((1, n_kv, tk, HEAD_DIM), lambda bi: (bi, 0, 0, 0))
    return pl.pallas_call(
        functools.partial(_swa_kernel, cq=cq, wl=wl, pad=pad, n_chunks=n_chunks, group=n_heads // n_kv),
        grid=(b,),
        in_specs=[pl.BlockSpec(memory_space=pltpu.SMEM),
                  pl.BlockSpec((1, n_heads, tq, HEAD_DIM), lambda bi: (bi, 0, 0, 0)),
                  kv_blk, kv_blk, _resident(bias.shape)],
        out_specs=pl.BlockSpec((1, tq, n_heads * HEAD_DIM), lambda bi: (bi, 0, 0)),
        out_shape=jax.ShapeDtypeStruct((b, tq, n_heads * HEAD_DIM), BF16),
        compiler_params=_cparams(("parallel",)),
        name="swa",
    )(sinks, q, k_win, v_win, bias)


def _row(v):
    return v.reshape(1, -1).astype(F32)


def _router_hi_hi_lo(wr):
    hi = wr.astype(BF16)
    lo = (wr - hi.astype(F32)).astype(BF16)
    return jnp.concatenate([hi, hi, lo], axis=0)


def _prep_weights(prm):
    d = prm["a_w_o"].shape[1]
    depth = prm["norm_ffn"].shape[0]
    n_a = prm["a_w_qkv"].shape[0]
    scale = HEAD_DIM ** -0.5
    w = {"channel": [], "n_a": n_a, "depth": depth}
    for i in range(depth):
        wo = prm["a_w_o"][i] if i < n_a else prm["b_w_o"][i - n_a]
        pad = ROUTER_LANES - N_GROUPS - N_EXPERTS
        wr = jnp.concatenate([prm["moe_w_group"][i], prm["moe_w_router"][i].reshape(d, N_EXPERTS),
                              jnp.zeros((d, pad), F32)], axis=1)
        w["channel"].append({
            "wo": wo.astype(BF16), "gffn": _row(prm["norm_ffn"][i]), "wr": _router_hi_hi_lo(wr),
            "wgu": jnp.concatenate([prm["moe_w_gate"][i], prm["moe_w_up"][i]], axis=-1).astype(BF16),
            "wd": prm["moe_w_down"][i].astype(BF16), "gple": _row(prm["norm_ple"][i]),
            "wpg": prm["ple_w_gate"][i].astype(BF16), "wpp": prm["ple_w_proj"][i].astype(BF16)})
    w["qkv"] = []
    for i in range(n_a):
        wq = prm["a_w_qkv"][i]
        hd = wq.shape[1] // 3
        w["qkv"].append(((wq[:, :hd] * scale).astype(BF16),
                         wq[:, hd:].T.astype(BF16)))
    n_heads = prm["b_w_q"].shape[2] // HEAD_DIM
    head_of = jnp.arange(n_heads * HEAD_DIM, dtype=jnp.int32) // HEAD_DIM
    lanes = jnp.arange(ROUTER_LANES, dtype=jnp.int32)
    member = (head_of[:, None] == lanes[None, :]).astype(BF16)
    w["gsum"] = jnp.concatenate([member, member], axis=0)
    w["gexp"] = jnp.concatenate([member.T, member.T], axis=0)
    w["wq_b"] = [prm["b_w_q"][j].astype(BF16) for j in range(depth - n_a)]
    w["gq_b"] = [_row(jnp.tile(prm["b_q_norm"][j], n_heads)) for j in range(depth - n_a)]
    w["wkv"] = prm["b_w_kv"].astype(BF16)
    return w


def _assert_sample_window_visible(past_len, s, tk):
    q_chunk = [(past_len + i) // CHUNK for i in range(s)]
    k_pos = [past_len + s - tk + j for j in range(tk)]
    ok = all(kp >= 0 and qc - WIN_CHUNKS <= kp // CHUNK <= qc for qc in q_chunk for kp in k_pos)
    if not ok:
        raise NotImplementedError("sample window with masked keys")


def _run_trunk(x, p, prm, w, sb_cache_k=None, sb_cache_v=None, swa_cache_k=None, swa_cache_v=None):
    bx, t, d = x.shape
    n_a, depth = w["n_a"], w["depth"]
    sample = sb_cache_k is not None
    h = x
    sb_k, sb_v = [], []
    k_win = v_win = None
    q_b = None
    for i in range(depth):
        if i < n_a:
            q, kt, vt = _proj_a(h, _row(prm["norm_attn"][i]), *w["qkv"][i])
            sb_k.append(kt)
            sb_v.append(vt)
            if sample:
                o = _sb_sample(q, kt, vt, jnp.swapaxes(sb_cache_k, -1, -2), jnp.swapaxes(sb_cache_v, -1, -2), i)
            else:
                o = _sb_prompt(q, kt, vt)
        else:
            j = i - n_a
            if j > 0:
                raise NotImplementedError("one B layer supported")
            if sample:
                tk = k_win.shape[2]
                _assert_sample_window_visible(sb_cache_k.shape[3], t, tk)
                o = _swa(q_b, k_win, v_win, _bias_table(prm["rel_bias"], t, tk, tk - t),
                         prm["b_sinks"][j], cq=t, wl=tk, pad=0)
            else:
                front = ((0, 0), (0, 0), (WINDOW, 0), (0, 0))
                wl = WINDOW + CHUNK
                o = _swa(q_b, jnp.pad(k_win, front), jnp.pad(v_win, front),
                         _bias_table(prm["rel_bias"], CHUNK, wl, WINDOW), prm["b_sinks"][j],
                         cq=CHUNK, wl=wl, pad=WINDOW)
        h = _channel(h.reshape(bx * t, d), o.reshape(bx * t, d), p[i].reshape(bx * t, -1),
                     w["channel"][i]).reshape(bx, t, d)
        if i == n_a - 1:
            q_b, k_s, v_s = _proj_b(h, _row(prm["norm_attn"][n_a]), w["wq_b"][0], w["gq_b"][0],
                                    _row(prm["kv_norm"]), w["wkv"], _row(prm["b_k_norm"]),
                                    w["gsum"], w["gexp"])
            if sample:
                k_win = jnp.concatenate([swa_cache_k, k_s], axis=2)
                v_win = jnp.concatenate([swa_cache_v, v_s], axis=2)
            else:
                k_win, v_win = k_s, v_s
    sb_k = jnp.swapaxes(jnp.stack(sb_k), -1, -2)
    sb_v = jnp.swapaxes(jnp.stack(sb_v), -1, -2)
    return h, sb_k, sb_v, k_win[:, :, -WINDOW:], v_win[:, :, -WINDOW:]


def kernel(x_prompt, x_sample, p_prompt, p_sample, cache_sb_k, cache_sb_v, cache_swa_k, cache_swa_v, norm_attn, norm_ffn, norm_ple, a_w_qkv, a_w_o, kv_norm, b_w_kv, b_k_norm, b_w_q, b_q_norm, b_sinks, b_w_o, rel_bias, moe_w_group, moe_w_router, moe_w_gate, moe_w_up, moe_w_down, ple_w_proj, ple_w_gate):
    prm = {
        "norm_attn": norm_attn, "norm_ffn": norm_ffn, "norm_ple": norm_ple,
        "a_w_qkv": a_w_qkv, "a_w_o": a_w_o, "kv_norm": kv_norm, "b_w_kv": b_w_kv, "b_k_norm": b_k_norm,
        "b_w_q": b_w_q, "b_q_norm": b_q_norm, "b_sinks": b_sinks, "b_w_o": b_w_o, "rel_bias": rel_bias,
        "moe_w_group": moe_w_group, "moe_w_router": moe_w_router, "moe_w_gate": moe_w_gate,
        "moe_w_up": moe_w_up, "moe_w_down": moe_w_down, "ple_w_proj": ple_w_proj, "ple_w_gate": ple_w_gate,
    }
    w = _prep_weights(prm)
    y_p, sb_k_p, sb_v_p, swa_k_p, swa_v_p = _run_trunk(x_prompt, p_prompt, prm, w)
    y_s, sb_k_s, sb_v_s, swa_k_s, swa_v_s = _run_trunk(x_sample, p_sample, prm, w, cache_sb_k, cache_sb_v,
                                                       cache_swa_k, cache_swa_v)
    return (y_p, y_s, sb_k_p, sb_v_p, swa_k_p, swa_v_p, sb_k_s, sb_v_s, swa_k_s, swa_v_s)
```

```python
import functools
import math

import jax
import jax.numpy as jnp
from jax import lax
from jax.experimental import pallas as pl
from jax.experimental.pallas import tpu as pltpu

F32 = jnp.float32
BF16 = jnp.bfloat16

HEAD_DIM = 64
CHUNK = 64
WINDOW = 128
WIN_CHUNKS = WINDOW // CHUNK
NUM_BUCKETS = 32
MAX_DISTANCE = 128
N_GROUPS = 4
EXPERTS_PER_GROUP = 4
N_EXPERTS = N_GROUPS * EXPERTS_PER_GROUP
EPS = 1e-6
LOG2E = 1.4426950408889634
SB_BLOCK = 128
ROUTER_LANES = 128
VMEM_LIMIT = 56 * 1024 * 1024

_NT = (((1,), (1,)), ((), ()))


def _cparams(sem):
    return pltpu.CompilerParams(dimension_semantics=sem, vmem_limit_bytes=VMEM_LIMIT)


def _rms_unit(x):
    return x * lax.rsqrt(jnp.mean(x * x, axis=-1, keepdims=True) + EPS)


def _split_hl(a):
    hi = a.astype(BF16)
    lo = (a - hi.astype(F32)).astype(BF16)
    return jnp.concatenate([hi, lo], axis=-1)


def _resident(shape):
    nd = len(shape)
    return pl.BlockSpec(shape, lambda *_: (0,) * nd, pipeline_mode=pl.Buffered(1))


def _proj_a_kernel(x_ref, g_ref, wq_ref, wkvt_ref, q_ref, kt_ref, vt_ref, *, n_heads):
    xn = (_rms_unit(x_ref[0]) * g_ref[...]).astype(BF16)
    heads_per_dot = 4
    width = heads_per_dot * HEAD_DIM
    for c in range(n_heads // heads_per_dot):
        r = jnp.dot(xn, wq_ref[:, c * width:(c + 1) * width], preferred_element_type=F32)
        for hh in range(heads_per_dot):
            q_ref[0, c * heads_per_dot + hh] = r[:, hh * HEAD_DIM:(hh + 1) * HEAD_DIM].astype(BF16)
    for c in range(2 * n_heads // heads_per_dot):
        r = lax.dot_general(wkvt_ref[c * width:(c + 1) * width, :], xn, _NT, preferred_element_type=F32)
        for hh in range(heads_per_dot):
            h = c * heads_per_dot + hh
            piece = r[hh * HEAD_DIM:(hh + 1) * HEAD_DIM, :]
            if h < n_heads:
                kt_ref[0, h] = piece
            else:
                vt_ref[0, h - n_heads] = piece


def _proj_a(x, g, wq, wkvt):
    bx, t, d = x.shape
    n_heads = wq.shape[1] // HEAD_DIM
    tm = min(t, 256)
    return pl.pallas_call(
        functools.partial(_proj_a_kernel, n_heads=n_heads),
        grid=(bx, t // tm),
        in_specs=[pl.BlockSpec((1, tm, d), lambda b, i: (b, i, 0)),
                  _resident((1, d)), _resident(wq.shape), _resident(wkvt.shape)],
        out_specs=[pl.BlockSpec((1, n_heads, tm, HEAD_DIM), lambda b, i: (b, 0, i, 0)),
                   pl.BlockSpec((1, n_heads, HEAD_DIM, tm), lambda b, i: (b, 0, 0, i)),
                   pl.BlockSpec((1, n_heads, HEAD_DIM, tm), lambda b, i: (b, 0, 0, i))],
        out_shape=[jax.ShapeDtypeStruct((bx, n_heads, t, HEAD_DIM), BF16),
                   jax.ShapeDtypeStruct((bx, n_heads, HEAD_DIM, t), F32),
                   jax.ShapeDtypeStruct((bx, n_heads, HEAD_DIM, t), F32)],
        compiler_params=_cparams(("parallel", "parallel")),
        name="proj_a",
    )(x, g, wq, wkvt)


SB_DEAD = -104.0
SB_WINDOW_BLOCKS = 3
SB_SAMPLE_WINDOW = 256


def _suffix_neg_ones(n):
    r = lax.broadcasted_iota(jnp.int32, (n, n), 0)
    c = lax.broadcasted_iota(jnp.int32, (n, n), 1)
    return jnp.where(r >= c, -1.0, 0.0).astype(BF16)


def _strict_mask(n):
    row = lax.broadcasted_iota(jnp.int32, (n, n), 0)
    col = lax.broadcasted_iota(jnp.int32, (n, n), 1)
    return col < row


def _softplus(z):
    return jnp.maximum(z, 0.0) + jnp.log(1.0 + jnp.exp2(jnp.abs(z) * -LOG2E))


def _sb_strips(qs, kts, vts, carries, accs, uu, masks):
    n = len(qs)
    blk = uu.shape[1]
    nb = kts[0].shape[1] // blk
    zs = [jnp.dot(qs[i], kts[i], preferred_element_type=F32) for i in range(n)]
    sps = [_softplus(z) for z in zs]
    ws = [[None] * nb for _ in range(n)]
    carries = list(carries)
    for b in reversed(range(nb)):
        sl = slice(b * blk, (b + 1) * blk)
        spbs = [sp[:, sl] if masks[b] is None else jnp.where(masks[b], sp[:, sl], 0.0) for sp in sps]
        sufs = [jnp.dot(spb.astype(BF16), uu, preferred_element_type=F32) for spb in spbs]
        for i in range(n):
            wb = jnp.exp(zs[i][:, sl] + sufs[i] + carries[i])
            if masks[b] is not None:
                wb = jnp.where(masks[b], wb, 0.0)
            ws[i][b] = wb.astype(BF16)
            carries[i] = carries[i] + sufs[i][:, 0:1]
    wcat = [w[0] if nb == 1 else jnp.concatenate(w, axis=-1) for w in ws]
    accs = [accs[i] + lax.dot_general(wcat[i], vts[i], _NT, preferred_element_type=F32) for i in range(n)]
    return carries, accs


SB_STAGES = 5


def _sb_prompt_fast_kernel(q_ref, kt_ref, vt_ref, o_ref, flag_ref, z_ref, sp_ref, suf_ref, w_ref, *, heads):
    t = q_ref.shape[2]
    blk = SB_BLOCK
    nq = t // blk
    win = SB_WINDOW_BLOCKS
    wl = win * blk
    ring = z_ref.shape[0]
    first = win - 1
    uu = _suffix_neg_ones(blk)
    strict = _strict_mask(blk)
    z_ref[...] = jnp.zeros(z_ref.shape, F32)
    sp_ref[...] = jnp.zeros(sp_ref.shape, BF16)
    suf_ref[...] = jnp.zeros(suf_ref.shape, F32)
    w_ref[...] = jnp.zeros(w_ref.shape, BF16)

    for i in range(first):
        _, accs = _sb_strips([q_ref[0, h, i * blk:(i + 1) * blk, :] for h in range(heads)],
                             [kt_ref[0, h, :, 0:(i + 1) * blk].astype(BF16) for h in range(heads)],
                             [vt_ref[0, h, :, 0:(i + 1) * blk].astype(BF16) for h in range(heads)],
                             [jnp.zeros((blk, 1), F32)] * heads, [jnp.zeros((blk, HEAD_DIM), F32)] * heads,
                             uu, [None] * i + [strict])
        o_ref[0, i * blk:(i + 1) * blk, :] = jnp.concatenate(accs, axis=-1).astype(BF16)

    def where_is(n):
        i = jnp.clip(first + n, first, nq - 1)
        q0 = pl.multiple_of(i * blk, blk)
        k0 = pl.multiple_of((i - first) * blk, blk)
        return q0, k0, lax.rem(n + ring * SB_STAGES, ring)

    def body(n, worst):
        q0, k0, _ = where_is(n - 4)
        accs = []
        for h in range(heads):
            vt = vt_ref[0, h, :, pl.ds(k0, wl)].astype(BF16)
            accs.append(lax.dot_general(w_ref[h], vt, _NT, preferred_element_type=F32))
        o_ref[0, pl.ds(q0, blk), :] = jnp.concatenate(accs, axis=-1).astype(BF16)

        _, _, slot = where_is(n - 3)
        counts = n - 3 >= 1
        for h in range(heads):
            carry = jnp.zeros((blk, 1), F32)
            for b in reversed(range(win)):
                sl = slice(b * blk, (b + 1) * blk)
                suffix = suf_ref[h, b]
                wb = jnp.exp(z_ref[slot, h, :, sl] + suffix + carry)
                if b == win - 1:
                    wb = jnp.where(strict, wb, 0.0)
                w_ref[h, :, sl] = wb.astype(BF16)
                carry = carry + suffix[:, 0:1]
            worst = jnp.maximum(worst, jnp.where(counts, carry, -jnp.inf))

        for h in range(heads):
            suf_ref[h] = jnp.dot(sp_ref[h].reshape(win * blk, blk), uu,
                                 preferred_element_type=F32).reshape(win, blk, blk)

        _, _, slot = where_is(n - 1)
        for h in range(heads):
            sp = _softplus(z_ref[slot, h])
            for b in range(win):
                spb = sp[:, b * blk:(b + 1) * blk]
                if b == win - 1:
                    spb = jnp.where(strict, spb, 0.0)
                sp_ref[h, b] = spb.astype(BF16)

        q0, k0, slot = where_is(n)
        for h in range(heads):
            kt = kt_ref[0, h, :, pl.ds(k0, wl)].astype(BF16)
            z_ref[slot, h] = jnp.dot(q_ref[0, h, pl.ds(q0, blk), :], kt, preferred_element_type=F32)
        return worst

    worst = lax.fori_loop(0, nq - first + SB_STAGES - 1, body, jnp.full((blk, 1), -jnp.inf, F32))
    flag_ref[...] = jnp.broadcast_to(jnp.max(worst), flag_ref.shape)


def _sb_prompt_full_kernel(q_ref, kt_ref, vt_ref, o_ref, *, heads):
    t = q_ref.shape[2]
    blk = SB_BLOCK
    uu = _suffix_neg_ones(blk)
    strict = _strict_mask(blk)

    def q_block(i, _):
        q0 = pl.multiple_of(i * blk, blk)
        qs = [q_ref[0, h, pl.ds(q0, blk), :] for h in range(heads)]

        def strips(k0, carries, accs, masks):
            return _sb_strips(qs, [kt_ref[0, h, :, pl.ds(k0, blk)].astype(BF16) for h in range(heads)],
                              [vt_ref[0, h, :, pl.ds(k0, blk)].astype(BF16) for h in range(heads)],
                              carries, accs, uu, masks)

        state = strips(q0, [jnp.zeros((blk, 1), F32)] * heads, [jnp.zeros((blk, HEAD_DIM), F32)] * heads,
                       [strict])

        def k_block(jj, st):
            carries, accs = strips(pl.multiple_of((i - 1 - jj) * blk, blk), st[0], st[1], [None])
            return tuple(carries), tuple(accs)

        _, accs = lax.fori_loop(0, i, k_block, (tuple(state[0]), tuple(state[1])))
        o_ref[0, pl.ds(q0, blk), :] = jnp.concatenate(list(accs), axis=-1).astype(BF16)
        return 0

    lax.fori_loop(0, t // blk, q_block, 0)


def _sb_prompt(q, kt, vt):
    b, n_heads, t, _ = q.shape
    heads = 4
    wl = SB_WINDOW_BLOCKS * SB_BLOCK
    assert t >= wl
    q_blk = pl.BlockSpec((1, heads, t, HEAD_DIM), lambda bi, hg: (bi, hg, 0, 0))
    kv_blk = pl.BlockSpec((1, heads, HEAD_DIM, t), lambda bi, hg: (bi, hg, 0, 0))
    o_blk = pl.BlockSpec((1, t, heads * HEAD_DIM), lambda bi, hg: (bi, 0, hg))
    o_shape = jax.ShapeDtypeStruct((b, t, n_heads * HEAD_DIM), BF16)
    grid = (b, n_heads // heads)
    o_fast, flags = pl.pallas_call(
        functools.partial(_sb_prompt_fast_kernel, heads=heads),
        grid=grid,
        in_specs=[q_blk, kv_blk, kv_blk],
        out_specs=[o_blk, pl.BlockSpec((1, 1, 8, 128), lambda bi, hg: (bi, hg, 0, 0))],
        out_shape=[o_shape, jax.ShapeDtypeStruct((b, n_heads // heads, 8, 128), F32)],
        scratch_shapes=[pltpu.VMEM((SB_STAGES - 1, heads, SB_BLOCK, wl), F32),
                        pltpu.VMEM((heads, SB_WINDOW_BLOCKS, SB_BLOCK, SB_BLOCK), BF16),
                        pltpu.VMEM((heads, SB_WINDOW_BLOCKS, SB_BLOCK, SB_BLOCK), F32),
                        pltpu.VMEM((heads, SB_BLOCK, wl), BF16)],
        compiler_params=_cparams(("parallel", "parallel")),
        name="sb_prompt_fast",
    )(q, kt, vt)
    if t // SB_BLOCK <= SB_WINDOW_BLOCKS:
        return o_fast

    def full():
        return pl.pallas_call(
            functools.partial(_sb_prompt_full_kernel, heads=heads),
            grid=grid,
            in_specs=[q_blk, kv_blk, kv_blk],
            out_specs=o_blk,
            out_shape=o_shape,
            compiler_params=_cparams(("parallel", "parallel")),
            name="sb_prompt_full",
        )(q, kt, vt)

    return lax.cond(jnp.max(flags) > SB_DEAD, full, lambda: o_fast)


def _sb_sample_kernel(q_ref, ktn_ref, vtn_ref, ktc_ref, vtc_ref, o_ref, *maybe_flag, heads, strip):
    s = q_ref.shape[2]
    width = ktc_ref.shape[4]
    blk = SB_BLOCK
    uu = _suffix_neg_ones(blk)
    uu_new = _suffix_neg_ones(s)
    strict = _strict_mask(s)

    qs = [q_ref[0, h] for h in range(heads)]
    state = _sb_strips(qs, [ktn_ref[0, h].astype(BF16) for h in range(heads)],
                       [vtn_ref[0, h].astype(BF16) for h in range(heads)],
                       [jnp.zeros((s, 1), F32)] * heads, [jnp.zeros((s, HEAD_DIM), F32)] * heads,
                       uu_new, [strict])

    def k_strip(jj, st):
        k0 = pl.multiple_of(width - (jj + 1) * strip * blk, blk)
        carries, accs = _sb_strips(
            qs, [ktc_ref[0, 0, h, :, pl.ds(k0, strip * blk)].astype(BF16) for h in range(heads)],
            [vtc_ref[0, 0, h, :, pl.ds(k0, strip * blk)].astype(BF16) for h in range(heads)],
            st[0], st[1], uu, [None] * strip)
        return tuple(carries), tuple(accs)

    carries, accs = lax.fori_loop(0, width // (strip * blk), k_strip, (tuple(state[0]), tuple(state[1])))
    o_ref[0] = jnp.concatenate(list(accs), axis=-1).astype(BF16)
    if maybe_flag:
        worst = carries[0]
        for h in range(1, heads):
            worst = jnp.maximum(worst, carries[h])
        maybe_flag[0][...] = jnp.broadcast_to(jnp.max(worst), maybe_flag[0].shape)


def _sb_sample(q, kt_new, vt_new, cache_kt, cache_vt, layer):
    b, n_heads, s, _ = q.shape
    past = cache_kt.shape[4]
    width = min(past, SB_SAMPLE_WINDOW)
    assert past % width == 0 and width % SB_BLOCK == 0
    last = past // width - 1
    o_shape = jax.ShapeDtypeStruct((b, s, n_heads * HEAD_DIM), BF16)

    def specs(heads, cache_width, cache_block):
        q_blk = pl.BlockSpec((1, heads, s, HEAD_DIM), lambda bi, hg: (bi, hg, 0, 0))
        new_blk = pl.BlockSpec((1, heads, HEAD_DIM, s), lambda bi, hg: (bi, hg, 0, 0))
        cache_blk = pl.BlockSpec((1, 1, heads, HEAD_DIM, cache_width),
                                 lambda bi, hg: (layer, bi, hg, 0, cache_block))
        o_blk = pl.BlockSpec((1, s, heads * HEAD_DIM), lambda bi, hg: (bi, 0, hg))
        return (b, n_heads // heads), [q_blk, new_blk, new_blk, cache_blk, cache_blk], o_blk

    heads = n_heads
    grid, in_specs, o_blk = specs(heads, width, last)
    o_fast, flags = pl.pallas_call(
        functools.partial(_sb_sample_kernel, heads=heads, strip=width // SB_BLOCK),
        grid=grid,
        in_specs=in_specs,
        out_specs=[o_blk, pl.BlockSpec((1, 1, 8, 128), lambda bi, hg: (bi, hg, 0, 0))],
        out_shape=[o_shape, jax.ShapeDtypeStruct((b, n_heads // heads, 8, 128), F32)],
        compiler_params=_cparams(("parallel", "parallel")),
        name="sb_sample_fast",
    )(q, kt_new, vt_new, cache_kt, cache_vt)
    if width == past:
        return o_fast

    def full():
        heads = 4
        grid, in_specs, o_blk = specs(heads, past, 0)
        return pl.pallas_call(
            functools.partial(_sb_sample_kernel, heads=heads, strip=1),
            grid=grid,
            in_specs=in_specs,
            out_specs=o_blk,
            out_shape=o_shape,
            compiler_params=_cparams(("parallel", "parallel")),
            name="sb_sample_full",
        )(q, kt_new, vt_new, cache_kt, cache_vt)

    return lax.cond(jnp.max(flags) > SB_DEAD, full, lambda: o_fast)


def _route(logits):
    tm = logits.shape[0]
    lane = lax.broadcasted_iota(jnp.int32, (tm, ROUTER_LANES), 1)
    lane_f = lane.astype(F32)
    neg = -jnp.inf
    first = lambda hit: jnp.min(jnp.where(hit, lane_f, float(ROUTER_LANES)), axis=-1, keepdims=True)

    gl = jnp.where(lane < N_GROUPS, logits, neg)
    g_max = jnp.max(gl, axis=-1, keepdims=True)
    g_top = 1.0 / jnp.sum(jnp.exp(gl - g_max), axis=-1, keepdims=True)
    g_idx = first(gl == g_max)

    lo = N_GROUPS + g_idx * EXPERTS_PER_GROUP
    in_group = (lane_f >= lo) & (lane_f < lo + EXPERTS_PER_GROUP)
    sel = jnp.where(in_group, logits, neg)
    t1 = jnp.max(sel, axis=-1, keepdims=True)
    i1 = first(sel == t1)
    sel2 = jnp.where(lane_f == i1, neg, sel)
    t2 = jnp.max(sel2, axis=-1, keepdims=True)
    i2 = first(sel2 == t2)
    e2 = jnp.exp(t2 - t1)
    den = 1.0 + e2
    w1 = (1.0 / den) * g_top
    w2 = (e2 / den) * g_top
    return jnp.where(lane_f == i1, w1, 0.0) + jnp.where(lane_f == i2, w2, 0.0)


def _channel_kernel(x_ref, o_ref, p_ref, wo_ref, gffn_ref, wr_ref, wgu_ref, wd_ref, gple_ref,
                    wpg_ref, wpp_ref, out_ref, acc_ref):
    d_expert = wd_ref.shape[1]
    h1 = x_ref[...] + jnp.dot(o_ref[...], wo_ref[...], preferred_element_type=F32)
    xn = _rms_unit(h1) * gffn_ref[...]
    xnb = xn.astype(BF16)
    xlo = (xn - xnb.astype(F32)).astype(BF16)
    logits = jnp.dot(jnp.concatenate([xnb, xlo, xnb], axis=-1), wr_ref[...], preferred_element_type=F32)
    gates = _route(logits)
    lane = lax.broadcasted_iota(jnp.int32, gates.shape, 1)
    acc_ref[...] = h1

    def expert(e, _):
        gate_e = jnp.sum(jnp.where(lane == N_GROUPS + e, gates, 0.0), axis=-1, keepdims=True)
        gu = jnp.dot(xnb, wgu_ref[e], preferred_element_type=F32)
        g = gu[:, :d_expert]
        u = gu[:, d_expert:]
        hid = (g * jax.nn.sigmoid(g)) * u * gate_e
        acc_ref[...] += jnp.dot(hid.astype(BF16), wd_ref[e], preferred_element_type=F32)
        return 0

    lax.fori_loop(0, N_EXPERTS, expert, 0)
    h2 = acc_ref[...]
    x3 = (_rms_unit(h2) * gple_ref[...]).astype(BF16)
    gate = jax.nn.sigmoid(jnp.dot(x3, wpg_ref[...], preferred_element_type=F32))
    proj = jnp.dot(p_ref[...].astype(BF16), wpp_ref[...], preferred_element_type=F32)
    out_ref[...] = h2 + proj * gate


def _channel(x, o, p, w):
    n, d = x.shape
    tm = min(n, 512)
    row = lambda cols: pl.BlockSpec((tm, cols), lambda i: (i, 0))
    weights = [w["wo"], w["gffn"], w["wr"], w["wgu"], w["wd"], w["gple"], w["wpg"], w["wpp"]]
    return pl.pallas_call(
        _channel_kernel,
        grid=(n // tm,),
        in_specs=[row(d), row(d), row(p.shape[1])] + [_resident(a.shape) for a in weights],
        out_specs=row(d),
        out_shape=jax.ShapeDtypeStruct((n, d), F32),
        scratch_shapes=[pltpu.VMEM((tm, d), F32)],
        compiler_params=_cparams(("parallel",)),
        name="channel",
    )(x, o, p, *weights)


def _proj_b_kernel(h_ref, ga_ref, wq_ref, gq_ref, gkv_ref, wkv_ref, gk_ref, gsum_ref, gexp_ref,
                   q_ref, k_ref, v_ref, *, n_heads, n_kv):
    y = _rms_unit(h_ref[0])
    q = jnp.dot((y * ga_ref[...]).astype(BF16), wq_ref[...], preferred_element_type=F32)
    ms = jnp.dot(_split_hl(q * q), gsum_ref[...], preferred_element_type=F32) * (1.0 / HEAD_DIM)
    inv = lax.rsqrt(ms + EPS)
    inv_full = jnp.dot(_split_hl(inv), gexp_ref[...], preferred_element_type=F32)
    qn = (q * inv_full * gq_ref[...]) * (HEAD_DIM ** -0.5)
    for h in range(n_heads):
        q_ref[0, h] = qn[:, h * HEAD_DIM:(h + 1) * HEAD_DIM].astype(BF16)
    kv = jnp.dot((y * gkv_ref[...]).astype(BF16), wkv_ref[...], preferred_element_type=F32)
    for h in range(n_kv):
        kh = kv[:, h * HEAD_DIM:(h + 1) * HEAD_DIM]
        k_ref[0, h] = _rms_unit(kh) * gk_ref[...]
        v_ref[0, h] = kv[:, (n_kv + h) * HEAD_DIM:(n_kv + h + 1) * HEAD_DIM]


def _proj_b(h, ga, wq, gq, gkv, wkv, gk, gsum, gexp):
    bx, t, d = h.shape
    n_heads = wq.shape[1] // HEAD_DIM
    n_kv = wkv.shape[1] // (2 * HEAD_DIM)
    tm = min(t, 256)
    hm = lambda b, i: (b, 0, i, 0)
    ins = [ga, wq, gq, gkv, wkv, gk, gsum, gexp]
    return pl.pallas_call(
        functools.partial(_proj_b_kernel, n_heads=n_heads, n_kv=n_kv),
        grid=(bx, t // tm),
        in_specs=[pl.BlockSpec((1, tm, d), lambda b, i: (b, i, 0))] + [_resident(a.shape) for a in ins],
        out_specs=[pl.BlockSpec((1, n_heads, tm, HEAD_DIM), hm),
                   pl.BlockSpec((1, n_kv, tm, HEAD_DIM), hm),
                   pl.BlockSpec((1, n_kv, tm, HEAD_DIM), hm)],
        out_shape=[jax.ShapeDtypeStruct((bx, n_heads, t, HEAD_DIM), BF16),
                   jax.ShapeDtypeStruct((bx, n_kv, t, HEAD_DIM), F32),
                   jax.ShapeDtypeStruct((bx, n_kv, t, HEAD_DIM), F32)],
        compiler_params=_cparams(("parallel", "parallel")),
        name="proj_b",
    )(h, *ins)


def _t5_bucket(rel):
    nb = NUM_BUCKETS // 2
    max_exact = nb // 2
    n = jnp.abs(rel)
    large = max_exact + (jnp.log(jnp.maximum(n, 1).astype(jnp.float32) / max_exact)
                         / math.log(MAX_DISTANCE / max_exact) * (nb - max_exact)).astype(jnp.int32)
    large = jnp.minimum(large, nb - 1)
    return jnp.where(rel > 0, nb, 0) + jnp.where(n < max_exact, n, large)


def _bias_kernel(bucket_ref, rbt_ref, out_ref):
    bucket = bucket_ref[...]
    acc = jnp.zeros(out_ref.shape, F32)
    for b in range(NUM_BUCKETS):
        acc = acc + jnp.where(bucket == b, rbt_ref[:, b:b + 1], 0.0)
    out_ref[...] = acc


def _bias_table(rel_bias, nq, nk, key_offset):
    rel = (jnp.arange(nk, dtype=jnp.int32)[None, :] - key_offset) - jnp.arange(nq, dtype=jnp.int32)[:, None]
    bucket = _t5_bucket(rel).reshape(1, nq * nk)
    n_heads = rel_bias.shape[1]
    out = pl.pallas_call(
        _bias_kernel,
        out_shape=jax.ShapeDtypeStruct((n_heads, nq * nk), F32),
        name="bias_table",
    )(bucket, rel_bias.T)
    return out.reshape(n_heads, nq, nk)


def _swa_kernel(sink_ref, q_ref, k_ref, v_ref, bias_ref, o_ref, *, cq, wl, pad, n_chunks, group):
    n_kv = k_ref.shape[1]

    def chunk(c, _):
        r0 = pl.multiple_of(c * cq, cq)
        col = lax.broadcasted_iota(jnp.int32, (cq, wl), 1) + r0
        valid = col >= pad
        logits = []
        for kv in range(n_kv):
            kw = k_ref[0, kv, pl.ds(r0, wl), :].astype(BF16)
            qg = q_ref[0, kv * group:(kv + 1) * group, pl.ds(r0, cq), :].reshape(group * cq, HEAD_DIM)
            logits.append(lax.dot_general(qg, kw, _NT, preferred_element_type=F32))
        es, dens = [], []
        for h in range(n_kv * group):
            l = logits[h // group][(h % group) * cq:(h % group + 1) * cq] + bias_ref[h]
            if pad:
                l = jnp.where(valid, l, -jnp.inf)
            sink = sink_ref[h]
            m = jnp.maximum(jnp.max(l, axis=-1, keepdims=True), sink)
            e = jnp.exp(l - m)
            dens.append(jnp.sum(e, axis=-1, keepdims=True) + jnp.exp(sink - m))
            es.append(e.astype(BF16))
        pvs = []
        for kv in range(n_kv):
            vw = v_ref[0, kv, pl.ds(r0, wl), :].astype(BF16)
            pvs.append(jnp.dot(jnp.concatenate(es[kv * group:(kv + 1) * group], axis=0), vw,
                               preferred_element_type=F32))
        outs = [pvs[h // group][(h % group) * cq:(h % group + 1) * cq] / dens[h] for h in range(n_kv * group)]
        o_ref[0, pl.ds(r0, cq), :] = jnp.concatenate(outs, axis=-1).astype(BF16)
        return 0

    lax.fori_loop(0, n_chunks, chunk, 0)


def _swa(q, k_win, v_win, bias, sinks, *, cq, wl, pad):
    b, n_heads, tq, _ = q.shape
    n_kv, tk = k_win.shape[1], k_win.shape[2]
    n_chunks = tq // cq
    assert (n_chunks - 1) * cq + wl == tk
    kv_blk = pl.BlockSpec((1, n_kv, tk, HEAD_DIM), lambda bi: (bi, 0, 0, 0))
    return pl.pallas_call(
        functools.partial(_swa_kernel, cq=cq, wl=wl, pad=pad, n_chunks=n_chunks, group=n_heads // n_kv),
        grid=(b,),
        in_specs=[pl.BlockSpec(memory_space=pltpu.SMEM),
                  pl.BlockSpec((1, n_heads, tq, HEAD_DIM), lambda bi: (bi, 0, 0, 0)),
                  kv_blk, kv_blk, _resident(bias.shape)],
        out_specs=pl.BlockSpec((1, tq, n_heads * HEAD_DIM), lambda bi: (bi, 0, 0)),
        out_shape=jax.ShapeDtypeStruct((b, tq, n_heads * HEAD_DIM), BF16),
        compiler_params=_cparams(("parallel",)),
        name="swa",
    )(sinks, q, k_win, v_win, bias)


def _row(v):
    return v.reshape(1, -1).astype(F32)


def _router_hi_hi_lo(wr):
    hi = wr.astype(BF16)
    lo = (wr - hi.astype(F32)).astype(BF16)
    return jnp.concatenate([hi, hi, lo], axis=0)


def _prep_weights(prm):
    d = prm["a_w_o"].shape[1]
    depth = prm["norm_ffn"].shape[0]
    n_a = prm["a_w_qkv"].shape[0]
    scale = HEAD_DIM ** -0.5
    w = {"channel": [], "n_a": n_a, "depth": depth}
    for i in range(depth):
        wo = prm["a_w_o"][i] if i < n_a else prm["b_w_o"][i - n_a]
        pad = ROUTER_LANES - N_GROUPS - N_EXPERTS
        wr = jnp.concatenate([prm["moe_w_group"][i], prm["moe_w_router"][i].reshape(d, N_EXPERTS),
                              jnp.zeros((d, pad), F32)], axis=1)
        w["channel"].append({
            "wo": wo.astype(BF16), "gffn": _row(prm["norm_ffn"][i]), "wr": _router_hi_hi_lo(wr),
            "wgu": jnp.concatenate([prm["moe_w_gate"][i], prm["moe_w_up"][i]], axis=-1).astype(BF16),
            "wd": prm["moe_w_down"][i].astype(BF16), "gple": _row(prm["norm_ple"][i]),
            "wpg": prm["ple_w_gate"][i].astype(BF16), "wpp": prm["ple_w_proj"][i].astype(BF16)})
    w["qkv"] = []
    for i in range(n_a):
        wq = prm["a_w_qkv"][i]
        hd = wq.shape[1] // 3
        w["qkv"].append(((wq[:, :hd] * scale).astype(BF16),
                         wq[:, hd:].T.astype(BF16)))
    n_heads = prm["b_w_q"].shape[2] // HEAD_DIM
    head_of = jnp.arange(n_heads * HEAD_DIM, dtype=jnp.int32) // HEAD_DIM
    lanes = jnp.arange(ROUTER_LANES, dtype=jnp.int32)
    member = (head_of[:, None] == lanes[None, :]).astype(BF16)
    w["gsum"] = jnp.concatenate([member, member], axis=0)
    w["gexp"] = jnp.concatenate([member.T, member.T], axis=0)
    w["wq_b"] = [prm["b_w_q"][j].astype(BF16) for j in range(depth - n_a)]
    w["gq_b"] = [_row(jnp.tile(prm["b_q_norm"][j], n_heads)) for j in range(depth - n_a)]
    w["wkv"] = prm["b_w_kv"].astype(BF16)
    return w


def _assert_sample_window_visible(past_len, s, tk):
    q_chunk = [(past_len + i) // CHUNK for i in range(s)]
    k_pos = [past_len + s - tk + j for j in range(tk)]
    ok = all(kp >= 0 and qc - WIN_CHUNKS <= kp // CHUNK <= qc for qc in q_chunk for kp in k_pos)
    if not ok:
        raise NotImplementedError("sample window with masked keys")


def _run_trunk(x, p, prm, w, sb_cache_k=None, sb_cache_v=None, swa_cache_k=None, swa_cache_v=None):
    bx, t, d = x.shape
    n_a, depth = w["n_a"], w["depth"]
    sample = sb_cache_k is not None
    h = x
    sb_k, sb_v = [], []
    k_win = v_win = None
    q_b = None
    for i in range(depth):
        if i < n_a:
            q, kt, vt = _proj_a(h, _row(prm["norm_attn"][i]), *w["qkv"][i])
            sb_k.append(kt)
            sb_v.append(vt)
            if sample:
                o = _sb_sample(q, kt, vt, jnp.swapaxes(sb_cache_k, -1, -2), jnp.swapaxes(sb_cache_v, -1, -2), i)
            else:
                o = _sb_prompt(q, kt, vt)
        else:
            j = i - n_a
            if j > 0:
                raise NotImplementedError("one B layer supported")
            if sample:
                tk = k_win.shape[2]
                _assert_sample_window_visible(sb_cache_k.shape[3], t, tk)
                o = _swa(q_b, k_win, v_win, _bias_table(prm["rel_bias"], t, tk, tk - t),
                         prm["b_sinks"][j], cq=t, wl=tk, pad=0)
            else:
                front = ((0, 0), (0, 0), (WINDOW, 0), (0, 0))
                wl = WINDOW + CHUNK
                o = _swa(q_b, jnp.pad(k_win, front), jnp.pad(v_win, front),
                         _bias_table(prm["rel_bias"], CHUNK, wl, WINDOW), prm["b_sinks"][j],
                         cq=CHUNK, wl=wl, pad=WINDOW)
        h = _channel(h.reshape(bx * t, d), o.reshape(bx * t, d), p[i].reshape(bx * t, -1),
                     w["channel"][i]).reshape(bx, t, d)
        if i == n_a - 1:
            q_b, k_s, v_s = _proj_b(h, _row(prm["norm_attn"][n_a]), w["wq_b"][0], w["gq_b"][0],
                                    _row(prm["kv_norm"]), w["wkv"], _row(prm["b_k_norm"]),
                                    w["gsum"], w["gexp"])
            if sample:
                k_win = jnp.concatenate([swa_cache_k, k_s], axis=2)
                v_win = jnp.concatenate([swa_cache_v, v_s], axis=2)
            else:
                k_win, v_win = k_s, v_s
    sb_k = jnp.swapaxes(jnp.stack(sb_k), -1, -2)
    sb_v = jnp.swapaxes(jnp.stack(sb_v), -1, -2)
    return h, sb_k, sb_v, k_win[:, :, -WINDOW:], v_win[:, :, -WINDOW:]


def kernel(x_prompt, x_sample, p_prompt, p_sample, cache_sb_k, cache_sb_v, cache_swa_k, cache_swa_v, norm_attn, norm_ffn, norm_ple, a_w_qkv, a_w_o, kv_norm, b_w_kv, b_k_norm, b_w_q, b_q_norm, b_sinks, b_w_o, rel_bias, moe_w_group, moe_w_router, moe_w_gate, moe_w_up, moe_w_down, ple_w_proj, ple_w_gate):
    prm = {
        "norm_attn": norm_attn, "norm_ffn": norm_ffn, "norm_ple": norm_ple,
        "a_w_qkv": a_w_qkv, "a_w_o": a_w_o, "kv_norm": kv_norm, "b_w_kv": b_w_kv, "b_k_norm": b_k_norm,
        "b_w_q": b_w_q, "b_q_norm": b_q_norm, "b_sinks": b_sinks, "b_w_o": b_w_o, "rel_bias": rel_bias,
        "moe_w_group": moe_w_group, "moe_w_router": moe_w_router, "moe_w_gate": moe_w_gate,
        "moe_w_up": moe_w_up, "moe_w_down": moe_w_down, "ple_w_proj": ple_w_proj, "ple_w_gate": ple_w_gate,
    }
    w = _prep_weights(prm)
    y_p, sb_k_p, sb_v_p, swa_k_p, swa_v_p = _run_trunk(x_prompt, p_prompt, prm, w)
    y_s, sb_k_s, sb_v_s, swa_k_s, swa_v_s = _run_trunk(x_sample, p_sample, prm, w, cache_sb_k, cache_sb_v,
                                                       cache_swa_k, cache_swa_v)
    return (y_p, y_s, sb_k_p, sb_v_p, swa_k_p, swa_v_p, sb_k_s, sb_v_s, swa_k_s, swa_v_s)
```

```python
import functools
import math

import jax
import jax.numpy as jnp
from jax import lax
from jax.experimental import pallas as pl
from jax.experimental.pallas import tpu as pltpu

F32 = jnp.float32
BF16 = jnp.bfloat16

HEAD_DIM = 64
CHUNK = 64
WINDOW = 128
WIN_CHUNKS = WINDOW // CHUNK
NUM_BUCKETS = 32
MAX_DISTANCE = 128
N_GROUPS = 4
EXPERTS_PER_GROUP = 4
N_EXPERTS = N_GROUPS * EXPERTS_PER_GROUP
EPS = 1e-6
LOG2E = 1.4426950408889634
SB_BLOCK = 128
ROUTER_LANES = 128
VMEM_LIMIT = 56 * 1024 * 1024

_NT = (((1,), (1,)), ((), ()))


def _cparams(sem):
    return pltpu.CompilerParams(dimension_semantics=sem, vmem_limit_bytes=VMEM_LIMIT)


def _rms_unit(x):
    return x * lax.rsqrt(jnp.mean(x * x, axis=-1, keepdims=True) + EPS)


def _split_hl(a):
    hi = a.astype(BF16)
    lo = (a - hi.astype(F32)).astype(BF16)
    return jnp.concatenate([hi, lo], axis=-1)


def _resident(shape):
    nd = len(shape)
    return pl.BlockSpec(shape, lambda *_: (0,) * nd, pipeline_mode=pl.Buffered(1))


def _proj_a_kernel(x_ref, g_ref, wq_ref, wkvt_ref, q_ref, kt_ref, vt_ref, *, n_heads):
    xn = (_rms_unit(x_ref[0]) * g_ref[...]).astype(BF16)
    heads_per_dot = 4
    width = heads_per_dot * HEAD_DIM
    for c in range(n_heads // heads_per_dot):
        r = jnp.dot(xn, wq_ref[:, c * width:(c + 1) * width], preferred_element_type=F32)
        for hh in range(heads_per_dot):
            q_ref[0, c * heads_per_dot + hh] = r[:, hh * HEAD_DIM:(hh + 1) * HEAD_DIM].astype(BF16)
    for c in range(2 * n_heads // heads_per_dot):
        r = lax.dot_general(wkvt_ref[c * width:(c + 1) * width, :], xn, _NT, preferred_element_type=F32)
        for hh in range(heads_per_dot):
            h = c * heads_per_dot + hh
            piece = r[hh * HEAD_DIM:(hh + 1) * HEAD_DIM, :]
            if h < n_heads:
                kt_ref[0, h] = piece
            else:
                vt_ref[0, h - n_heads] = piece


def _proj_a(x, g, wq, wkvt):
    bx, t, d = x.shape
    n_heads = wq.shape[1] // HEAD_DIM
    tm = min(t, 256)
    return pl.pallas_call(
        functools.partial(_proj_a_kernel, n_heads=n_heads),
        grid=(bx, t // tm),
        in_specs=[pl.BlockSpec((1, tm, d), lambda b, i: (b, i, 0)),
                  _resident((1, d)), _resident(wq.shape), _resident(wkvt.shape)],
        out_specs=[pl.BlockSpec((1, n_heads, tm, HEAD_DIM), lambda b, i: (b, 0, i, 0)),
                   pl.BlockSpec((1, n_heads, HEAD_DIM, tm), lambda b, i: (b, 0, 0, i)),
                   pl.BlockSpec((1, n_heads, HEAD_DIM, tm), lambda b, i: (b, 0, 0, i))],
        out_shape=[jax.ShapeDtypeStruct((bx, n_heads, t, HEAD_DIM), BF16),
                   jax.ShapeDtypeStruct((bx, n_heads, HEAD_DIM, t), F32),
                   jax.ShapeDtypeStruct((bx, n_heads, HEAD_DIM, t), F32)],
        compiler_params=_cparams(("parallel", "parallel")),
        name="proj_a",
    )(x, g, wq, wkvt)


SB_DEAD = -104.0
SB_WINDOW_BLOCKS = 3
SB_SAMPLE_WINDOW = 256


def _suffix_neg_ones(n):
    r = lax.broadcasted_iota(jnp.int32, (n, n), 0)
    c = lax.broadcasted_iota(jnp.int32, (n, n), 1)
    return jnp.where(r >= c, -1.0, 0.0).astype(BF16)


def _strict_mask(n):
    row = lax.broadcasted_iota(jnp.int32, (n, n), 0)
    col = lax.broadcasted_iota(jnp.int32, (n, n), 1)
    return col < row


def _softplus(z):
    return jnp.maximum(z, 0.0) + jnp.log(1.0 + jnp.exp2(jnp.abs(z) * -LOG2E))


def _sb_strips(qs, kts, vts, carries, accs, uu, masks):
    n = len(qs)
    blk = uu.shape[1]
    nb = kts[0].shape[1] // blk
    zs = [jnp.dot(qs[i], kts[i], preferred_element_type=F32) for i in range(n)]
    sps = [_softplus(z) for z in zs]
    ws = [[None] * nb for _ in range(n)]
    carries = list(carries)
    for b in reversed(range(nb)):
        sl = slice(b * blk, (b + 1) * blk)
        spbs = [sp[:, sl] if masks[b] is None else jnp.where(masks[b], sp[:, sl], 0.0) for sp in sps]
        sufs = [jnp.dot(spb.astype(BF16), uu, preferred_element_type=F32) for spb in spbs]
        for i in range(n):
            wb = jnp.exp(zs[i][:, sl] + sufs[i] + carries[i])
            if masks[b] is not None:
                wb = jnp.where(masks[b], wb, 0.0)
            ws[i][b] = wb.astype(BF16)
            carries[i] = carries[i] + sufs[i][:, 0:1]
    wcat = [w[0] if nb == 1 else jnp.concatenate(w, axis=-1) for w in ws]
    accs = [accs[i] + lax.dot_general(wcat[i], vts[i], _NT, preferred_element_type=F32) for i in range(n)]
    return carries, accs


SB_STAGES = 5


def _sb_prompt_fast_kernel(q_ref, kt_ref, vt_ref, o_ref, flag_ref, z_ref, sp_ref, suf_ref, w_ref, *, heads):
    t = q_ref.shape[2]
    blk = SB_BLOCK
    nq = t // blk
    win = SB_WINDOW_BLOCKS
    wl = win * blk
    ring = z_ref.shape[0]
    first = win - 1
    uu = _suffix_neg_ones(blk)
    strict = _strict_mask(blk)
    z_ref[...] = jnp.zeros(z_ref.shape, F32)
    sp_ref[...] = jnp.zeros(sp_ref.shape, BF16)
    suf_ref[...] = jnp.zeros(suf_ref.shape, F32)
    w_ref[...] = jnp.zeros(w_ref.shape, BF16)

    for i in range(first):
        _, accs = _sb_strips([q_ref[0, h, i * blk:(i + 1) * blk, :] for h in range(heads)],
                             [kt_ref[0, h, :, 0:(i + 1) * blk].astype(BF16) for h in range(heads)],
                             [vt_ref[0, h, :, 0:(i + 1) * blk].astype(BF16) for h in range(heads)],
                             [jnp.zeros((blk, 1), F32)] * heads, [jnp.zeros((blk, HEAD_DIM), F32)] * heads,
                             uu, [None] * i + [strict])
        o_ref[0, i * blk:(i + 1) * blk, :] = jnp.concatenate(accs, axis=-1).astype(BF16)

    def where_is(n):
        i = jnp.clip(first + n, first, nq - 1)
        q0 = pl.multiple_of(i * blk, blk)
        k0 = pl.multiple_of((i - first) * blk, blk)
        return q0, k0, lax.rem(n + ring * SB_STAGES, ring)

    def body(n, worst):
        q0, k0, _ = where_is(n - 4)
        accs = []
        for h in range(heads):
            vt = vt_ref[0, h, :, pl.ds(k0, wl)].astype(BF16)
            accs.append(lax.dot_general(w_ref[h], vt, _NT, preferred_element_type=F32))
        o_ref[0, pl.ds(q0, blk), :] = jnp.concatenate(accs, axis=-1).astype(BF16)

        _, _, slot = where_is(n - 3)
        counts = n - 3 >= 1
        for h in range(heads):
            carry = jnp.zeros((blk, 1), F32)
            for b in reversed(range(win)):
                sl = slice(b * blk, (b + 1) * blk)
                suffix = suf_ref[h, b]
                wb = jnp.exp(z_ref[slot, h, :, sl] + suffix + carry)
                if b == win - 1:
                    wb = jnp.where(strict, wb, 0.0)
                w_ref[h, :, sl] = wb.astype(BF16)
                carry = carry + suffix[:, 0:1]
            worst = jnp.maximum(worst, jnp.where(counts, carry, -jnp.inf))

        for h in range(heads):
            suf_ref[h] = jnp.dot(sp_ref[h].reshape(win * blk, blk), uu,
                                 preferred_element_type=F32).reshape(win, blk, blk)

        _, _, slot = where_is(n - 1)
        for h in range(heads):
            sp = _softplus(z_ref[slot, h])
            for b in range(win):
                spb = sp[:, b * blk:(b + 1) * blk]
                if b == win - 1:
                    spb = jnp.where(strict, spb, 0.0)
                sp_ref[h, b] = spb.astype(BF16)

        q0, k0, slot = where_is(n)
        for h in range(heads):
            kt = kt_ref[0, h, :, pl.ds(k0, wl)].astype(BF16)
            z_ref[slot, h] = jnp.dot(q_ref[0, h, pl.ds(q0, blk), :], kt, preferred_element_type=F32)
        return worst

    worst = lax.fori_loop(0, nq - first + SB_STAGES - 1, body, jnp.full((blk, 1), -jnp.inf, F32))
    flag_ref[...] = jnp.broadcast_to(jnp.max(worst), flag_ref.shape)


def _sb_prompt_full_kernel(q_ref, kt_ref, vt_ref, o_ref, *, heads):
    t = q_ref.shape[2]
    blk = SB_BLOCK
    uu = _suffix_neg_ones(blk)
    strict = _strict_mask(blk)

    def q_block(i, _):
        q0 = pl.multiple_of(i * blk, blk)
        qs = [q_ref[0, h, pl.ds(q0, blk), :] for h in range(heads)]

        def strips(k0, carries, accs, masks):
            return _sb_strips(qs, [kt_ref[0, h, :, pl.ds(k0, blk)].astype(BF16) for h in range(heads)],
                              [vt_ref[0, h, :, pl.ds(k0, blk)].astype(BF16) for h in range(heads)],
                              carries, accs, uu, masks)

        state = strips(q0, [jnp.zeros((blk, 1), F32)] * heads, [jnp.zeros((blk, HEAD_DIM), F32)] * heads,
                       [strict])

        def k_block(jj, st):
            carries, accs = strips(pl.multiple_of((i - 1 - jj) * blk, blk), st[0], st[1], [None])
            return tuple(carries), tuple(accs)

        _, accs = lax.fori_loop(0, i, k_block, (tuple(state[0]), tuple(state[1])))
        o_ref[0, pl.ds(q0, blk), :] = jnp.concatenate(list(accs), axis=-1).astype(BF16)
        return 0

    lax.fori_loop(0, t // blk, q_block, 0)


def _sb_prompt(q, kt, vt):
    b, n_heads, t, _ = q.shape
    heads = 4
    wl = SB_WINDOW_BLOCKS * SB_BLOCK
    assert t >= wl
    q_blk = pl.BlockSpec((1, heads, t, HEAD_DIM), lambda bi, hg: (bi, hg, 0, 0))
    kv_blk = pl.BlockSpec((1, heads, HEAD_DIM, t), lambda bi, hg: (bi, hg, 0, 0))
    o_blk = pl.BlockSpec((1, t, heads * HEAD_DIM), lambda bi, hg: (bi, 0, hg))
    o_shape = jax.ShapeDtypeStruct((b, t, n_heads * HEAD_DIM), BF16)
    grid = (b, n_heads // heads)
    o_fast, flags = pl.pallas_call(
        functools.partial(_sb_prompt_fast_kernel, heads=heads),
        grid=grid,
        in_specs=[q_blk, kv_blk, kv_blk],
        out_specs=[o_blk, pl.BlockSpec((1, 1, 8, 128), lambda bi, hg: (bi, hg, 0, 0))],
        out_shape=[o_shape, jax.ShapeDtypeStruct((b, n_heads // heads, 8, 128), F32)],
        scratch_shapes=[pltpu.VMEM((SB_STAGES - 1, heads, SB_BLOCK, wl), F32),
                        pltpu.VMEM((heads, SB_WINDOW_BLOCKS, SB_BLOCK, SB_BLOCK), BF16),
                        pltpu.VMEM((heads, SB_WINDOW_BLOCKS, SB_BLOCK, SB_BLOCK), F32),
                        pltpu.VMEM((heads, SB_BLOCK, wl), BF16)],
        compiler_params=_cparams(("parallel", "parallel")),
        name="sb_prompt_fast",
    )(q, kt, vt)
    if t // SB_BLOCK <= SB_WINDOW_BLOCKS:
        return o_fast

    def full():
        return pl.pallas_call(
            functools.partial(_sb_prompt_full_kernel, heads=heads),
            grid=grid,
            in_specs=[q_blk, kv_blk, kv_blk],
            out_specs=o_blk,
            out_shape=o_shape,
            compiler_params=_cparams(("parallel", "parallel")),
            name="sb_prompt_full",
        )(q, kt, vt)

    return lax.cond(jnp.max(flags) > SB_DEAD, full, lambda: o_fast)


def _sb_sample_kernel(q_ref, ktn_ref, vtn_ref, ktc_ref, vtc_ref, o_ref, *maybe_flag, heads, strip):
    s = q_ref.shape[2]
    width = ktc_ref.shape[4]
    blk = SB_BLOCK
    uu = _suffix_neg_ones(blk)
    uu_new = _suffix_neg_ones(s)
    strict = _strict_mask(s)

    qs = [q_ref[0, h] for h in range(heads)]
    state = _sb_strips(qs, [ktn_ref[0, h].astype(BF16) for h in range(heads)],
                       [vtn_ref[0, h].astype(BF16) for h in range(heads)],
                       [jnp.zeros((s, 1), F32)] * heads, [jnp.zeros((s, HEAD_DIM), F32)] * heads,
                       uu_new, [strict])

    def k_strip(jj, st):
        k0 = pl.multiple_of(width - (jj + 1) * strip * blk, blk)
        carries, accs = _sb_strips(
            qs, [ktc_ref[0, 0, h, :, pl.ds(k0, strip * blk)].astype(BF16) for h in range(heads)],
            [vtc_ref[0, 0, h, :, pl.ds(k0, strip * blk)].astype(BF16) for h in range(heads)],
            st[0], st[1], uu, [None] * strip)
        return tuple(carries), tuple(accs)

    carries, accs = lax.fori_loop(0, width // (strip * blk), k_strip, (tuple(state[0]), tuple(state[1])))
    o_ref[0] = jnp.concatenate(list(accs), axis=-1).astype(BF16)
    if maybe_flag:
        worst = carries[0]
        for h in range(1, heads):
            worst = jnp.maximum(worst, carries[h])
        maybe_flag[0][...] = jnp.broadcast_to(jnp.max(worst), maybe_flag[0].shape)


def _sb_sample(q, kt_new, vt_new, cache_kt, cache_vt, layer):
    b, n_heads, s, _ = q.shape
    past = cache_kt.shape[4]
    width = min(past, SB_SAMPLE_WINDOW)
    assert past % width == 0 and width % SB_BLOCK == 0
    last = past // width - 1
    o_shape = jax.ShapeDtypeStruct((b, s, n_heads * HEAD_DIM), BF16)

    def specs(heads, cache_width, cache_block):
        q_blk = pl.BlockSpec((1, heads, s, HEAD_DIM), lambda bi, hg: (bi, hg, 0, 0))
        new_blk = pl.BlockSpec((1, heads, HEAD_DIM, s), lambda bi, hg: (bi, hg, 0, 0))
        cache_blk = pl.BlockSpec((1, 1, heads, HEAD_DIM, cache_width),
                                 lambda bi, hg: (layer, bi, hg, 0, cache_block))
        o_blk = pl.BlockSpec((1, s, heads * HEAD_DIM), lambda bi, hg: (bi, 0, hg))
        return (b, n_heads // heads), [q_blk, new_blk, new_blk, cache_blk, cache_blk], o_blk

    heads = n_heads
    grid, in_specs, o_blk = specs(heads, width, last)
    o_fast, flags = pl.pallas_call(
        functools.partial(_sb_sample_kernel, heads=heads, strip=width // SB_BLOCK),
        grid=grid,
        in_specs=in_specs,
        out_specs=[o_blk, pl.BlockSpec((1, 1, 8, 128), lambda bi, hg: (bi, hg, 0, 0))],
        out_shape=[o_shape, jax.ShapeDtypeStruct((b, n_heads // heads, 8, 128), F32)],
        compiler_params=_cparams(("parallel", "parallel")),
        name="sb_sample_fast",
    )(q, kt_new, vt_new, cache_kt, cache_vt)
    if width == past:
        return o_fast

    def full():
        heads = 4
        grid, in_specs, o_blk = specs(heads, past, 0)
        return pl.pallas_call(
            functools.partial(_sb_sample_kernel, heads=heads, strip=1),
            grid=grid,
            in_specs=in_specs,
            out_specs=o_blk,
            out_shape=o_shape,
            compiler_params=_cparams(("parallel", "parallel")),
            name="sb_sample_full",
        )(q, kt_new, vt_new, cache_kt, cache_vt)

    return lax.cond(jnp.max(flags) > SB_DEAD, full, lambda: o_fast)


def _route(logits):
    tm = logits.shape[0]
    lane = lax.broadcasted_iota(jnp.int32, (tm, ROUTER_LANES), 1)
    lane_f = lane.astype(F32)
    neg = -jnp.inf
    first = lambda hit: jnp.min(jnp.where(hit, lane_f, float(ROUTER_LANES)), axis=-1, keepdims=True)

    gl = jnp.where(lane < N_GROUPS, logits, neg)
    g_max = jnp.max(gl, axis=-1, keepdims=True)
    g_top = 1.0 / jnp.sum(jnp.exp(gl - g_max), axis=-1, keepdims=True)
    g_idx = first(gl == g_max)

    lo = N_GROUPS + g_idx * EXPERTS_PER_GROUP
    in_group = (lane_f >= lo) & (lane_f < lo + EXPERTS_PER_GROUP)
    sel = jnp.where(in_group, logits, neg)
    t1 = jnp.max(sel, axis=-1, keepdims=True)
    i1 = first(sel == t1)
    sel2 = jnp.where(lane_f == i1, neg, sel)
    t2 = jnp.max(sel2, axis=-1, keepdims=True)
    i2 = first(sel2 == t2)
    e2 = jnp.exp(t2 - t1)
    den = 1.0 + e2
    w1 = (1.0 / den) * g_top
    w2 = (e2 / den) * g_top
    return jnp.where(lane_f == i1, w1, 0.0) + jnp.where(lane_f == i2, w2, 0.0)


CHANNEL_ROW_SPLITS = 2
EXPERTS_PER_TRIP = 2


def _channel_kernel(x_ref, o_ref, p_ref, wo_ref, gffn_ref, wr_ref, wgu_ref, wd_ref, gple_ref,
                    wpg_ref, wpp_ref, out_ref, acc_ref):
    tm = x_ref.shape[0]
    d_expert = wd_ref.shape[1] // EXPERTS_PER_TRIP
    half = tm // CHANNEL_ROW_SPLITS
    rows = [slice(i * half, (i + 1) * half) for i in range(CHANNEL_ROW_SPLITS)]

    h1s = [x_ref[r, :] + jnp.dot(o_ref[r, :], wo_ref[...], preferred_element_type=F32) for r in rows]
    xns = [_rms_unit(h1) * gffn_ref[...] for h1 in h1s]
    xnbs = [xn.astype(BF16) for xn in xns]
    xlos = [(xn - xnb.astype(F32)).astype(BF16) for xn, xnb in zip(xns, xnbs)]
    logits = [jnp.dot(jnp.concatenate([xnb, xlo, xnb], axis=-1), wr_ref[...], preferred_element_type=F32)
              for xnb, xlo in zip(xnbs, xlos)]
    gates = jnp.concatenate([_route(l) for l in logits], axis=0)
    xnb = jnp.concatenate(xnbs, axis=0)
    lane = lax.broadcasted_iota(jnp.int32, gates.shape, 1)
    for r, h1 in zip(rows, h1s):
        acc_ref[r, :] = h1

    def experts(j, _):
        gu = jnp.dot(xnb, wgu_ref[j], preferred_element_type=F32)
        hids = []
        for k in range(EXPERTS_PER_TRIP):
            e = j * EXPERTS_PER_TRIP + k
            gate_e = jnp.sum(jnp.where(lane == N_GROUPS + e, gates, 0.0), axis=-1, keepdims=True)
            g = gu[:, 2 * k * d_expert:(2 * k + 1) * d_expert]
            u = gu[:, (2 * k + 1) * d_expert:(2 * k + 2) * d_expert]
            hids.append(((g * jax.nn.sigmoid(g)) * u * gate_e).astype(BF16))
        acc_ref[...] += jnp.dot(jnp.concatenate(hids, axis=-1), wd_ref[j], preferred_element_type=F32)
        return 0

    lax.fori_loop(0, N_EXPERTS // EXPERTS_PER_TRIP, experts, 0)
    h2s = [acc_ref[r, :] for r in rows]
    x3s = [(_rms_unit(h2) * gple_ref[...]).astype(BF16) for h2 in h2s]
    ple_gates = [jax.nn.sigmoid(jnp.dot(x3, wpg_ref[...], preferred_element_type=F32)) for x3 in x3s]
    projs = [jnp.dot(p_ref[r, :].astype(BF16), wpp_ref[...], preferred_element_type=F32) for r in rows]
    for r, h2, proj, gate in zip(rows, h2s, projs, ple_gates):
        out_ref[r, :] = h2 + proj * gate


def _channel(x, o, p, w):
    n, d = x.shape
    tm = min(n, 512)
    row = lambda cols: pl.BlockSpec((tm, cols), lambda i: (i, 0))
    weights = [w["wo"], w["gffn"], w["wr"], w["wgu"], w["wd"], w["gple"], w["wpg"], w["wpp"]]
    return pl.pallas_call(
        _channel_kernel,
        grid=(n // tm,),
        in_specs=[row(d), row(d), row(p.shape[1])] + [_resident(a.shape) for a in weights],
        out_specs=row(d),
        out_shape=jax.ShapeDtypeStruct((n, d), F32),
        scratch_shapes=[pltpu.VMEM((tm, d), F32)],
        compiler_params=_cparams(("parallel",)),
        name="channel",
    )(x, o, p, *weights)


def _proj_b_kernel(h_ref, ga_ref, wq_ref, gq_ref, gkv_ref, wkv_ref, gk_ref, gsum_ref, gexp_ref,
                   q_ref, k_ref, v_ref, *, n_heads, n_kv):
    y = _rms_unit(h_ref[0])
    q = jnp.dot((y * ga_ref[...]).astype(BF16), wq_ref[...], preferred_element_type=F32)
    ms = jnp.dot((q * q).astype(BF16), gsum_ref[...], preferred_element_type=F32) * (1.0 / HEAD_DIM)
    inv = lax.rsqrt(ms + EPS)
    inv_full = jnp.dot(_split_hl(inv), gexp_ref[...], preferred_element_type=F32)
    qn = (q * inv_full * gq_ref[...]) * (HEAD_DIM ** -0.5)
    for h in range(n_heads):
        q_ref[0, h] = qn[:, h * HEAD_DIM:(h + 1) * HEAD_DIM].astype(BF16)
    kv = jnp.dot((y * gkv_ref[...]).astype(BF16), wkv_ref[...], preferred_element_type=F32)
    for h in range(n_kv):
        kh = kv[:, h * HEAD_DIM:(h + 1) * HEAD_DIM]
        k_ref[0, h] = _rms_unit(kh) * gk_ref[...]
        v_ref[0, h] = kv[:, (n_kv + h) * HEAD_DIM:(n_kv + h + 1) * HEAD_DIM]


def _proj_b(h, ga, wq, gq, gkv, wkv, gk, gsum, gexp):
    bx, t, d = h.shape
    n_heads = wq.shape[1] // HEAD_DIM
    n_kv = wkv.shape[1] // (2 * HEAD_DIM)
    tm = min(t, 256)
    hm = lambda b, i: (b, 0, i, 0)
    ins = [ga, wq, gq, gkv, wkv, gk, gsum, gexp]
    return pl.pallas_call(
        functools.partial(_proj_b_kernel, n_heads=n_heads, n_kv=n_kv),
        grid=(bx, t // tm),
        in_specs=[pl.BlockSpec((1, tm, d), lambda b, i: (b, i, 0))] + [_resident(a.shape) for a in ins],
        out_specs=[pl.BlockSpec((1, n_heads, tm, HEAD_DIM), hm),
                   pl.BlockSpec((1, n_kv, tm, HEAD_DIM), hm),
                   pl.BlockSpec((1, n_kv, tm, HEAD_DIM), hm)],
        out_shape=[jax.ShapeDtypeStruct((bx, n_heads, t, HEAD_DIM), BF16),
                   jax.ShapeDtypeStruct((bx, n_kv, t, HEAD_DIM), F32),
                   jax.ShapeDtypeStruct((bx, n_kv, t, HEAD_DIM), F32)],
        compiler_params=_cparams(("parallel", "parallel")),
        name="proj_b",
    )(h, *ins)


def _t5_bucket(rel):
    nb = NUM_BUCKETS // 2
    max_exact = nb // 2
    n = jnp.abs(rel)
    large = max_exact + (jnp.log(jnp.maximum(n, 1).astype(jnp.float32) / max_exact)
                         / math.log(MAX_DISTANCE / max_exact) * (nb - max_exact)).astype(jnp.int32)
    large = jnp.minimum(large, nb - 1)
    return jnp.where(rel > 0, nb, 0) + jnp.where(n < max_exact, n, large)


def _bias_kernel(bucket_ref, rbt_ref, out_ref):
    bucket = bucket_ref[...]
    acc = jnp.zeros(out_ref.shape, F32)
    for b in range(NUM_BUCKETS):
        acc = acc + jnp.where(bucket == b, rbt_ref[:, b:b + 1], 0.0)
    out_ref[...] = acc


def _bias_table(rel_bias, nq, nk, key_offset):
    rel = (jnp.arange(nk, dtype=jnp.int32)[None, :] - key_offset) - jnp.arange(nq, dtype=jnp.int32)[:, None]
    bucket = _t5_bucket(rel).reshape(1, nq * nk)
    n_heads = rel_bias.shape[1]
    out = pl.pallas_call(
        _bias_kernel,
        out_shape=jax.ShapeDtypeStruct((n_heads, nq * nk), F32),
        name="bias_table",
    )(bucket, rel_bias.T)
    return out.reshape(n_heads, nq, nk)


def _swa_kernel(sink_ref, q_ref, k_ref, v_ref, bias_ref, o_ref, *, cq, wl, pad, n_chunks, group):
    n_kv = k_ref.shape[1]

    def chunk(c, _):
        r0 = pl.multiple_of(c * cq, cq)
        col = lax.broadcasted_iota(jnp.int32, (cq, wl), 1) + r0
        valid = col >= pad
        logits = []
        for kv in range(n_kv):
            kw = k_ref[0, kv, pl.ds(r0, wl), :].astype(BF16)
            qg = q_ref[0, kv * group:(kv + 1) * group, pl.ds(r0, cq), :].reshape(group * cq, HEAD_DIM)
            logits.append(lax.dot_general(qg, kw, _NT, preferred_element_type=F32))
        es, dens = [], []
        for h in range(n_kv * group):
            l = logits[h // group][(h % group) * cq:(h % group + 1) * cq] + bias_ref[h]
            if pad:
                l = jnp.where(valid, l, -jnp.inf)
            sink = sink_ref[h]
            m = jnp.maximum(jnp.max(l, axis=-1, keepdims=True), sink)
            e = jnp.exp(l - m)
            dens.append(jnp.sum(e, axis=-1, keepdims=True) + jnp.exp(sink - m))
            es.append(e.astype(BF16))
        pvs = []
        for kv in range(n_kv):
            vw = v_ref[0, kv, pl.ds(r0, wl), :].astype(BF16)
            pvs.append(jnp.dot(jnp.concatenate(es[kv * group:(kv + 1) * group], axis=0), vw,
                               preferred_element_type=F32))
        outs = [pvs[h // group][(h % group) * cq:(h % group + 1) * cq] / dens[h] for h in range(n_kv * group)]
        o_ref[0, pl.ds(r0, cq), :] = jnp.concatenate(outs, axis=-1).astype(BF16)
        return 0

    lax.fori_loop(0, n_chunks, chunk, 0)


def _swa(q, k_win, v_win, bias, sinks, *, cq, wl, pad):
    b, n_heads, tq, _ = q.shape
    n_kv, tk = k_win.shape[1], k_win.shape[2]
    n_chunks = tq // cq
    assert (n_chunks - 1) * cq + wl == tk
    kv_blk = pl.BlockSpec((1, n_kv, tk, HEAD_DIM), lambda bi: (bi, 0, 0, 0))
    return pl.pallas_call(
        functools.partial(_swa_kernel, cq=cq, wl=wl, pad=pad, n_chunks=n_chunks, group=n_heads // n_kv),
        grid=(b,),
        in_specs=[pl.BlockSpec(memory_space=pltpu.SMEM),
                  pl.BlockSpec((1, n_heads, tq, HEAD_DIM), lambda bi: (bi, 0, 0, 0)),
                  kv_blk, kv_blk, _resident(bias.shape)],
        out_specs=pl.BlockSpec((1, tq, n_heads * HEAD_DIM), lambda bi: (bi, 0, 0)),
        out_shape=jax.ShapeDtypeStruct((b, tq, n_heads * HEAD_DIM), BF16),
        compiler_params=_cparams(("parallel",)),
        name="swa",
    )(sinks, q, k_win, v_win, bias)


def _row(v):
    return v.reshape(1, -1).astype(F32)


def _router_hi_hi_lo(wr):
    hi = wr.astype(BF16)
    lo = (wr - hi.astype(F32)).astype(BF16)
    return jnp.concatenate([hi, hi, lo], axis=0)


def _group_experts_wide(w):
    e, d, f = w.shape
    w = w.reshape(e // EXPERTS_PER_TRIP, EXPERTS_PER_TRIP, d, f).transpose(0, 2, 1, 3)
    return w.reshape(e // EXPERTS_PER_TRIP, d, EXPERTS_PER_TRIP * f).astype(BF16)


def _group_experts_deep(w):
    e, f, d = w.shape
    return w.reshape(e // EXPERTS_PER_TRIP, EXPERTS_PER_TRIP * f, d).astype(BF16)


def _prep_weights(prm):
    d = prm["a_w_o"].shape[1]
    depth = prm["norm_ffn"].shape[0]
    n_a = prm["a_w_qkv"].shape[0]
    scale = HEAD_DIM ** -0.5
    w = {"channel": [], "n_a": n_a, "depth": depth}
    for i in range(depth):
        wo = prm["a_w_o"][i] if i < n_a else prm["b_w_o"][i - n_a]
        pad = ROUTER_LANES - N_GROUPS - N_EXPERTS
        wr = jnp.concatenate([prm["moe_w_group"][i], prm["moe_w_router"][i].reshape(d, N_EXPERTS),
                              jnp.zeros((d, pad), F32)], axis=1)
        w["channel"].append({
            "wo": wo.astype(BF16), "gffn": _row(prm["norm_ffn"][i]), "wr": _router_hi_hi_lo(wr),
            "wgu": _group_experts_wide(jnp.concatenate([prm["moe_w_gate"][i], prm["moe_w_up"][i]], axis=-1)),
            "wd": _group_experts_deep(prm["moe_w_down"][i]), "gple": _row(prm["norm_ple"][i]),
            "wpg": prm["ple_w_gate"][i].astype(BF16), "wpp": prm["ple_w_proj"][i].astype(BF16)})
    w["qkv"] = []
    for i in range(n_a):
        wq = prm["a_w_qkv"][i]
        hd = wq.shape[1] // 3
        w["qkv"].append(((wq[:, :hd] * scale).astype(BF16),
                         wq[:, hd:].T.astype(BF16)))
    n_heads = prm["b_w_q"].shape[2] // HEAD_DIM
    head_of = jnp.arange(n_heads * HEAD_DIM, dtype=jnp.int32) // HEAD_DIM
    lanes = jnp.arange(ROUTER_LANES, dtype=jnp.int32)
    member = (head_of[:, None] == lanes[None, :]).astype(BF16)
    w["gsum"] = member
    w["gexp"] = jnp.concatenate([member.T, member.T], axis=0)
    w["wq_b"] = [prm["b_w_q"][j].astype(BF16) for j in range(depth - n_a)]
    w["gq_b"] = [_row(jnp.tile(prm["b_q_norm"][j], n_heads)) for j in range(depth - n_a)]
    w["wkv"] = prm["b_w_kv"].astype(BF16)
    return w


def _assert_sample_window_visible(past_len, s, tk):
    q_chunk = [(past_len + i) // CHUNK for i in range(s)]
    k_pos = [past_len + s - tk + j for j in range(tk)]
    ok = all(kp >= 0 and qc - WIN_CHUNKS <= kp // CHUNK <= qc for qc in q_chunk for kp in k_pos)
    if not ok:
        raise NotImplementedError("sample window with masked keys")


def _run_trunk(x, p, prm, w, sb_cache_k=None, sb_cache_v=None, swa_cache_k=None, swa_cache_v=None):
    bx, t, d = x.shape
    n_a, depth = w["n_a"], w["depth"]
    sample = sb_cache_k is not None
    h = x
    sb_k, sb_v = [], []
    k_win = v_win = None
    q_b = None
    for i in range(depth):
        if i < n_a:
            q, kt, vt = _proj_a(h, _row(prm["norm_attn"][i]), *w["qkv"][i])
            sb_k.append(kt)
            sb_v.append(vt)
            if sample:
                o = _sb_sample(q, kt, vt, jnp.swapaxes(sb_cache_k, -1, -2), jnp.swapaxes(sb_cache_v, -1, -2), i)
            else:
                o = _sb_prompt(q, kt, vt)
        else:
            j = i - n_a
            if j > 0:
                raise NotImplementedError("one B layer supported")
            if sample:
                tk = k_win.shape[2]
                _assert_sample_window_visible(sb_cache_k.shape[3], t, tk)
                o = _swa(q_b, k_win, v_win, _bias_table(prm["rel_bias"], t, tk, tk - t),
                         prm["b_sinks"][j], cq=t, wl=tk, pad=0)
            else:
                front = ((0, 0), (0, 0), (WINDOW, 0), (0, 0))
                wl = WINDOW + CHUNK
                o = _swa(q_b, jnp.pad(k_win, front), jnp.pad(v_win, front),
                         _bias_table(prm["rel_bias"], CHUNK, wl, WINDOW), prm["b_sinks"][j],
                         cq=CHUNK, wl=wl, pad=WINDOW)
        h = _channel(h.reshape(bx * t, d), o.reshape(bx * t, d), p[i].reshape(bx * t, -1),
                     w["channel"][i]).reshape(bx, t, d)
        if i == n_a - 1:
            q_b, k_s, v_s = _proj_b(h, _row(prm["norm_attn"][n_a]), w["wq_b"][0], w["gq_b"][0],
                                    _row(prm["kv_norm"]), w["wkv"], _row(prm["b_k_norm"]),
                                    w["gsum"], w["gexp"])
            if sample:
                k_win = jnp.concatenate([swa_cache_k, k_s], axis=2)
                v_win = jnp.concatenate([swa_cache_v, v_s], axis=2)
            else:
                k_win, v_win = k_s, v_s
    sb_k = jnp.swapaxes(jnp.stack(sb_k), -1, -2)
    sb_v = jnp.swapaxes(jnp.stack(sb_v), -1, -2)
    return h, sb_k, sb_v, k_win[:, :, -WINDOW:], v_win[:, :, -WINDOW:]


def kernel(x_prompt, x_sample, p_prompt, p_sample, cache_sb_k, cache_sb_v, cache_swa_k, cache_swa_v, norm_attn, norm_ffn, norm_ple, a_w_qkv, a_w_o, kv_norm, b_w_kv, b_k_norm, b_w_q, b_q_norm, b_sinks, b_w_o, rel_bias, moe_w_group, moe_w_router, moe_w_gate, moe_w_up, moe_w_down, ple_w_proj, ple_w_gate):
    prm = {
        "norm_attn": norm_attn, "norm_ffn": norm_ffn, "norm_ple": norm_ple,
        "a_w_qkv": a_w_qkv, "a_w_o": a_w_o, "kv_norm": kv_norm, "b_w_kv": b_w_kv, "b_k_norm": b_k_norm,
        "b_w_q": b_w_q, "b_q_norm": b_q_norm, "b_sinks": b_sinks, "b_w_o": b_w_o, "rel_bias": rel_bias,
        "moe_w_group": moe_w_group, "moe_w_router": moe_w_router, "moe_w_gate": moe_w_gate,
        "moe_w_up": moe_w_up, "moe_w_down": moe_w_down, "ple_w_proj": ple_w_proj, "ple_w_gate": ple_w_gate,
    }
    w = _prep_weights(prm)
    y_p, sb_k_p, sb_v_p, swa_k_p, swa_v_p = _run_trunk(x_prompt, p_prompt, prm, w)
    y_s, sb_k_s, sb_v_s, swa_k_s, swa_v_s = _run_trunk(x_sample, p_sample, prm, w, cache_sb_k, cache_sb_v,
                                                       cache_swa_k, cache_swa_v)
    return (y_p, y_s, sb_k_p, sb_v_p, swa_k_p, swa_v_p, sb_k_s, sb_v_s, swa_k_s, swa_v_s)
```

```python
import functools
import math

import jax
import jax.numpy as jnp
from jax import lax
from jax.experimental import pallas as pl
from jax.experimental.pallas import tpu as pltpu

F32 = jnp.float32
BF16 = jnp.bfloat16

HEAD_DIM = 64
CHUNK = 64
WINDOW = 128
WIN_CHUNKS = WINDOW // CHUNK
NUM_BUCKETS = 32
MAX_DISTANCE = 128
N_GROUPS = 4
EXPERTS_PER_GROUP = 4
N_EXPERTS = N_GROUPS * EXPERTS_PER_GROUP
EPS = 1e-6
LOG2E = 1.4426950408889634
SB_BLOCK = 128
ROUTER_LANES = 128
VMEM_LIMIT = 56 * 1024 * 1024

_NT = (((1,), (1,)), ((), ()))


def _cparams(sem):
    return pltpu.CompilerParams(dimension_semantics=sem, vmem_limit_bytes=VMEM_LIMIT)


def _rms_unit(x):
    return x * lax.rsqrt(jnp.mean(x * x, axis=-1, keepdims=True) + EPS)


def _split_hl(a):
    hi = a.astype(BF16)
    lo = (a - hi.astype(F32)).astype(BF16)
    return jnp.concatenate([hi, lo], axis=-1)


def _resident(shape):
    nd = len(shape)
    return pl.BlockSpec(shape, lambda *_: (0,) * nd, pipeline_mode=pl.Buffered(1))


def _resident_layer(shape, layer):
    nd = len(shape)
    return pl.BlockSpec((1,) + tuple(shape[1:]), lambda *_: (layer,) + (0,) * (nd - 1),
                        pipeline_mode=pl.Buffered(1))


def _proj_a_kernel(x_ref, g_ref, wq_ref, wkvt_ref, q_ref, kt_ref, vt_ref, *, n_heads):
    xn = (_rms_unit(x_ref[0]) * g_ref[...]).astype(BF16)
    heads_per_dot = 4
    width = heads_per_dot * HEAD_DIM
    for c in range(n_heads // heads_per_dot):
        r = jnp.dot(xn, wq_ref[:, c * width:(c + 1) * width], preferred_element_type=F32)
        for hh in range(heads_per_dot):
            q_ref[0, c * heads_per_dot + hh] = r[:, hh * HEAD_DIM:(hh + 1) * HEAD_DIM].astype(BF16)
    for c in range(2 * n_heads // heads_per_dot):
        r = lax.dot_general(wkvt_ref[c * width:(c + 1) * width, :], xn, _NT, preferred_element_type=F32)
        for hh in range(heads_per_dot):
            h = c * heads_per_dot + hh
            piece = r[hh * HEAD_DIM:(hh + 1) * HEAD_DIM, :]
            if h < n_heads:
                kt_ref[0, h] = piece
            else:
                vt_ref[0, h - n_heads] = piece


def _proj_a(x, g, wq, wkvt):
    bx, t, d = x.shape
    n_heads = wq.shape[1] // HEAD_DIM
    tm = min(t, 256)
    return pl.pallas_call(
        functools.partial(_proj_a_kernel, n_heads=n_heads),
        grid=(bx, t // tm),
        in_specs=[pl.BlockSpec((1, tm, d), lambda b, i: (b, i, 0)),
                  _resident((1, d)), _resident(wq.shape), _resident(wkvt.shape)],
        out_specs=[pl.BlockSpec((1, n_heads, tm, HEAD_DIM), lambda b, i: (b, 0, i, 0)),
                   pl.BlockSpec((1, n_heads, HEAD_DIM, tm), lambda b, i: (b, 0, 0, i)),
                   pl.BlockSpec((1, n_heads, HEAD_DIM, tm), lambda b, i: (b, 0, 0, i))],
        out_shape=[jax.ShapeDtypeStruct((bx, n_heads, t, HEAD_DIM), BF16),
                   jax.ShapeDtypeStruct((bx, n_heads, HEAD_DIM, t), F32),
                   jax.ShapeDtypeStruct((bx, n_heads, HEAD_DIM, t), F32)],
        compiler_params=_cparams(("parallel", "parallel")),
        name="proj_a",
    )(x, g, wq, wkvt)


SB_DEAD = -104.0
SB_WINDOW_BLOCKS = 3
SB_SAMPLE_WINDOW = 256


def _suffix_neg_ones(n):
    r = lax.broadcasted_iota(jnp.int32, (n, n), 0)
    c = lax.broadcasted_iota(jnp.int32, (n, n), 1)
    return jnp.where(r >= c, -1.0, 0.0).astype(BF16)


def _strict_mask(n):
    row = lax.broadcasted_iota(jnp.int32, (n, n), 0)
    col = lax.broadcasted_iota(jnp.int32, (n, n), 1)
    return col < row


def _softplus(z):
    return jnp.maximum(z, 0.0) + jnp.log(1.0 + jnp.exp2(jnp.abs(z) * -LOG2E))


def _sb_strips(qs, kts, vts, carries, accs, uu, masks):
    n = len(qs)
    blk = uu.shape[1]
    nb = kts[0].shape[1] // blk
    zs = [jnp.dot(qs[i], kts[i], preferred_element_type=F32) for i in range(n)]
    sps = [_softplus(z) for z in zs]
    ws = [[None] * nb for _ in range(n)]
    carries = list(carries)
    for b in reversed(range(nb)):
        sl = slice(b * blk, (b + 1) * blk)
        spbs = [sp[:, sl] if masks[b] is None else jnp.where(masks[b], sp[:, sl], 0.0) for sp in sps]
        sufs = [jnp.dot(spb.astype(BF16), uu, preferred_element_type=F32) for spb in spbs]
        for i in range(n):
            wb = jnp.exp(zs[i][:, sl] + sufs[i] + carries[i])
            if masks[b] is not None:
                wb = jnp.where(masks[b], wb, 0.0)
            ws[i][b] = wb.astype(BF16)
            carries[i] = carries[i] + sufs[i][:, 0:1]
    wcat = [w[0] if nb == 1 else jnp.concatenate(w, axis=-1) for w in ws]
    accs = [accs[i] + lax.dot_general(wcat[i], vts[i], _NT, preferred_element_type=F32) for i in range(n)]
    return carries, accs


SB_STAGES = 5


def _sb_prompt_fast_kernel(q_ref, kt_ref, vt_ref, o_ref, flag_ref, z_ref, sp_ref, suf_ref, w_ref, *, heads):
    t = q_ref.shape[2]
    blk = SB_BLOCK
    nq = t // blk
    win = SB_WINDOW_BLOCKS
    wl = win * blk
    ring = z_ref.shape[0]
    first = win - 1
    uu = _suffix_neg_ones(blk)
    strict = _strict_mask(blk)
    z_ref[...] = jnp.zeros(z_ref.shape, F32)
    sp_ref[...] = jnp.zeros(sp_ref.shape, BF16)
    suf_ref[...] = jnp.zeros(suf_ref.shape, F32)
    w_ref[...] = jnp.zeros(w_ref.shape, BF16)

    for i in range(first):
        _, accs = _sb_strips([q_ref[0, h, i * blk:(i + 1) * blk, :] for h in range(heads)],
                             [kt_ref[0, h, :, 0:(i + 1) * blk].astype(BF16) for h in range(heads)],
                             [vt_ref[0, h, :, 0:(i + 1) * blk].astype(BF16) for h in range(heads)],
                             [jnp.zeros((blk, 1), F32)] * heads, [jnp.zeros((blk, HEAD_DIM), F32)] * heads,
                             uu, [None] * i + [strict])
        o_ref[0, i * blk:(i + 1) * blk, :] = jnp.concatenate(accs, axis=-1).astype(BF16)

    def where_is(n):
        i = jnp.clip(first + n, first, nq - 1)
        q0 = pl.multiple_of(i * blk, blk)
        k0 = pl.multiple_of((i - first) * blk, blk)
        return q0, k0, lax.rem(n + ring * SB_STAGES, ring)

    def body(n, worst):
        q0, k0, _ = where_is(n - 4)
        accs = []
        for h in range(heads):
            vt = vt_ref[0, h, :, pl.ds(k0, wl)].astype(BF16)
            accs.append(lax.dot_general(w_ref[h], vt, _NT, preferred_element_type=F32))
        o_ref[0, pl.ds(q0, blk), :] = jnp.concatenate(accs, axis=-1).astype(BF16)

        _, _, slot = where_is(n - 3)
        counts = n - 3 >= 1
        for h in range(heads):
            carry = jnp.zeros((blk, 1), F32)
            for b in reversed(range(win)):
                sl = slice(b * blk, (b + 1) * blk)
                suffix = suf_ref[h, b]
                wb = jnp.exp(z_ref[slot, h, :, sl] + suffix + carry)
                if b == win - 1:
                    wb = jnp.where(strict, wb, 0.0)
                w_ref[h, :, sl] = wb.astype(BF16)
                carry = carry + suffix[:, 0:1]
            worst = jnp.maximum(worst, jnp.where(counts, carry, -jnp.inf))

        for h in range(heads):
            suf_ref[h] = jnp.dot(sp_ref[h].reshape(win * blk, blk), uu,
                                 preferred_element_type=F32).reshape(win, blk, blk)

        _, _, slot = where_is(n - 1)
        for h in range(heads):
            sp = _softplus(z_ref[slot, h])
            for b in range(win):
                spb = sp[:, b * blk:(b + 1) * blk]
                if b == win - 1:
                    spb = jnp.where(strict, spb, 0.0)
                sp_ref[h, b] = spb.astype(BF16)

        q0, k0, slot = where_is(n)
        for h in range(heads):
            kt = kt_ref[0, h, :, pl.ds(k0, wl)].astype(BF16)
            z_ref[slot, h] = jnp.dot(q_ref[0, h, pl.ds(q0, blk), :], kt, preferred_element_type=F32)
        return worst

    worst = lax.fori_loop(0, nq - first + SB_STAGES - 1, body, jnp.full((blk, 1), -jnp.inf, F32))
    flag_ref[...] = jnp.broadcast_to(jnp.max(worst), flag_ref.shape)


def _sb_prompt_full_kernel(q_ref, kt_ref, vt_ref, o_ref, *, heads):
    t = q_ref.shape[2]
    blk = SB_BLOCK
    uu = _suffix_neg_ones(blk)
    strict = _strict_mask(blk)

    def q_block(i, _):
        q0 = pl.multiple_of(i * blk, blk)
        qs = [q_ref[0, h, pl.ds(q0, blk), :] for h in range(heads)]

        def strips(k0, carries, accs, masks):
            return _sb_strips(qs, [kt_ref[0, h, :, pl.ds(k0, blk)].astype(BF16) for h in range(heads)],
                              [vt_ref[0, h, :, pl.ds(k0, blk)].astype(BF16) for h in range(heads)],
                              carries, accs, uu, masks)

        state = strips(q0, [jnp.zeros((blk, 1), F32)] * heads, [jnp.zeros((blk, HEAD_DIM), F32)] * heads,
                       [strict])

        def k_block(jj, st):
            carries, accs = strips(pl.multiple_of((i - 1 - jj) * blk, blk), st[0], st[1], [None])
            return tuple(carries), tuple(accs)

        _, accs = lax.fori_loop(0, i, k_block, (tuple(state[0]), tuple(state[1])))
        o_ref[0, pl.ds(q0, blk), :] = jnp.concatenate(list(accs), axis=-1).astype(BF16)
        return 0

    lax.fori_loop(0, t // blk, q_block, 0)


def _sb_prompt(q, kt, vt):
    b, n_heads, t, _ = q.shape
    heads = 4
    wl = SB_WINDOW_BLOCKS * SB_BLOCK
    assert t >= wl
    q_blk = pl.BlockSpec((1, heads, t, HEAD_DIM), lambda bi, hg: (bi, hg, 0, 0))
    kv_blk = pl.BlockSpec((1, heads, HEAD_DIM, t), lambda bi, hg: (bi, hg, 0, 0))
    o_blk = pl.BlockSpec((1, t, heads * HEAD_DIM), lambda bi, hg: (bi, 0, hg))
    o_shape = jax.ShapeDtypeStruct((b, t, n_heads * HEAD_DIM), BF16)
    grid = (b, n_heads // heads)
    o_fast, flags = pl.pallas_call(
        functools.partial(_sb_prompt_fast_kernel, heads=heads),
        grid=grid,
        in_specs=[q_blk, kv_blk, kv_blk],
        out_specs=[o_blk, pl.BlockSpec((1, 1, 8, 128), lambda bi, hg: (bi, hg, 0, 0))],
        out_shape=[o_shape, jax.ShapeDtypeStruct((b, n_heads // heads, 8, 128), F32)],
        scratch_shapes=[pltpu.VMEM((SB_STAGES - 1, heads, SB_BLOCK, wl), F32),
                        pltpu.VMEM((heads, SB_WINDOW_BLOCKS, SB_BLOCK, SB_BLOCK), BF16),
                        pltpu.VMEM((heads, SB_WINDOW_BLOCKS, SB_BLOCK, SB_BLOCK), F32),
                        pltpu.VMEM((heads, SB_BLOCK, wl), BF16)],
        compiler_params=_cparams(("parallel", "parallel")),
        name="sb_prompt_fast",
    )(q, kt, vt)
    if t // SB_BLOCK <= SB_WINDOW_BLOCKS:
        return o_fast

    def full():
        return pl.pallas_call(
            functools.partial(_sb_prompt_full_kernel, heads=heads),
            grid=grid,
            in_specs=[q_blk, kv_blk, kv_blk],
            out_specs=o_blk,
            out_shape=o_shape,
            compiler_params=_cparams(("parallel", "parallel")),
            name="sb_prompt_full",
        )(q, kt, vt)

    return lax.cond(jnp.max(flags) > SB_DEAD, full, lambda: o_fast)


def _sb_sample_kernel(q_ref, ktn_ref, vtn_ref, ktc_ref, vtc_ref, o_ref, *maybe_flag, heads, strip):
    s = q_ref.shape[2]
    width = ktc_ref.shape[4]
    blk = SB_BLOCK
    uu = _suffix_neg_ones(blk)
    uu_new = _suffix_neg_ones(s)
    strict = _strict_mask(s)

    qs = [q_ref[0, h] for h in range(heads)]
    state = _sb_strips(qs, [ktn_ref[0, h].astype(BF16) for h in range(heads)],
                       [vtn_ref[0, h].astype(BF16) for h in range(heads)],
                       [jnp.zeros((s, 1), F32)] * heads, [jnp.zeros((s, HEAD_DIM), F32)] * heads,
                       uu_new, [strict])

    def k_strip(jj, st):
        k0 = pl.multiple_of(width - (jj + 1) * strip * blk, blk)
        carries, accs = _sb_strips(
            qs, [ktc_ref[0, 0, h, :, pl.ds(k0, strip * blk)].astype(BF16) for h in range(heads)],
            [vtc_ref[0, 0, h, :, pl.ds(k0, strip * blk)].astype(BF16) for h in range(heads)],
            st[0], st[1], uu, [None] * strip)
        return tuple(carries), tuple(accs)

    carries, accs = lax.fori_loop(0, width // (strip * blk), k_strip, (tuple(state[0]), tuple(state[1])))
    o_ref[0] = jnp.concatenate(list(accs), axis=-1).astype(BF16)
    if maybe_flag:
        worst = carries[0]
        for h in range(1, heads):
            worst = jnp.maximum(worst, carries[h])
        maybe_flag[0][...] = jnp.broadcast_to(jnp.max(worst), maybe_flag[0].shape)


def _sb_sample(q, kt_new, vt_new, cache_kt, cache_vt, layer):
    b, n_heads, s, _ = q.shape
    past = cache_kt.shape[4]
    width = min(past, SB_SAMPLE_WINDOW)
    assert past % width == 0 and width % SB_BLOCK == 0
    last = past // width - 1
    o_shape = jax.ShapeDtypeStruct((b, s, n_heads * HEAD_DIM), BF16)

    def specs(heads, cache_width, cache_block):
        q_blk = pl.BlockSpec((1, heads, s, HEAD_DIM), lambda bi, hg: (bi, hg, 0, 0))
        new_blk = pl.BlockSpec((1, heads, HEAD_DIM, s), lambda bi, hg: (bi, hg, 0, 0))
        cache_blk = pl.BlockSpec((1, 1, heads, HEAD_DIM, cache_width),
                                 lambda bi, hg: (layer, bi, hg, 0, cache_block))
        o_blk = pl.BlockSpec((1, s, heads * HEAD_DIM), lambda bi, hg: (bi, 0, hg))
        return (b, n_heads // heads), [q_blk, new_blk, new_blk, cache_blk, cache_blk], o_blk

    heads = n_heads
    grid, in_specs, o_blk = specs(heads, width, last)
    o_fast, flags = pl.pallas_call(
        functools.partial(_sb_sample_kernel, heads=heads, strip=width // SB_BLOCK),
        grid=grid,
        in_specs=in_specs,
        out_specs=[o_blk, pl.BlockSpec((1, 1, 8, 128), lambda bi, hg: (bi, hg, 0, 0))],
        out_shape=[o_shape, jax.ShapeDtypeStruct((b, n_heads // heads, 8, 128), F32)],
        compiler_params=_cparams(("parallel", "parallel")),
        name="sb_sample_fast",
    )(q, kt_new, vt_new, cache_kt, cache_vt)
    if width == past:
        return o_fast

    def full():
        heads = 4
        grid, in_specs, o_blk = specs(heads, past, 0)
        return pl.pallas_call(
            functools.partial(_sb_sample_kernel, heads=heads, strip=1),
            grid=grid,
            in_specs=in_specs,
            out_specs=o_blk,
            out_shape=o_shape,
            compiler_params=_cparams(("parallel", "parallel")),
            name="sb_sample_full",
        )(q, kt_new, vt_new, cache_kt, cache_vt)

    return lax.cond(jnp.max(flags) > SB_DEAD, full, lambda: o_fast)


def _route(logits):
    tm = logits.shape[0]
    lane = lax.broadcasted_iota(jnp.int32, (tm, ROUTER_LANES), 1)
    lane_f = lane.astype(F32)
    neg = -jnp.inf
    first = lambda hit: jnp.min(jnp.where(hit, lane_f, float(ROUTER_LANES)), axis=-1, keepdims=True)

    gl = jnp.where(lane < N_GROUPS, logits, neg)
    g_max = jnp.max(gl, axis=-1, keepdims=True)
    g_top = 1.0 / jnp.sum(jnp.exp(gl - g_max), axis=-1, keepdims=True)
    g_idx = first(gl == g_max)

    lo = N_GROUPS + g_idx * EXPERTS_PER_GROUP
    in_group = (lane_f >= lo) & (lane_f < lo + EXPERTS_PER_GROUP)
    sel = jnp.where(in_group, logits, neg)
    t1 = jnp.max(sel, axis=-1, keepdims=True)
    i1 = first(sel == t1)
    sel2 = jnp.where(lane_f == i1, neg, sel)
    t2 = jnp.max(sel2, axis=-1, keepdims=True)
    i2 = first(sel2 == t2)
    e2 = jnp.exp(t2 - t1)
    den = 1.0 + e2
    w1 = (1.0 / den) * g_top
    w2 = (e2 / den) * g_top
    return jnp.where(lane_f == i1, w1, 0.0) + jnp.where(lane_f == i2, w2, 0.0)


CHANNEL_ROW_SPLITS = 2
EXPERTS_PER_TRIP = 2


def _channel_kernel(x_ref, o_ref, p_ref, wo_ref, gffn_ref, wr_ref, wg_ref, wu_ref, wd_ref, gple_ref,
                    wpg_ref, wpp_ref, out_ref, acc_ref):
    tm = x_ref.shape[0]
    d_expert = wd_ref.shape[2]
    half = tm // CHANNEL_ROW_SPLITS
    rows = [slice(i * half, (i + 1) * half) for i in range(CHANNEL_ROW_SPLITS)]

    h1s = [x_ref[r, :] + jnp.dot(o_ref[r, :], wo_ref[...], preferred_element_type=F32) for r in rows]
    xns = [_rms_unit(h1) * gffn_ref[...] for h1 in h1s]
    xnbs = [xn.astype(BF16) for xn in xns]
    xlos = [(xn - xnb.astype(F32)).astype(BF16) for xn, xnb in zip(xns, xnbs)]
    logits = [jnp.dot(jnp.concatenate([xnb, xlo, xnb], axis=-1), wr_ref[...], preferred_element_type=F32)
              for xnb, xlo in zip(xnbs, xlos)]
    gates = jnp.concatenate([_route(l) for l in logits], axis=0)
    xnb = jnp.concatenate(xnbs, axis=0)
    lane = lax.broadcasted_iota(jnp.int32, gates.shape, 1)
    for r, h1 in zip(rows, h1s):
        acc_ref[r, :] = h1

    def experts(j, _):
        es = [j * EXPERTS_PER_TRIP + k for k in range(EXPERTS_PER_TRIP)]
        gs = [jnp.dot(xnb, wg_ref[0, e], preferred_element_type=F32) for e in es]
        us = [jnp.dot(xnb, wu_ref[0, e], preferred_element_type=F32) for e in es]
        hids = []
        for e, g, u in zip(es, gs, us):
            gate_e = jnp.sum(jnp.where(lane == N_GROUPS + e, gates, 0.0), axis=-1, keepdims=True)
            hids.append(((g * jax.nn.sigmoid(g)) * u * gate_e).astype(BF16))
        wd = wd_ref[0, pl.ds(j * EXPERTS_PER_TRIP, EXPERTS_PER_TRIP)]
        acc_ref[...] += jnp.dot(jnp.concatenate(hids, axis=-1),
                                wd.reshape(EXPERTS_PER_TRIP * d_expert, wd.shape[-1]),
                                preferred_element_type=F32)
        return 0

    lax.fori_loop(0, N_EXPERTS // EXPERTS_PER_TRIP, experts, 0)
    h2s = [acc_ref[r, :] for r in rows]
    x3s = [(_rms_unit(h2) * gple_ref[...]).astype(BF16) for h2 in h2s]
    ple_gates = [jax.nn.sigmoid(jnp.dot(x3, wpg_ref[0], preferred_element_type=F32)) for x3 in x3s]
    projs = [jnp.dot(p_ref[0, r, :].astype(BF16), wpp_ref[0], preferred_element_type=F32) for r in rows]
    for r, h2, proj, gate in zip(rows, h2s, projs, ple_gates):
        out_ref[r, :] = h2 + proj * gate


def _channel(x, o, p, layer, w, stacks):
    n, d = x.shape
    tm = min(n, 512)
    row = lambda cols: pl.BlockSpec((tm, cols), lambda i: (i, 0))
    small = [w["wo"], w["gffn"], w["wr"]]
    moe = [stacks["wg"], stacks["wu"], stacks["wd"]]
    ple = [stacks["wpg"], stacks["wpp"]]
    return pl.pallas_call(
        _channel_kernel,
        grid=(n // tm,),
        in_specs=([row(d), row(d), pl.BlockSpec((1, tm, p.shape[2]), lambda i: (layer, i, 0))]
                  + [_resident(a.shape) for a in small] + [_resident_layer(a.shape, layer) for a in moe]
                  + [_resident(w["gple"].shape)] + [_resident_layer(a.shape, layer) for a in ple]),
        out_specs=row(d),
        out_shape=jax.ShapeDtypeStruct((n, d), F32),
        scratch_shapes=[pltpu.VMEM((tm, d), F32)],
        compiler_params=_cparams(("parallel",)),
        name="channel",
    )(x, o, p, *small, *moe, w["gple"], *ple)


def _proj_b_kernel(h_ref, ga_ref, wq_ref, gq_ref, gkv_ref, wkv_ref, gk_ref, gsum_ref, gexp_ref,
                   q_ref, k_ref, v_ref, *, n_heads, n_kv):
    y = _rms_unit(h_ref[0])
    q = jnp.dot((y * ga_ref[...]).astype(BF16), wq_ref[...], preferred_element_type=F32)
    ms = jnp.dot((q * q).astype(BF16), gsum_ref[...], preferred_element_type=F32) * (1.0 / HEAD_DIM)
    inv = lax.rsqrt(ms + EPS)
    inv_full = jnp.dot(_split_hl(inv), gexp_ref[...], preferred_element_type=F32)
    qn = (q * inv_full * gq_ref[...]) * (HEAD_DIM ** -0.5)
    for h in range(n_heads):
        q_ref[0, h] = qn[:, h * HEAD_DIM:(h + 1) * HEAD_DIM].astype(BF16)
    kv = jnp.dot((y * gkv_ref[...]).astype(BF16), wkv_ref[...], preferred_element_type=F32)
    for h in range(n_kv):
        kh = kv[:, h * HEAD_DIM:(h + 1) * HEAD_DIM]
        k_ref[0, h] = _rms_unit(kh) * gk_ref[...]
        v_ref[0, h] = kv[:, (n_kv + h) * HEAD_DIM:(n_kv + h + 1) * HEAD_DIM]


def _proj_b(h, ga, wq, gq, gkv, wkv, gk, gsum, gexp):
    bx, t, d = h.shape
    n_heads = wq.shape[1] // HEAD_DIM
    n_kv = wkv.shape[1] // (2 * HEAD_DIM)
    tm = min(t, 256)
    hm = lambda b, i: (b, 0, i, 0)
    ins = [ga, wq, gq, gkv, wkv, gk, gsum, gexp]
    return pl.pallas_call(
        functools.partial(_proj_b_kernel, n_heads=n_heads, n_kv=n_kv),
        grid=(bx, t // tm),
        in_specs=[pl.BlockSpec((1, tm, d), lambda b, i: (b, i, 0))] + [_resident(a.shape) for a in ins],
        out_specs=[pl.BlockSpec((1, n_heads, tm, HEAD_DIM), hm),
                   pl.BlockSpec((1, n_kv, tm, HEAD_DIM), hm),
                   pl.BlockSpec((1, n_kv, tm, HEAD_DIM), hm)],
        out_shape=[jax.ShapeDtypeStruct((bx, n_heads, t, HEAD_DIM), BF16),
                   jax.ShapeDtypeStruct((bx, n_kv, t, HEAD_DIM), F32),
                   jax.ShapeDtypeStruct((bx, n_kv, t, HEAD_DIM), F32)],
        compiler_params=_cparams(("parallel", "parallel")),
        name="proj_b",
    )(h, *ins)


def _t5_bucket(rel):
    nb = NUM_BUCKETS // 2
    max_exact = nb // 2
    n = jnp.abs(rel)
    large = max_exact + (jnp.log(jnp.maximum(n, 1).astype(jnp.float32) / max_exact)
                         / math.log(MAX_DISTANCE / max_exact) * (nb - max_exact)).astype(jnp.int32)
    large = jnp.minimum(large, nb - 1)
    return jnp.where(rel > 0, nb, 0) + jnp.where(n < max_exact, n, large)


def _bias_kernel(bucket_ref, rbt_ref, out_ref):
    bucket = bucket_ref[...]
    acc = jnp.zeros(out_ref.shape, F32)
    for b in range(NUM_BUCKETS):
        acc = acc + jnp.where(bucket == b, rbt_ref[:, b:b + 1], 0.0)
    out_ref[...] = acc


def _bias_table(rel_bias, nq, nk, key_offset):
    rel = (jnp.arange(nk, dtype=jnp.int32)[None, :] - key_offset) - jnp.arange(nq, dtype=jnp.int32)[:, None]
    bucket = _t5_bucket(rel).reshape(1, nq * nk)
    n_heads = rel_bias.shape[1]
    out = pl.pallas_call(
        _bias_kernel,
        out_shape=jax.ShapeDtypeStruct((n_heads, nq * nk), F32),
        name="bias_table",
    )(bucket, rel_bias.T)
    return out.reshape(n_heads, nq, nk)


def _swa_kernel(sink_ref, q_ref, k_ref, v_ref, bias_ref, o_ref, kpad_ref, vpad_ref, *, cq, wl, pad, n_chunks,
                group):
    n_kv, tk = k_ref.shape[1], k_ref.shape[2]
    if pad:
        kpad_ref[:, 0:pad, :] = jnp.zeros((n_kv, pad, HEAD_DIM), BF16)
        vpad_ref[:, 0:pad, :] = jnp.zeros((n_kv, pad, HEAD_DIM), BF16)
    kpad_ref[:, pad:pad + tk, :] = k_ref[0].astype(BF16)
    vpad_ref[:, pad:pad + tk, :] = v_ref[0].astype(BF16)

    def chunk(c, _):
        r0 = pl.multiple_of(c * cq, cq)
        col = lax.broadcasted_iota(jnp.int32, (cq, wl), 1) + r0
        valid = col >= pad
        logits = []
        for kv in range(n_kv):
            kw = kpad_ref[kv, pl.ds(r0, wl), :]
            qg = q_ref[0, kv * group:(kv + 1) * group, pl.ds(r0, cq), :].reshape(group * cq, HEAD_DIM)
            logits.append(lax.dot_general(qg, kw, _NT, preferred_element_type=F32))
        es, dens = [], []
        for h in range(n_kv * group):
            l = logits[h // group][(h % group) * cq:(h % group + 1) * cq] + bias_ref[h]
            if pad:
                l = jnp.where(valid, l, -jnp.inf)
            sink = sink_ref[h]
            m = jnp.maximum(jnp.max(l, axis=-1, keepdims=True), sink)
            e = jnp.exp(l - m)
            dens.append(jnp.sum(e, axis=-1, keepdims=True) + jnp.exp(sink - m))
            es.append(e.astype(BF16))
        pvs = []
        for kv in range(n_kv):
            vw = vpad_ref[kv, pl.ds(r0, wl), :]
            pvs.append(jnp.dot(jnp.concatenate(es[kv * group:(kv + 1) * group], axis=0), vw,
                               preferred_element_type=F32))
        outs = [pvs[h // group][(h % group) * cq:(h % group + 1) * cq] / dens[h] for h in range(n_kv * group)]
        o_ref[0, pl.ds(r0, cq), :] = jnp.concatenate(outs, axis=-1).astype(BF16)
        return 0

    lax.fori_loop(0, n_chunks, chunk, 0)


def _swa(q, k_win, v_win, bias, sinks, *, cq, wl, pad):
    b, n_heads, tq, _ = q.shape
    n_kv, tk = k_win.shape[1], k_win.shape[2]
    n_chunks = tq // cq
    assert (n_chunks - 1) * cq + wl == pad + tk
    kv_blk = pl.BlockSpec((1, n_kv, tk, HEAD_DIM), lambda bi: (bi, 0, 0, 0))
    return pl.pallas_call(
        functools.partial(_swa_kernel, cq=cq, wl=wl, pad=pad, n_chunks=n_chunks, group=n_heads // n_kv),
        grid=(b,),
        in_specs=[pl.BlockSpec(memory_space=pltpu.SMEM),
                  pl.BlockSpec((1, n_heads, tq, HEAD_DIM), lambda bi: (bi, 0, 0, 0)),
                  kv_blk, kv_blk, _resident(bias.shape)],
        out_specs=pl.BlockSpec((1, tq, n_heads * HEAD_DIM), lambda bi: (bi, 0, 0)),
        out_shape=jax.ShapeDtypeStruct((b, tq, n_heads * HEAD_DIM), BF16),
        scratch_shapes=[pltpu.VMEM((n_kv, pad + tk, HEAD_DIM), BF16),
                        pltpu.VMEM((n_kv, pad + tk, HEAD_DIM), BF16)],
        compiler_params=_cparams(("parallel",)),
        name="swa",
    )(sinks, q, k_win, v_win, bias)


def _row(v):
    return v.reshape(1, -1).astype(F32)


def _router_hi_hi_lo(wr):
    hi = wr.astype(BF16)
    lo = (wr - hi.astype(F32)).astype(BF16)
    return jnp.concatenate([hi, hi, lo], axis=0)


def _prep_weights(prm):
    d = prm["a_w_o"].shape[1]
    depth = prm["norm_ffn"].shape[0]
    n_a = prm["a_w_qkv"].shape[0]
    scale = HEAD_DIM ** -0.5
    w = {"channel": [], "n_a": n_a, "depth": depth}
    for i in range(depth):
        wo = prm["a_w_o"][i] if i < n_a else prm["b_w_o"][i - n_a]
        pad = ROUTER_LANES - N_GROUPS - N_EXPERTS
        wr = jnp.concatenate([prm["moe_w_group"][i], prm["moe_w_router"][i].reshape(d, N_EXPERTS),
                              jnp.zeros((d, pad), F32)], axis=1)
        w["channel"].append({"wo": wo.astype(BF16), "gffn": _row(prm["norm_ffn"][i]),
                             "wr": _router_hi_hi_lo(wr), "gple": _row(prm["norm_ple"][i])})
    w["stacks"] = {"wg": prm["moe_w_gate"].astype(BF16), "wu": prm["moe_w_up"].astype(BF16),
                   "wd": prm["moe_w_down"].astype(BF16), "wpg": prm["ple_w_gate"].astype(BF16),
                   "wpp": prm["ple_w_proj"].astype(BF16)}
    w["qkv"] = []
    for i in range(n_a):
        wq = prm["a_w_qkv"][i]
        hd = wq.shape[1] // 3
        w["qkv"].append(((wq[:, :hd] * scale).astype(BF16),
                         wq[:, hd:].T.astype(BF16)))
    n_heads = prm["b_w_q"].shape[2] // HEAD_DIM
    head_of = jnp.arange(n_heads * HEAD_DIM, dtype=jnp.int32) // HEAD_DIM
    lanes = jnp.arange(ROUTER_LANES, dtype=jnp.int32)
    member = (head_of[:, None] == lanes[None, :]).astype(BF16)
    w["gsum"] = member
    w["gexp"] = jnp.concatenate([member.T, member.T], axis=0)
    w["wq_b"] = [prm["b_w_q"][j].astype(BF16) for j in range(depth - n_a)]
    w["gq_b"] = [_row(jnp.tile(prm["b_q_norm"][j], n_heads)) for j in range(depth - n_a)]
    w["wkv"] = prm["b_w_kv"].astype(BF16)
    return w


def _assert_sample_window_visible(past_len, s, tk):
    q_chunk = [(past_len + i) // CHUNK for i in range(s)]
    k_pos = [past_len + s - tk + j for j in range(tk)]
    ok = all(kp >= 0 and qc - WIN_CHUNKS <= kp // CHUNK <= qc for qc in q_chunk for kp in k_pos)
    if not ok:
        raise NotImplementedError("sample window with masked keys")


def _run_trunk(x, p, prm, w, sb_cache_k=None, sb_cache_v=None, swa_cache_k=None, swa_cache_v=None):
    bx, t, d = x.shape
    n_a, depth = w["n_a"], w["depth"]
    sample = sb_cache_k is not None
    h = x
    sb_k, sb_v = [], []
    k_win = v_win = None
    q_b = None
    for i in range(depth):
        if i < n_a:
            q, kt, vt = _proj_a(h, _row(prm["norm_attn"][i]), *w["qkv"][i])
            sb_k.append(kt)
            sb_v.append(vt)
            if sample:
                o = _sb_sample(q, kt, vt, jnp.swapaxes(sb_cache_k, -1, -2), jnp.swapaxes(sb_cache_v, -1, -2), i)
            else:
                o = _sb_prompt(q, kt, vt)
        else:
            j = i - n_a
            if j > 0:
                raise NotImplementedError("one B layer supported")
            if sample:
                tk = k_win.shape[2]
                _assert_sample_window_visible(sb_cache_k.shape[3], t, tk)
                o = _swa(q_b, k_win, v_win, _bias_table(prm["rel_bias"], t, tk, tk - t),
                         prm["b_sinks"][j], cq=t, wl=tk, pad=0)
            else:
                wl = WINDOW + CHUNK
                o = _swa(q_b, k_win, v_win, _bias_table(prm["rel_bias"], CHUNK, wl, WINDOW),
                         prm["b_sinks"][j], cq=CHUNK, wl=wl, pad=WINDOW)
        h = _channel(h.reshape(bx * t, d), o.reshape(bx * t, d), p.reshape(depth, bx * t, -1), i,
                     w["channel"][i], w["stacks"]).reshape(bx, t, d)
        if i == n_a - 1:
            q_b, k_s, v_s = _proj_b(h, _row(prm["norm_attn"][n_a]), w["wq_b"][0], w["gq_b"][0],
                                    _row(prm["kv_norm"]), w["wkv"], _row(prm["b_k_norm"]),
                                    w["gsum"], w["gexp"])
            if sample:
                k_win = jnp.concatenate([swa_cache_k, k_s], axis=2)
                v_win = jnp.concatenate([swa_cache_v, v_s], axis=2)
            else:
                k_win, v_win = k_s, v_s
    sb_k = jnp.swapaxes(jnp.stack(sb_k), -1, -2)
    sb_v = jnp.swapaxes(jnp.stack(sb_v), -1, -2)
    return h, sb_k, sb_v, k_win[:, :, -WINDOW:], v_win[:, :, -WINDOW:]


def kernel(x_prompt, x_sample, p_prompt, p_sample, cache_sb_k, cache_sb_v, cache_swa_k, cache_swa_v, norm_attn, norm_ffn, norm_ple, a_w_qkv, a_w_o, kv_norm, b_w_kv, b_k_norm, b_w_q, b_q_norm, b_sinks, b_w_o, rel_bias, moe_w_group, moe_w_router, moe_w_gate, moe_w_up, moe_w_down, ple_w_proj, ple_w_gate):
    prm = {
        "norm_attn": norm_attn, "norm_ffn": norm_ffn, "norm_ple": norm_ple,
        "a_w_qkv": a_w_qkv, "a_w_o": a_w_o, "kv_norm": kv_norm, "b_w_kv": b_w_kv, "b_k_norm": b_k_norm,
        "b_w_q": b_w_q, "b_q_norm": b_q_norm, "b_sinks": b_sinks, "b_w_o": b_w_o, "rel_bias": rel_bias,
        "moe_w_group": moe_w_group, "moe_w_router": moe_w_router, "moe_w_gate": moe_w_gate,
        "moe_w_up": moe_w_up, "moe_w_down": moe_w_down, "ple_w_proj": ple_w_proj, "ple_w_gate": ple_w_gate,
    }
    w = _prep_weights(prm)
    y_p, sb_k_p, sb_v_p, swa_k_p, swa_v_p = _run_trunk(x_prompt, p_prompt, prm, w)
    y_s, sb_k_s, sb_v_s, swa_k_s, swa_v_s = _run_trunk(x_sample, p_sample, prm, w, cache_sb_k, cache_sb_v,
                                                       cache_swa_k, cache_swa_v)
    return (y_p, y_s, sb_k_p, sb_v_p, swa_k_p, swa_v_p, sb_k_s, sb_v_s, swa_k_s, swa_v_s)
```

```python
import functools
import math

import jax
import jax.numpy as jnp
from jax import lax
from jax.experimental import pallas as pl
from jax.experimental.pallas import tpu as pltpu

F32 = jnp.float32
BF16 = jnp.bfloat16

HEAD_DIM = 64
CHUNK = 64
WINDOW = 128
WIN_CHUNKS = WINDOW // CHUNK
NUM_BUCKETS = 32
MAX_DISTANCE = 128
N_GROUPS = 4
EXPERTS_PER_GROUP = 4
N_EXPERTS = N_GROUPS * EXPERTS_PER_GROUP
EPS = 1e-6
LOG2E = 1.4426950408889634
SB_BLOCK = 128
ROUTER_LANES = 128
VMEM_LIMIT = 56 * 1024 * 1024

_NT = (((1,), (1,)), ((), ()))


def _cparams(sem):
    return pltpu.CompilerParams(dimension_semantics=sem, vmem_limit_bytes=VMEM_LIMIT)


def _rms_unit(x):
    return x * lax.rsqrt(jnp.mean(x * x, axis=-1, keepdims=True) + EPS)


def _split_hl(a):
    hi = a.astype(BF16)
    lo = (a - hi.astype(F32)).astype(BF16)
    return jnp.concatenate([hi, lo], axis=-1)


def _resident(shape):
    nd = len(shape)
    return pl.BlockSpec(shape, lambda *_: (0,) * nd, pipeline_mode=pl.Buffered(1))


def _resident_layer(shape, layer):
    nd = len(shape)
    return pl.BlockSpec((1,) + tuple(shape[1:]), lambda *_: (layer,) + (0,) * (nd - 1),
                        pipeline_mode=pl.Buffered(1))


def _proj_a_kernel(x_ref, g_ref, wq_ref, wkvt_ref, q_ref, kt_ref, vt_ref, *, n_heads):
    xn = (_rms_unit(x_ref[0]) * g_ref[...]).astype(BF16)
    heads_per_dot = 4
    width = heads_per_dot * HEAD_DIM
    for c in range(n_heads // heads_per_dot):
        r = jnp.dot(xn, wq_ref[:, c * width:(c + 1) * width], preferred_element_type=F32)
        for hh in range(heads_per_dot):
            q_ref[0, c * heads_per_dot + hh] = r[:, hh * HEAD_DIM:(hh + 1) * HEAD_DIM].astype(BF16)
    for c in range(2 * n_heads // heads_per_dot):
        r = lax.dot_general(wkvt_ref[c * width:(c + 1) * width, :], xn, _NT, preferred_element_type=F32)
        for hh in range(heads_per_dot):
            h = c * heads_per_dot + hh
            piece = r[hh * HEAD_DIM:(hh + 1) * HEAD_DIM, :]
            if h < n_heads:
                kt_ref[0, h] = piece
            else:
                vt_ref[0, h - n_heads] = piece


def _proj_a(x, g, wq, wkvt):
    bx, t, d = x.shape
    n_heads = wq.shape[1] // HEAD_DIM
    tm = min(t, 256)
    return pl.pallas_call(
        functools.partial(_proj_a_kernel, n_heads=n_heads),
        grid=(bx, t // tm),
        in_specs=[pl.BlockSpec((1, tm, d), lambda b, i: (b, i, 0)),
                  _resident((1, d)), _resident(wq.shape), _resident(wkvt.shape)],
        out_specs=[pl.BlockSpec((1, n_heads, tm, HEAD_DIM), lambda b, i: (b, 0, i, 0)),
                   pl.BlockSpec((1, n_heads, HEAD_DIM, tm), lambda b, i: (b, 0, 0, i)),
                   pl.BlockSpec((1, n_heads, HEAD_DIM, tm), lambda b, i: (b, 0, 0, i))],
        out_shape=[jax.ShapeDtypeStruct((bx, n_heads, t, HEAD_DIM), BF16),
                   jax.ShapeDtypeStruct((bx, n_heads, HEAD_DIM, t), F32),
                   jax.ShapeDtypeStruct((bx, n_heads, HEAD_DIM, t), F32)],
        compiler_params=_cparams(("parallel", "parallel")),
        name="proj_a",
    )(x, g, wq, wkvt)


SB_DEAD = -104.0
SB_WINDOW_BLOCKS = 3
SB_SAMPLE_WINDOW = 256


def _suffix_neg_ones(n):
    r = lax.broadcasted_iota(jnp.int32, (n, n), 0)
    c = lax.broadcasted_iota(jnp.int32, (n, n), 1)
    return jnp.where(r >= c, -1.0, 0.0).astype(BF16)


def _strict_mask(n):
    row = lax.broadcasted_iota(jnp.int32, (n, n), 0)
    col = lax.broadcasted_iota(jnp.int32, (n, n), 1)
    return col < row


def _softplus(z):
    return jnp.maximum(z, 0.0) + jnp.log(1.0 + jnp.exp2(jnp.abs(z) * -LOG2E))


def _sb_strips(qs, kts, vts, carries, accs, uu, masks):
    n = len(qs)
    blk = uu.shape[1]
    nb = kts[0].shape[1] // blk
    zs = [jnp.dot(qs[i], kts[i], preferred_element_type=F32) for i in range(n)]
    sps = [_softplus(z) for z in zs]
    ws = [[None] * nb for _ in range(n)]
    carries = list(carries)
    for b in reversed(range(nb)):
        sl = slice(b * blk, (b + 1) * blk)
        spbs = [sp[:, sl] if masks[b] is None else jnp.where(masks[b], sp[:, sl], 0.0) for sp in sps]
        sufs = [jnp.dot(spb.astype(BF16), uu, preferred_element_type=F32) for spb in spbs]
        for i in range(n):
            wb = jnp.exp(zs[i][:, sl] + sufs[i] + carries[i])
            if masks[b] is not None:
                wb = jnp.where(masks[b], wb, 0.0)
            ws[i][b] = wb.astype(BF16)
            carries[i] = carries[i] + sufs[i][:, 0:1]
    wcat = [w[0] if nb == 1 else jnp.concatenate(w, axis=-1) for w in ws]
    accs = [accs[i] + lax.dot_general(wcat[i], vts[i], _NT, preferred_element_type=F32) for i in range(n)]
    return carries, accs


SB_STAGES = 5


def _sb_prompt_fast_kernel(q_ref, kt_ref, vt_ref, o_ref, flag_ref, z_ref, sp_ref, suf_ref, w_ref, *, heads):
    t = q_ref.shape[2]
    blk = SB_BLOCK
    nq = t // blk
    win = SB_WINDOW_BLOCKS
    wl = win * blk
    ring = z_ref.shape[0]
    first = win - 1
    uu = _suffix_neg_ones(blk)
    strict = _strict_mask(blk)
    z_ref[...] = jnp.zeros(z_ref.shape, F32)
    sp_ref[...] = jnp.zeros(sp_ref.shape, BF16)
    suf_ref[...] = jnp.zeros(suf_ref.shape, F32)
    w_ref[...] = jnp.zeros(w_ref.shape, BF16)

    for i in range(first):
        _, accs = _sb_strips([q_ref[0, h, i * blk:(i + 1) * blk, :] for h in range(heads)],
                             [kt_ref[0, h, :, 0:(i + 1) * blk].astype(BF16) for h in range(heads)],
                             [vt_ref[0, h, :, 0:(i + 1) * blk].astype(BF16) for h in range(heads)],
                             [jnp.zeros((blk, 1), F32)] * heads, [jnp.zeros((blk, HEAD_DIM), F32)] * heads,
                             uu, [None] * i + [strict])
        o_ref[0, i * blk:(i + 1) * blk, :] = jnp.concatenate(accs, axis=-1).astype(BF16)

    def where_is(n):
        i = jnp.clip(first + n, first, nq - 1)
        q0 = pl.multiple_of(i * blk, blk)
        k0 = pl.multiple_of((i - first) * blk, blk)
        return q0, k0, lax.rem(n + ring * SB_STAGES, ring)

    def body(n, worst):
        q0, k0, _ = where_is(n - 4)
        accs = []
        for h in range(heads):
            vt = vt_ref[0, h, :, pl.ds(k0, wl)].astype(BF16)
            accs.append(lax.dot_general(w_ref[h], vt, _NT, preferred_element_type=F32))
        o_ref[0, pl.ds(q0, blk), :] = jnp.concatenate(accs, axis=-1).astype(BF16)

        _, _, slot = where_is(n - 3)
        counts = n - 3 >= 1
        for h in range(heads):
            carry = jnp.zeros((blk, 1), F32)
            for b in reversed(range(win)):
                sl = slice(b * blk, (b + 1) * blk)
                suffix = suf_ref[h, b]
                wb = jnp.exp(z_ref[slot, h, :, sl] + suffix + carry)
                if b == win - 1:
                    wb = jnp.where(strict, wb, 0.0)
                w_ref[h, :, sl] = wb.astype(BF16)
                carry = carry + suffix[:, 0:1]
            worst = jnp.maximum(worst, jnp.where(counts, carry, -jnp.inf))

        for h in range(heads):
            suf_ref[h] = jnp.dot(sp_ref[h].reshape(win * blk, blk), uu,
                                 preferred_element_type=F32).reshape(win, blk, blk)

        _, _, slot = where_is(n - 1)
        for h in range(heads):
            sp = _softplus(z_ref[slot, h])
            for b in range(win):
                spb = sp[:, b * blk:(b + 1) * blk]
                if b == win - 1:
                    spb = jnp.where(strict, spb, 0.0)
                sp_ref[h, b] = spb.astype(BF16)

        q0, k0, slot = where_is(n)
        for h in range(heads):
            kt = kt_ref[0, h, :, pl.ds(k0, wl)].astype(BF16)
            z_ref[slot, h] = jnp.dot(q_ref[0, h, pl.ds(q0, blk), :], kt, preferred_element_type=F32)
        return worst

    worst = lax.fori_loop(0, nq - first + SB_STAGES - 1, body, jnp.full((blk, 1), -jnp.inf, F32))
    flag_ref[...] = jnp.broadcast_to(jnp.max(worst), flag_ref.shape)


def _sb_prompt_full_kernel(q_ref, kt_ref, vt_ref, o_ref, *, heads):
    t = q_ref.shape[2]
    blk = SB_BLOCK
    uu = _suffix_neg_ones(blk)
    strict = _strict_mask(blk)

    def q_block(i, _):
        q0 = pl.multiple_of(i * blk, blk)
        qs = [q_ref[0, h, pl.ds(q0, blk), :] for h in range(heads)]

        def strips(k0, carries, accs, masks):
            return _sb_strips(qs, [kt_ref[0, h, :, pl.ds(k0, blk)].astype(BF16) for h in range(heads)],
                              [vt_ref[0, h, :, pl.ds(k0, blk)].astype(BF16) for h in range(heads)],
                              carries, accs, uu, masks)

        state = strips(q0, [jnp.zeros((blk, 1), F32)] * heads, [jnp.zeros((blk, HEAD_DIM), F32)] * heads,
                       [strict])

        def k_block(jj, st):
            carries, accs = strips(pl.multiple_of((i - 1 - jj) * blk, blk), st[0], st[1], [None])
            return tuple(carries), tuple(accs)

        _, accs = lax.fori_loop(0, i, k_block, (tuple(state[0]), tuple(state[1])))
        o_ref[0, pl.ds(q0, blk), :] = jnp.concatenate(list(accs), axis=-1).astype(BF16)
        return 0

    lax.fori_loop(0, t // blk, q_block, 0)


def _sb_prompt(q, kt, vt):
    b, n_heads, t, _ = q.shape
    heads = 4
    wl = SB_WINDOW_BLOCKS * SB_BLOCK
    assert t >= wl
    q_blk = pl.BlockSpec((1, heads, t, HEAD_DIM), lambda bi, hg: (bi, hg, 0, 0))
    kv_blk = pl.BlockSpec((1, heads, HEAD_DIM, t), lambda bi, hg: (bi, hg, 0, 0))
    o_blk = pl.BlockSpec((1, t, heads * HEAD_DIM), lambda bi, hg: (bi, 0, hg))
    o_shape = jax.ShapeDtypeStruct((b, t, n_heads * HEAD_DIM), BF16)
    grid = (b, n_heads // heads)
    o_fast, flags = pl.pallas_call(
        functools.partial(_sb_prompt_fast_kernel, heads=heads),
        grid=grid,
        in_specs=[q_blk, kv_blk, kv_blk],
        out_specs=[o_blk, pl.BlockSpec((1, 1, 8, 128), lambda bi, hg: (bi, hg, 0, 0))],
        out_shape=[o_shape, jax.ShapeDtypeStruct((b, n_heads // heads, 8, 128), F32)],
        scratch_shapes=[pltpu.VMEM((SB_STAGES - 1, heads, SB_BLOCK, wl), F32),
                        pltpu.VMEM((heads, SB_WINDOW_BLOCKS, SB_BLOCK, SB_BLOCK), BF16),
                        pltpu.VMEM((heads, SB_WINDOW_BLOCKS, SB_BLOCK, SB_BLOCK), F32),
                        pltpu.VMEM((heads, SB_BLOCK, wl), BF16)],
        compiler_params=_cparams(("parallel", "parallel")),
        name="sb_prompt_fast",
    )(q, kt, vt)
    if t // SB_BLOCK <= SB_WINDOW_BLOCKS:
        return o_fast

    def full():
        return pl.pallas_call(
            functools.partial(_sb_prompt_full_kernel, heads=heads),
            grid=grid,
            in_specs=[q_blk, kv_blk, kv_blk],
            out_specs=o_blk,
            out_shape=o_shape,
            compiler_params=_cparams(("parallel", "parallel")),
            name="sb_prompt_full",
        )(q, kt, vt)

    return lax.cond(jnp.max(flags) > SB_DEAD, full, lambda: o_fast)


def _sb_sample_kernel(q_ref, ktn_ref, vtn_ref, ktc_ref, vtc_ref, o_ref, *maybe_flag, heads, strip):
    s = q_ref.shape[2]
    width = ktc_ref.shape[4]
    blk = SB_BLOCK
    uu = _suffix_neg_ones(blk)
    uu_new = _suffix_neg_ones(s)
    strict = _strict_mask(s)

    qs = [q_ref[0, h] for h in range(heads)]
    state = _sb_strips(qs, [ktn_ref[0, h].astype(BF16) for h in range(heads)],
                       [vtn_ref[0, h].astype(BF16) for h in range(heads)],
                       [jnp.zeros((s, 1), F32)] * heads, [jnp.zeros((s, HEAD_DIM), F32)] * heads,
                       uu_new, [strict])

    def k_strip(jj, st):
        k0 = pl.multiple_of(width - (jj + 1) * strip * blk, blk)
        carries, accs = _sb_strips(
            qs, [ktc_ref[0, 0, h, :, pl.ds(k0, strip * blk)].astype(BF16) for h in range(heads)],
            [vtc_ref[0, 0, h, :, pl.ds(k0, strip * blk)].astype(BF16) for h in range(heads)],
            st[0], st[1], uu, [None] * strip)
        return tuple(carries), tuple(accs)

    carries, accs = lax.fori_loop(0, width // (strip * blk), k_strip, (tuple(state[0]), tuple(state[1])))
    o_ref[0] = jnp.concatenate(list(accs), axis=-1).astype(BF16)
    if maybe_flag:
        worst = carries[0]
        for h in range(1, heads):
            worst = jnp.maximum(worst, carries[h])
        maybe_flag[0][...] = jnp.broadcast_to(jnp.max(worst), maybe_flag[0].shape)


def _sb_sample(q, kt_new, vt_new, cache_kt, cache_vt, layer):
    b, n_heads, s, _ = q.shape
    past = cache_kt.shape[4]
    width = min(past, SB_SAMPLE_WINDOW)
    assert past % width == 0 and width % SB_BLOCK == 0
    last = past // width - 1
    o_shape = jax.ShapeDtypeStruct((b, s, n_heads * HEAD_DIM), BF16)

    def specs(heads, cache_width, cache_block):
        q_blk = pl.BlockSpec((1, heads, s, HEAD_DIM), lambda bi, hg: (bi, hg, 0, 0))
        new_blk = pl.BlockSpec((1, heads, HEAD_DIM, s), lambda bi, hg: (bi, hg, 0, 0))
        cache_blk = pl.BlockSpec((1, 1, heads, HEAD_DIM, cache_width),
                                 lambda bi, hg: (layer, bi, hg, 0, cache_block))
        o_blk = pl.BlockSpec((1, s, heads * HEAD_DIM), lambda bi, hg: (bi, 0, hg))
        return (b, n_heads // heads), [q_blk, new_blk, new_blk, cache_blk, cache_blk], o_blk

    heads = n_heads
    grid, in_specs, o_blk = specs(heads, width, last)
    o_fast, flags = pl.pallas_call(
        functools.partial(_sb_sample_kernel, heads=heads, strip=width // SB_BLOCK),
        grid=grid,
        in_specs=in_specs,
        out_specs=[o_blk, pl.BlockSpec((1, 1, 8, 128), lambda bi, hg: (bi, hg, 0, 0))],
        out_shape=[o_shape, jax.ShapeDtypeStruct((b, n_heads // heads, 8, 128), F32)],
        compiler_params=_cparams(("parallel", "parallel")),
        name="sb_sample_fast",
    )(q, kt_new, vt_new, cache_kt, cache_vt)
    if width == past:
        return o_fast

    def full():
        heads = 4
        grid, in_specs, o_blk = specs(heads, past, 0)
        return pl.pallas_call(
            functools.partial(_sb_sample_kernel, heads=heads, strip=1),
            grid=grid,
            in_specs=in_specs,
            out_specs=o_blk,
            out_shape=o_shape,
            compiler_params=_cparams(("parallel", "parallel")),
            name="sb_sample_full",
        )(q, kt_new, vt_new, cache_kt, cache_vt)

    return lax.cond(jnp.max(flags) > SB_DEAD, full, lambda: o_fast)


def _route(logits):
    tm = logits.shape[0]
    lane = lax.broadcasted_iota(jnp.int32, (tm, ROUTER_LANES), 1)
    lane_f = lane.astype(F32)
    neg = -jnp.inf
    first = lambda hit: jnp.min(jnp.where(hit, lane_f, float(ROUTER_LANES)), axis=-1, keepdims=True)

    gl = jnp.where(lane < N_GROUPS, logits, neg)
    g_max = jnp.max(gl, axis=-1, keepdims=True)
    g_top = 1.0 / jnp.sum(jnp.exp(gl - g_max), axis=-1, keepdims=True)
    g_idx = first(gl == g_max)

    lo = N_GROUPS + g_idx * EXPERTS_PER_GROUP
    in_group = (lane_f >= lo) & (lane_f < lo + EXPERTS_PER_GROUP)
    sel = jnp.where(in_group, logits, neg)
    t1 = jnp.max(sel, axis=-1, keepdims=True)
    i1 = first(sel == t1)
    sel2 = jnp.where(lane_f == i1, neg, sel)
    t2 = jnp.max(sel2, axis=-1, keepdims=True)
    i2 = first(sel2 == t2)
    e2 = jnp.exp(t2 - t1)
    den = 1.0 + e2
    w1 = (1.0 / den) * g_top
    w2 = (e2 / den) * g_top
    return jnp.where(lane_f == i1, w1, 0.0) + jnp.where(lane_f == i2, w2, 0.0)


CHANNEL_ROW_SPLITS = 2
EXPERTS_PER_TRIP = 8


def _channel_kernel(x_ref, o_ref, p_ref, wo_ref, gffn_ref, wr_ref, wg_ref, wu_ref, wd_ref, gple_ref,
                    wpg_ref, wpp_ref, out_ref, acc_ref):
    tm = x_ref.shape[0]
    d_expert = wd_ref.shape[2]
    half = tm // CHANNEL_ROW_SPLITS
    rows = [slice(i * half, (i + 1) * half) for i in range(CHANNEL_ROW_SPLITS)]

    h1s = [x_ref[r, :] + jnp.dot(o_ref[r, :], wo_ref[...], preferred_element_type=F32) for r in rows]
    xns = [_rms_unit(h1) * gffn_ref[...] for h1 in h1s]
    xnbs = [xn.astype(BF16) for xn in xns]
    xlos = [(xn - xnb.astype(F32)).astype(BF16) for xn, xnb in zip(xns, xnbs)]
    logits = [jnp.dot(jnp.concatenate([xnb, xlo, xnb], axis=-1), wr_ref[...], preferred_element_type=F32)
              for xnb, xlo in zip(xnbs, xlos)]
    gates = jnp.concatenate([_route(l) for l in logits], axis=0)
    xnb = jnp.concatenate(xnbs, axis=0)
    lane = lax.broadcasted_iota(jnp.int32, gates.shape, 1)
    for r, h1 in zip(rows, h1s):
        acc_ref[r, :] = h1

    def experts(j, _):
        es = [j * EXPERTS_PER_TRIP + k for k in range(EXPERTS_PER_TRIP)]
        gs = [jnp.dot(xnb, wg_ref[0, e], preferred_element_type=F32) for e in es]
        us = [jnp.dot(xnb, wu_ref[0, e], preferred_element_type=F32) for e in es]
        hids = []
        for e, g, u in zip(es, gs, us):
            gate_e = jnp.sum(jnp.where(lane == N_GROUPS + e, gates, 0.0), axis=-1, keepdims=True)
            hids.append(((g * jax.nn.sigmoid(g)) * u * gate_e).astype(BF16))
        wd = wd_ref[0, pl.ds(j * EXPERTS_PER_TRIP, EXPERTS_PER_TRIP)]
        acc_ref[...] += jnp.dot(jnp.concatenate(hids, axis=-1),
                                wd.reshape(EXPERTS_PER_TRIP * d_expert, wd.shape[-1]),
                                preferred_element_type=F32)
        return 0

    lax.fori_loop(0, N_EXPERTS // EXPERTS_PER_TRIP, experts, 0)
    h2s = [acc_ref[r, :] for r in rows]
    x3s = [(_rms_unit(h2) * gple_ref[...]).astype(BF16) for h2 in h2s]
    ple_gates = [jax.nn.sigmoid(jnp.dot(x3, wpg_ref[0], preferred_element_type=F32)) for x3 in x3s]
    projs = [jnp.dot(p_ref[0, r, :].astype(BF16), wpp_ref[0], preferred_element_type=F32) for r in rows]
    for r, h2, proj, gate in zip(rows, h2s, projs, ple_gates):
        out_ref[r, :] = h2 + proj * gate


def _channel(x, o, p, layer, w, stacks):
    n, d = x.shape
    tm = min(n, 512)
    row = lambda cols: pl.BlockSpec((tm, cols), lambda i: (i, 0))
    small = [w["wo"], w["gffn"], w["wr"]]
    moe = [stacks["wg"], stacks["wu"], stacks["wd"]]
    ple = [stacks["wpg"], stacks["wpp"]]
    return pl.pallas_call(
        _channel_kernel,
        grid=(n // tm,),
        in_specs=([row(d), row(d), pl.BlockSpec((1, tm, p.shape[2]), lambda i: (layer, i, 0))]
                  + [_resident(a.shape) for a in small] + [_resident_layer(a.shape, layer) for a in moe]
                  + [_resident(w["gple"].shape)] + [_resident_layer(a.shape, layer) for a in ple]),
        out_specs=row(d),
        out_shape=jax.ShapeDtypeStruct((n, d), F32),
        scratch_shapes=[pltpu.VMEM((tm, d), F32)],
        compiler_params=_cparams(("parallel",)),
        name="channel",
    )(x, o, p, *small, *moe, w["gple"], *ple)


PROJ_B_ROW_SPLITS = 2


def _proj_b_kernel(h_ref, ga_ref, wq_ref, gq_ref, gkv_ref, wkv_ref, gk_ref, gsum_ref, gexp_ref,
                   q_ref, k_ref, v_ref, *, n_heads, n_kv):
    tm = h_ref.shape[1]
    splits = PROJ_B_ROW_SPLITS if tm % (16 * PROJ_B_ROW_SPLITS) == 0 else 1
    rows = [slice(i * tm // splits, (i + 1) * tm // splits) for i in range(splits)]
    ys = [_rms_unit(h_ref[0, r, :]) for r in rows]
    qs = [jnp.dot((y * ga_ref[...]).astype(BF16), wq_ref[...], preferred_element_type=F32) for y in ys]
    kvs = [jnp.dot((y * gkv_ref[...]).astype(BF16), wkv_ref[...], preferred_element_type=F32) for y in ys]
    mss = [jnp.dot((q * q).astype(BF16), gsum_ref[...], preferred_element_type=F32) * (1.0 / HEAD_DIM)
           for q in qs]
    inv_fulls = [jnp.dot(_split_hl(lax.rsqrt(ms + EPS)), gexp_ref[...], preferred_element_type=F32)
                 for ms in mss]
    for r, q, inv_full in zip(rows, qs, inv_fulls):
        qn = (q * inv_full * gq_ref[...]) * (HEAD_DIM ** -0.5)
        for h in range(n_heads):
            q_ref[0, h, r, :] = qn[:, h * HEAD_DIM:(h + 1) * HEAD_DIM].astype(BF16)
    for r, kv in zip(rows, kvs):
        for h in range(n_kv):
            kh = kv[:, h * HEAD_DIM:(h + 1) * HEAD_DIM]
            k_ref[0, h, r, :] = _rms_unit(kh) * gk_ref[...]
            v_ref[0, h, r, :] = kv[:, (n_kv + h) * HEAD_DIM:(n_kv + h + 1) * HEAD_DIM]


def _proj_b(h, ga, wq, gq, gkv, wkv, gk, gsum, gexp):
    bx, t, d = h.shape
    n_heads = wq.shape[1] // HEAD_DIM
    n_kv = wkv.shape[1] // (2 * HEAD_DIM)
    tm = min(t, 512)
    hm = lambda b, i: (b, 0, i, 0)
    ins = [ga, wq, gq, gkv, wkv, gk, gsum, gexp]
    return pl.pallas_call(
        functools.partial(_proj_b_kernel, n_heads=n_heads, n_kv=n_kv),
        grid=(bx, t // tm),
        in_specs=[pl.BlockSpec((1, tm, d), lambda b, i: (b, i, 0))] + [_resident(a.shape) for a in ins],
        out_specs=[pl.BlockSpec((1, n_heads, tm, HEAD_DIM), hm),
                   pl.BlockSpec((1, n_kv, tm, HEAD_DIM), hm),
                   pl.BlockSpec((1, n_kv, tm, HEAD_DIM), hm)],
        out_shape=[jax.ShapeDtypeStruct((bx, n_heads, t, HEAD_DIM), BF16),
                   jax.ShapeDtypeStruct((bx, n_kv, t, HEAD_DIM), F32),
                   jax.ShapeDtypeStruct((bx, n_kv, t, HEAD_DIM), F32)],
        compiler_params=_cparams(("parallel", "parallel")),
        name="proj_b",
    )(h, *ins)


def _t5_bucket(rel):
    nb = NUM_BUCKETS // 2
    max_exact = nb // 2
    n = jnp.abs(rel)
    large = max_exact + (jnp.log(jnp.maximum(n, 1).astype(jnp.float32) / max_exact)
                         / math.log(MAX_DISTANCE / max_exact) * (nb - max_exact)).astype(jnp.int32)
    large = jnp.minimum(large, nb - 1)
    return jnp.where(rel > 0, nb, 0) + jnp.where(n < max_exact, n, large)


def _bias_kernel(bucket_ref, rbt_ref, out_ref):
    bucket = bucket_ref[...].astype(F32)
    guard = 2.0 ** -10
    acc = jnp.zeros(out_ref.shape, F32)
    for b in range(NUM_BUCKETS):
        hit = (bucket >= b - guard) & (bucket < b + 1 - guard)
        acc = acc + jnp.where(hit, rbt_ref[:, b:b + 1], 0.0)
    out_ref[...] = acc


def _bias_table(rel_bias, nq, nk, key_offset):
    rel = (jnp.arange(nk, dtype=jnp.int32)[None, :] - key_offset) - jnp.arange(nq, dtype=jnp.int32)[:, None]
    bucket = _t5_bucket(rel).reshape(1, nq * nk)
    n_heads = rel_bias.shape[1]
    out = pl.pallas_call(
        _bias_kernel,
        out_shape=jax.ShapeDtypeStruct((n_heads, nq * nk), F32),
        name="bias_table",
    )(bucket, rel_bias.T)
    return out.reshape(n_heads, nq, nk)


SWA_CHUNKS_PER_TRIP = 2


def _swa_kernel(sink_ref, q_ref, k_ref, v_ref, bias_ref, o_ref, kpad_ref, vpad_ref, *, cq, wl, pad, n_chunks,
                group):
    n_kv, tk = k_ref.shape[1], k_ref.shape[2]
    if pad:
        kpad_ref[:, 0:pad, :] = jnp.zeros((n_kv, pad, HEAD_DIM), BF16)
        vpad_ref[:, 0:pad, :] = jnp.zeros((n_kv, pad, HEAD_DIM), BF16)
    kpad_ref[:, pad:pad + tk, :] = k_ref[0].astype(BF16)
    vpad_ref[:, pad:pad + tk, :] = v_ref[0].astype(BF16)

    per_trip = SWA_CHUNKS_PER_TRIP if n_chunks % SWA_CHUNKS_PER_TRIP == 0 else 1
    n_heads = n_kv * group

    def chunks(c, _):
        r0s = [pl.multiple_of((c * per_trip + i) * cq, cq) for i in range(per_trip)]
        logits = [[lax.dot_general(
            q_ref[0, kv * group:(kv + 1) * group, pl.ds(r0, cq), :].reshape(group * cq, HEAD_DIM),
            kpad_ref[kv, pl.ds(r0, wl), :], _NT, preferred_element_type=F32)
            for kv in range(n_kv)] for r0 in r0s]
        es, dens = [], []
        for r0, lg in zip(r0s, logits):
            valid = lax.broadcasted_iota(jnp.int32, (cq, wl), 1) + r0 >= pad
            for h in range(n_heads):
                l = lg[h // group][(h % group) * cq:(h % group + 1) * cq] + bias_ref[h]
                if pad:
                    l = jnp.where(valid, l, -jnp.inf)
                sink = sink_ref[h]
                m = jnp.maximum(jnp.max(l, axis=-1, keepdims=True), sink)
                e = jnp.exp(l - m)
                dens.append(jnp.sum(e, axis=-1, keepdims=True) + jnp.exp(sink - m))
                es.append(e.astype(BF16))
        pvs = [[jnp.dot(jnp.concatenate(es[i * n_heads + kv * group:i * n_heads + (kv + 1) * group], axis=0),
                        vpad_ref[kv, pl.ds(r0, wl), :], preferred_element_type=F32)
                for kv in range(n_kv)] for i, r0 in enumerate(r0s)]
        for i, r0 in enumerate(r0s):
            outs = [pvs[i][h // group][(h % group) * cq:(h % group + 1) * cq] / dens[i * n_heads + h]
                    for h in range(n_heads)]
            o_ref[0, pl.ds(r0, cq), :] = jnp.concatenate(outs, axis=-1).astype(BF16)
        return 0

    lax.fori_loop(0, n_chunks // per_trip, chunks, 0)


def _swa(q, k_win, v_win, bias, sinks, *, cq, wl, pad):
    b, n_heads, tq, _ = q.shape
    n_kv, tk = k_win.shape[1], k_win.shape[2]
    n_chunks = tq // cq
    assert (n_chunks - 1) * cq + wl == pad + tk
    kv_blk = pl.BlockSpec((1, n_kv, tk, HEAD_DIM), lambda bi: (bi, 0, 0, 0))
    return pl.pallas_call(
        functools.partial(_swa_kernel, cq=cq, wl=wl, pad=pad, n_chunks=n_chunks, group=n_heads // n_kv),
        grid=(b,),
        in_specs=[pl.BlockSpec(memory_space=pltpu.SMEM),
                  pl.BlockSpec((1, n_heads, tq, HEAD_DIM), lambda bi: (bi, 0, 0, 0)),
                  kv_blk, kv_blk, _resident(bias.shape)],
        out_specs=pl.BlockSpec((1, tq, n_heads * HEAD_DIM), lambda bi: (bi, 0, 0)),
        out_shape=jax.ShapeDtypeStruct((b, tq, n_heads * HEAD_DIM), BF16),
        scratch_shapes=[pltpu.VMEM((n_kv, pad + tk, HEAD_DIM), BF16),
                        pltpu.VMEM((n_kv, pad + tk, HEAD_DIM), BF16)],
        compiler_params=_cparams(("parallel",)),
        name="swa",
    )(sinks, q, k_win, v_win, bias)


def _row(v):
    return v.reshape(1, -1).astype(F32)


def _router_hi_hi_lo(wr):
    hi = wr.astype(BF16)
    lo = (wr - hi.astype(F32)).astype(BF16)
    return jnp.concatenate([hi, hi, lo], axis=0)


def _prep_weights(prm):
    d = prm["a_w_o"].shape[1]
    depth = prm["norm_ffn"].shape[0]
    n_a = prm["a_w_qkv"].shape[0]
    scale = HEAD_DIM ** -0.5
    w = {"channel": [], "n_a": n_a, "depth": depth}
    for i in range(depth):
        wo = prm["a_w_o"][i] if i < n_a else prm["b_w_o"][i - n_a]
        pad = ROUTER_LANES - N_GROUPS - N_EXPERTS
        wr = jnp.concatenate([prm["moe_w_group"][i], prm["moe_w_router"][i].reshape(d, N_EXPERTS),
                              jnp.zeros((d, pad), F32)], axis=1)
        w["channel"].append({"wo": wo.astype(BF16), "gffn": _row(prm["norm_ffn"][i]),
                             "wr": _router_hi_hi_lo(wr), "gple": _row(prm["norm_ple"][i])})
    w["stacks"] = {"wg": prm["moe_w_gate"].astype(BF16), "wu": prm["moe_w_up"].astype(BF16),
                   "wd": prm["moe_w_down"].astype(BF16), "wpg": prm["ple_w_gate"].astype(BF16),
                   "wpp": prm["ple_w_proj"].astype(BF16)}
    w["qkv"] = []
    for i in range(n_a):
        wq = prm["a_w_qkv"][i]
        hd = wq.shape[1] // 3
        w["qkv"].append(((wq[:, :hd] * scale).astype(BF16),
                         wq[:, hd:].T.astype(BF16)))
    n_heads = prm["b_w_q"].shape[2] // HEAD_DIM
    head_of = jnp.arange(n_heads * HEAD_DIM, dtype=jnp.int32) // HEAD_DIM
    lanes = jnp.arange(ROUTER_LANES, dtype=jnp.int32)
    member = (head_of[:, None] == lanes[None, :]).astype(BF16)
    w["gsum"] = member
    w["gexp"] = jnp.concatenate([member.T, member.T], axis=0)
    w["wq_b"] = [prm["b_w_q"][j].astype(BF16) for j in range(depth - n_a)]
    w["gq_b"] = [_row(jnp.tile(prm["b_q_norm"][j], n_heads)) for j in range(depth - n_a)]
    w["wkv"] = prm["b_w_kv"].astype(BF16)
    return w


def _assert_sample_window_visible(past_len, s, tk):
    q_chunk = [(past_len + i) // CHUNK for i in range(s)]
    k_pos = [past_len + s - tk + j for j in range(tk)]
    ok = all(kp >= 0 and qc - WIN_CHUNKS <= kp // CHUNK <= qc for qc in q_chunk for kp in k_pos)
    if not ok:
        raise NotImplementedError("sample window with masked keys")


def _run_trunk(x, p, prm, w, sb_cache_k=None, sb_cache_v=None, swa_cache_k=None, swa_cache_v=None):
    bx, t, d = x.shape
    n_a, depth = w["n_a"], w["depth"]
    sample = sb_cache_k is not None
    h = x
    sb_k, sb_v = [], []
    k_win = v_win = None
    q_b = None
    for i in range(depth):
        if i < n_a:
            q, kt, vt = _proj_a(h, _row(prm["norm_attn"][i]), *w["qkv"][i])
            sb_k.append(kt)
            sb_v.append(vt)
            if sample:
                o = _sb_sample(q, kt, vt, jnp.swapaxes(sb_cache_k, -1, -2), jnp.swapaxes(sb_cache_v, -1, -2), i)
            else:
                o = _sb_prompt(q, kt, vt)
        else:
            j = i - n_a
            if j > 0:
                raise NotImplementedError("one B layer supported")
            if sample:
                tk = k_win.shape[2]
                _assert_sample_window_visible(sb_cache_k.shape[3], t, tk)
                o = _swa(q_b, k_win, v_win, _bias_table(prm["rel_bias"], t, tk, tk - t),
                         prm["b_sinks"][j], cq=t, wl=tk, pad=0)
            else:
                wl = WINDOW + CHUNK
                o = _swa(q_b, k_win, v_win, _bias_table(prm["rel_bias"], CHUNK, wl, WINDOW),
                         prm["b_sinks"][j], cq=CHUNK, wl=wl, pad=WINDOW)
        h = _channel(h.reshape(bx * t, d), o.reshape(bx * t, d), p.reshape(depth, bx * t, -1), i,
                     w["channel"][i], w["stacks"]).reshape(bx, t, d)
        if i == n_a - 1:
            q_b, k_s, v_s = _proj_b(h, _row(prm["norm_attn"][n_a]), w["wq_b"][0], w["gq_b"][0],
                                    _row(prm["kv_norm"]), w["wkv"], _row(prm["b_k_norm"]),
                                    w["gsum"], w["gexp"])
            if sample:
                k_win = jnp.concatenate([swa_cache_k, k_s], axis=2)
                v_win = jnp.concatenate([swa_cache_v, v_s], axis=2)
            else:
                k_win, v_win = k_s, v_s
    sb_k = jnp.swapaxes(jnp.stack(sb_k), -1, -2)
    sb_v = jnp.swapaxes(jnp.stack(sb_v), -1, -2)
    return h, sb_k, sb_v, k_win[:, :, -WINDOW:], v_win[:, :, -WINDOW:]


def kernel(x_prompt, x_sample, p_prompt, p_sample, cache_sb_k, cache_sb_v, cache_swa_k, cache_swa_v, norm_attn, norm_ffn, norm_ple, a_w_qkv, a_w_o, kv_norm, b_w_kv, b_k_norm, b_w_q, b_q_norm, b_sinks, b_w_o, rel_bias, moe_w_group, moe_w_router, moe_w_gate, moe_w_up, moe_w_down, ple_w_proj, ple_w_gate):
    prm = {
        "norm_attn": norm_attn, "norm_ffn": norm_ffn, "norm_ple": norm_ple,
        "a_w_qkv": a_w_qkv, "a_w_o": a_w_o, "kv_norm": kv_norm, "b_w_kv": b_w_kv, "b_k_norm": b_k_norm,
        "b_w_q": b_w_q, "b_q_norm": b_q_norm, "b_sinks": b_sinks, "b_w_o": b_w_o, "rel_bias": rel_bias,
        "moe_w_group": moe_w_group, "moe_w_router": moe_w_router, "moe_w_gate": moe_w_gate,
        "moe_w_up": moe_w_up, "moe_w_down": moe_w_down, "ple_w_proj": ple_w_proj, "ple_w_gate": ple_w_gate,
    }
    w = _prep_weights(prm)
    y_p, sb_k_p, sb_v_p, swa_k_p, swa_v_p = _run_trunk(x_prompt, p_prompt, prm, w)
    y_s, sb_k_s, sb_v_s, swa_k_s, swa_v_s = _run_trunk(x_sample, p_sample, prm, w, cache_sb_k, cache_sb_v,
                                                       cache_swa_k, cache_swa_v)
    return (y_p, y_s, sb_k_p, sb_v_p, swa_k_p, swa_v_p, sb_k_s, sb_v_s, swa_k_s, swa_v_s)
```

```python
import functools
import math

import jax
import jax.numpy as jnp
from jax import lax
from jax.experimental import pallas as pl
from jax.experimental.pallas import tpu as pltpu

F32 = jnp.float32
BF16 = jnp.bfloat16

HEAD_DIM = 64
CHUNK = 64
WINDOW = 128
WIN_CHUNKS = WINDOW // CHUNK
NUM_BUCKETS = 32
MAX_DISTANCE = 128
N_GROUPS = 4
EXPERTS_PER_GROUP = 4
N_EXPERTS = N_GROUPS * EXPERTS_PER_GROUP
EPS = 1e-6
LOG2E = 1.4426950408889634
SB_BLOCK = 128
ROUTER_LANES = 128
VMEM_LIMIT = 56 * 1024 * 1024

_NT = (((1,), (1,)), ((), ()))


def _cparams(sem):
    return pltpu.CompilerParams(dimension_semantics=sem, vmem_limit_bytes=VMEM_LIMIT)


def _rms_unit(x):
    return x * lax.rsqrt(jnp.mean(x * x, axis=-1, keepdims=True) + EPS)


def _split_hl(a):
    hi = a.astype(BF16)
    lo = (a - hi.astype(F32)).astype(BF16)
    return jnp.concatenate([hi, lo], axis=-1)


def _resident(shape):
    nd = len(shape)
    return pl.BlockSpec(shape, lambda *_: (0,) * nd, pipeline_mode=pl.Buffered(1))


def _resident_layer(shape, layer):
    nd = len(shape)
    return pl.BlockSpec((1,) + tuple(shape[1:]), lambda *_: (layer,) + (0,) * (nd - 1),
                        pipeline_mode=pl.Buffered(1))


def _proj_a_kernel(x_ref, g_ref, wq_ref, wkvt_ref, q_ref, kt_ref, vt_ref, *, n_heads):
    xn = (_rms_unit(x_ref[0]) * g_ref[...]).astype(BF16)
    heads_per_dot = 4
    width = heads_per_dot * HEAD_DIM
    for c in range(n_heads // heads_per_dot):
        r = jnp.dot(xn, wq_ref[:, c * width:(c + 1) * width], preferred_element_type=F32)
        for hh in range(heads_per_dot):
            q_ref[0, c * heads_per_dot + hh] = r[:, hh * HEAD_DIM:(hh + 1) * HEAD_DIM].astype(BF16)
    for c in range(2 * n_heads // heads_per_dot):
        r = lax.dot_general(wkvt_ref[c * width:(c + 1) * width, :], xn, _NT, preferred_element_type=F32)
        for hh in range(heads_per_dot):
            h = c * heads_per_dot + hh
            piece = r[hh * HEAD_DIM:(hh + 1) * HEAD_DIM, :]
            if h < n_heads:
                kt_ref[0, h] = piece
            else:
                vt_ref[0, h - n_heads] = piece


def _proj_a(x, g, wq, wkvt):
    bx, t, d = x.shape
    n_heads = wq.shape[1] // HEAD_DIM
    tm = min(t, 256)
    return pl.pallas_call(
        functools.partial(_proj_a_kernel, n_heads=n_heads),
        grid=(bx, t // tm),
        in_specs=[pl.BlockSpec((1, tm, d), lambda b, i: (b, i, 0)),
                  _resident((1, d)), _resident(wq.shape), _resident(wkvt.shape)],
        out_specs=[pl.BlockSpec((1, n_heads, tm, HEAD_DIM), lambda b, i: (b, 0, i, 0)),
                   pl.BlockSpec((1, n_heads, HEAD_DIM, tm), lambda b, i: (b, 0, 0, i)),
                   pl.BlockSpec((1, n_heads, HEAD_DIM, tm), lambda b, i: (b, 0, 0, i))],
        out_shape=[jax.ShapeDtypeStruct((bx, n_heads, t, HEAD_DIM), BF16),
                   jax.ShapeDtypeStruct((bx, n_heads, HEAD_DIM, t), F32),
                   jax.ShapeDtypeStruct((bx, n_heads, HEAD_DIM, t), F32)],
        compiler_params=_cparams(("parallel", "parallel")),
        name="proj_a",
    )(x, g, wq, wkvt)


SB_DEAD = -104.0
SB_WINDOW_BLOCKS = 3
SB_SAMPLE_WINDOW = 256


def _suffix_neg_ones(n):
    r = lax.broadcasted_iota(jnp.int32, (n, n), 0)
    c = lax.broadcasted_iota(jnp.int32, (n, n), 1)
    return jnp.where(r >= c, -1.0, 0.0).astype(BF16)


def _strict_mask(n):
    row = lax.broadcasted_iota(jnp.int32, (n, n), 0)
    col = lax.broadcasted_iota(jnp.int32, (n, n), 1)
    return col < row


def _softplus(z):
    return jnp.maximum(z, 0.0) + jnp.log(1.0 + jnp.exp2(jnp.abs(z) * -LOG2E))


def _sb_strips(qs, kts, vts, carries, accs, uu, masks):
    n = len(qs)
    blk = uu.shape[1]
    nb = kts[0].shape[1] // blk
    zs = [jnp.dot(qs[i], kts[i], preferred_element_type=F32) for i in range(n)]
    sps = [_softplus(z) for z in zs]
    ws = [[None] * nb for _ in range(n)]
    carries = list(carries)
    for b in reversed(range(nb)):
        sl = slice(b * blk, (b + 1) * blk)
        spbs = [sp[:, sl] if masks[b] is None else jnp.where(masks[b], sp[:, sl], 0.0) for sp in sps]
        sufs = [jnp.dot(spb.astype(BF16), uu, preferred_element_type=F32) for spb in spbs]
        for i in range(n):
            wb = jnp.exp(zs[i][:, sl] + sufs[i] + carries[i])
            if masks[b] is not None:
                wb = jnp.where(masks[b], wb, 0.0)
            ws[i][b] = wb.astype(BF16)
            carries[i] = carries[i] + sufs[i][:, 0:1]
    wcat = [w[0] if nb == 1 else jnp.concatenate(w, axis=-1) for w in ws]
    accs = [accs[i] + lax.dot_general(wcat[i], vts[i], _NT, preferred_element_type=F32) for i in range(n)]
    return carries, accs


SB_STAGES = 5
SB_OLD_ROWS = 64


def _sb_prompt_fast_kernel(q_ref, kt_ref, vt_ref, o_ref, flag_ref, z_ref, sp_ref, suf_ref, w_ref, *, heads, groups):
    t = q_ref.shape[2]
    blk = SB_BLOCK
    old = SB_OLD_ROWS
    nq = t // blk
    win = SB_WINDOW_BLOCKS
    wl = win * blk
    ring = z_ref.shape[0]
    first = win - 1
    per_group = nq - first
    units = groups * per_group
    width = heads * HEAD_DIM
    uu = _suffix_neg_ones(blk)
    strict = _strict_mask(blk)
    z_ref[...] = jnp.zeros(z_ref.shape, F32)
    sp_ref[...] = jnp.zeros(sp_ref.shape, BF16)
    suf_ref[...] = jnp.zeros(suf_ref.shape, F32)
    w_ref[...] = jnp.zeros(w_ref.shape, BF16)

    for g in range(groups):
        hs = range(g * heads, (g + 1) * heads)
        for i in range(first):
            _, accs = _sb_strips([q_ref[0, h, i * blk:(i + 1) * blk, :] for h in hs],
                                 [kt_ref[0, h, :, 0:(i + 1) * blk].astype(BF16) for h in hs],
                                 [vt_ref[0, h, :, 0:(i + 1) * blk].astype(BF16) for h in hs],
                                 [jnp.zeros((blk, 1), F32)] * heads, [jnp.zeros((blk, HEAD_DIM), F32)] * heads,
                                 uu, [None] * i + [strict])
            o_ref[0, i * blk:(i + 1) * blk, g * width:(g + 1) * width] = (
                jnp.concatenate(accs, axis=-1).astype(BF16))

    def where_is(n):
        u = jnp.clip(n, 0, units - 1)
        g = u // per_group
        i = first + u - g * per_group
        q0 = pl.multiple_of(i * blk, blk)
        k0 = pl.multiple_of((i - first) * blk, blk)
        return g * heads, q0, k0, lax.rem(n + ring * SB_STAGES, ring)

    def body(n, worst):
        h0, q0, k0, _ = where_is(n - 4)
        accs = []
        for h in range(heads):
            vt = vt_ref[0, h0 + h, :, pl.ds(k0, wl)].astype(BF16)
            accs.append(lax.dot_general(w_ref[h], vt, _NT, preferred_element_type=F32))
        o_ref[0, pl.ds(q0, blk), pl.ds(pl.multiple_of(h0 * HEAD_DIM, width), width)] = (
            jnp.concatenate(accs, axis=-1).astype(BF16))

        _, _, _, slot = where_is(n - 3)
        counts = n - 3 >= 0
        for h in range(heads):
            s_old, s_prev, s_own = suf_ref[h, 0:old], suf_ref[h, old:old + blk], suf_ref[h, old + blk:]
            w_own = jnp.where(strict, jnp.exp(z_ref[slot, h, :, 2 * blk:] + s_own), 0.0)
            carry = s_own[:, 0:1]
            w_prev = jnp.exp(z_ref[slot, h, :, blk:2 * blk] + s_prev + carry)
            carry = carry + s_prev[:, 0:1]
            w_old = jnp.exp(z_ref[slot, h, 0:old, 0:blk] + s_old + carry[0:old])
            w_ref[h, :, 2 * blk:] = w_own.astype(BF16)
            w_ref[h, :, blk:2 * blk] = w_prev.astype(BF16)
            w_ref[h, 0:old, 0:blk] = w_old.astype(BF16)
            edge = jnp.concatenate([carry[0:old] + s_old[:, 0:1], carry[old:]], axis=0)
            worst = jnp.maximum(worst, jnp.where(counts, edge, -jnp.inf))

        for h in range(heads):
            suf_ref[h] = jnp.dot(sp_ref[h], uu, preferred_element_type=F32)

        _, _, _, slot = where_is(n - 1)
        for h in range(heads):
            sp_new = _softplus(z_ref[slot, h, :, blk:])
            sp_ref[h, 0:old] = _softplus(z_ref[slot, h, 0:old, 0:blk]).astype(BF16)
            sp_ref[h, old:old + blk] = sp_new[:, 0:blk].astype(BF16)
            sp_ref[h, old + blk:] = jnp.where(strict, sp_new[:, blk:], 0.0).astype(BF16)

        h0, q0, k0, slot = where_is(n)
        for h in range(heads):
            q = q_ref[0, h0 + h, pl.ds(q0, blk), :]
            kt = kt_ref[0, h0 + h, :, pl.ds(k0, wl)].astype(BF16)
            z_ref[slot, h, :, blk:] = jnp.dot(q, kt[:, blk:], preferred_element_type=F32)
            z_ref[slot, h, 0:old, 0:blk] = jnp.dot(q[0:old], kt[:, 0:blk], preferred_element_type=F32)
        return worst

    worst = lax.fori_loop(0, units + SB_STAGES - 1, body, jnp.full((blk, 1), -jnp.inf, F32))
    flag_ref[...] = jnp.broadcast_to(jnp.max(worst), flag_ref.shape)


def _sb_prompt_full_kernel(q_ref, kt_ref, vt_ref, o_ref, *, heads):
    t = q_ref.shape[2]
    blk = SB_BLOCK
    uu = _suffix_neg_ones(blk)
    strict = _strict_mask(blk)

    def q_block(i, _):
        q0 = pl.multiple_of(i * blk, blk)
        qs = [q_ref[0, h, pl.ds(q0, blk), :] for h in range(heads)]

        def strips(k0, carries, accs, masks):
            return _sb_strips(qs, [kt_ref[0, h, :, pl.ds(k0, blk)].astype(BF16) for h in range(heads)],
                              [vt_ref[0, h, :, pl.ds(k0, blk)].astype(BF16) for h in range(heads)],
                              carries, accs, uu, masks)

        state = strips(q0, [jnp.zeros((blk, 1), F32)] * heads, [jnp.zeros((blk, HEAD_DIM), F32)] * heads,
                       [strict])

        def k_block(jj, st):
            carries, accs = strips(pl.multiple_of((i - 1 - jj) * blk, blk), st[0], st[1], [None])
            return tuple(carries), tuple(accs)

        _, accs = lax.fori_loop(0, i, k_block, (tuple(state[0]), tuple(state[1])))
        o_ref[0, pl.ds(q0, blk), :] = jnp.concatenate(list(accs), axis=-1).astype(BF16)
        return 0

    lax.fori_loop(0, t // blk, q_block, 0)


def _sb_prompt(q, kt, vt):
    b, n_heads, t, _ = q.shape
    wl = SB_WINDOW_BLOCKS * SB_BLOCK
    assert t >= wl
    o_shape = jax.ShapeDtypeStruct((b, t, n_heads * HEAD_DIM), BF16)

    def specs(hb):
        q_blk = pl.BlockSpec((1, hb, t, HEAD_DIM), lambda bi, hg: (bi, hg, 0, 0))
        kv_blk = pl.BlockSpec((1, hb, HEAD_DIM, t), lambda bi, hg: (bi, hg, 0, 0))
        o_blk = pl.BlockSpec((1, t, hb * HEAD_DIM), lambda bi, hg: (bi, 0, hg))
        return (b, n_heads // hb), [q_blk, kv_blk, kv_blk], o_blk

    heads, groups = 4, 2
    rows = SB_OLD_ROWS + 2 * SB_BLOCK
    grid, in_specs, o_blk = specs(heads * groups)
    o_fast, flags = pl.pallas_call(
        functools.partial(_sb_prompt_fast_kernel, heads=heads, groups=groups),
        grid=grid,
        in_specs=in_specs,
        out_specs=[o_blk, pl.BlockSpec((1, 1, 8, 128), lambda bi, hg: (bi, hg, 0, 0))],
        out_shape=[o_shape, jax.ShapeDtypeStruct((b, grid[1], 8, 128), F32)],
        scratch_shapes=[pltpu.VMEM((SB_STAGES - 1, heads, SB_BLOCK, wl), F32),
                        pltpu.VMEM((heads, rows, SB_BLOCK), BF16),
                        pltpu.VMEM((heads, rows, SB_BLOCK), F32),
                        pltpu.VMEM((heads, SB_BLOCK, wl), BF16)],
        compiler_params=_cparams(("parallel", "parallel")),
        name="sb_prompt_fast",
    )(q, kt, vt)

    def full():
        grid, in_specs, o_blk = specs(4)
        return pl.pallas_call(
            functools.partial(_sb_prompt_full_kernel, heads=4),
            grid=grid,
            in_specs=in_specs,
            out_specs=o_blk,
            out_shape=o_shape,
            compiler_params=_cparams(("parallel", "parallel")),
            name="sb_prompt_full",
        )(q, kt, vt)

    return lax.cond(jnp.max(flags) > SB_DEAD, full, lambda: o_fast)


def _sb_sample_kernel(q_ref, ktn_ref, vtn_ref, ktc_ref, vtc_ref, o_ref, *maybe_flag, heads, strip):
    s = q_ref.shape[2]
    width = ktc_ref.shape[4]
    blk = SB_BLOCK
    uu = _suffix_neg_ones(blk)
    uu_new = _suffix_neg_ones(s)
    strict = _strict_mask(s)

    qs = [q_ref[0, h] for h in range(heads)]
    state = _sb_strips(qs, [ktn_ref[0, h].astype(BF16) for h in range(heads)],
                       [vtn_ref[0, h].astype(BF16) for h in range(heads)],
                       [jnp.zeros((s, 1), F32)] * heads, [jnp.zeros((s, HEAD_DIM), F32)] * heads,
                       uu_new, [strict])

    def k_strip(jj, st):
        k0 = pl.multiple_of(width - (jj + 1) * strip * blk, blk)
        carries, accs = _sb_strips(
            qs, [ktc_ref[0, 0, h, :, pl.ds(k0, strip * blk)].astype(BF16) for h in range(heads)],
            [vtc_ref[0, 0, h, :, pl.ds(k0, strip * blk)].astype(BF16) for h in range(heads)],
            st[0], st[1], uu, [None] * strip)
        return tuple(carries), tuple(accs)

    carries, accs = lax.fori_loop(0, width // (strip * blk), k_strip, (tuple(state[0]), tuple(state[1])))
    o_ref[0] = jnp.concatenate(list(accs), axis=-1).astype(BF16)
    if maybe_flag:
        worst = carries[0]
        for h in range(1, heads):
            worst = jnp.maximum(worst, carries[h])
        maybe_flag[0][...] = jnp.broadcast_to(jnp.max(worst), maybe_flag[0].shape)


def _sb_sample(q, kt_new, vt_new, cache_kt, cache_vt, layer):
    b, n_heads, s, _ = q.shape
    past = cache_kt.shape[4]
    width = min(past, SB_SAMPLE_WINDOW)
    assert past % width == 0 and width % SB_BLOCK == 0
    last = past // width - 1
    o_shape = jax.ShapeDtypeStruct((b, s, n_heads * HEAD_DIM), BF16)

    def specs(heads, cache_width, cache_block):
        q_blk = pl.BlockSpec((1, heads, s, HEAD_DIM), lambda bi, hg: (bi, hg, 0, 0))
        new_blk = pl.BlockSpec((1, heads, HEAD_DIM, s), lambda bi, hg: (bi, hg, 0, 0))
        cache_blk = pl.BlockSpec((1, 1, heads, HEAD_DIM, cache_width),
                                 lambda bi, hg: (layer, bi, hg, 0, cache_block))
        o_blk = pl.BlockSpec((1, s, heads * HEAD_DIM), lambda bi, hg: (bi, 0, hg))
        return (b, n_heads // heads), [q_blk, new_blk, new_blk, cache_blk, cache_blk], o_blk

    heads = n_heads
    grid, in_specs, o_blk = specs(heads, width, last)
    o_fast, flags = pl.pallas_call(
        functools.partial(_sb_sample_kernel, heads=heads, strip=width // SB_BLOCK),
        grid=grid,
        in_specs=in_specs,
        out_specs=[o_blk, pl.BlockSpec((1, 1, 8, 128), lambda bi, hg: (bi, hg, 0, 0))],
        out_shape=[o_shape, jax.ShapeDtypeStruct((b, n_heads // heads, 8, 128), F32)],
        compiler_params=_cparams(("parallel", "parallel")),
        name="sb_sample_fast",
    )(q, kt_new, vt_new, cache_kt, cache_vt)
    if width == past:
        return o_fast

    def full():
        heads = 4
        grid, in_specs, o_blk = specs(heads, past, 0)
        return pl.pallas_call(
            functools.partial(_sb_sample_kernel, heads=heads, strip=1),
            grid=grid,
            in_specs=in_specs,
            out_specs=o_blk,
            out_shape=o_shape,
            compiler_params=_cparams(("parallel", "parallel")),
            name="sb_sample_full",
        )(q, kt_new, vt_new, cache_kt, cache_vt)

    return lax.cond(jnp.max(flags) > SB_DEAD, full, lambda: o_fast)


def _route(logits):
    tm = logits.shape[0]
    lane = lax.broadcasted_iota(jnp.int32, (tm, ROUTER_LANES), 1)
    lane_f = lane.astype(F32)
    neg = -jnp.inf
    first = lambda hit: jnp.min(jnp.where(hit, lane_f, float(ROUTER_LANES)), axis=-1, keepdims=True)

    gl = jnp.where(lane < N_GROUPS, logits, neg)
    g_max = jnp.max(gl, axis=-1, keepdims=True)
    g_top = 1.0 / jnp.sum(jnp.exp(gl - g_max), axis=-1, keepdims=True)
    g_idx = first(gl == g_max)

    lo = N_GROUPS + g_idx * EXPERTS_PER_GROUP
    in_group = (lane_f >= lo) & (lane_f < lo + EXPERTS_PER_GROUP)
    sel = jnp.where(in_group, logits, neg)
    t1 = jnp.max(sel, axis=-1, keepdims=True)
    i1 = first(sel == t1)
    sel2 = jnp.where(lane_f == i1, neg, sel)
    t2 = jnp.max(sel2, axis=-1, keepdims=True)
    i2 = first(sel2 == t2)
    e2 = jnp.exp(t2 - t1)
    den = 1.0 + e2
    w1 = (1.0 / den) * g_top
    w2 = (e2 / den) * g_top
    return jnp.where(lane_f == i1, w1, 0.0) + jnp.where(lane_f == i2, w2, 0.0)


CHANNEL_ROW_SPLITS = 2
EXPERTS_PER_TRIP = 8


def _channel_kernel(x_ref, o_ref, p_ref, wo_ref, gffn_ref, wr_ref, wg_ref, wu_ref, wd_ref, gple_ref,
                    wpg_ref, wpp_ref, out_ref, acc_ref):
    tm = x_ref.shape[0]
    d_expert = wd_ref.shape[2]
    half = tm // CHANNEL_ROW_SPLITS
    rows = [slice(i * half, (i + 1) * half) for i in range(CHANNEL_ROW_SPLITS)]

    h1s = [x_ref[r, :] + jnp.dot(o_ref[r, :], wo_ref[...], preferred_element_type=F32) for r in rows]
    xns = [_rms_unit(h1) * gffn_ref[...] for h1 in h1s]
    xnbs = [xn.astype(BF16) for xn in xns]
    xlos = [(xn - xnb.astype(F32)).astype(BF16) for xn, xnb in zip(xns, xnbs)]
    logits = [jnp.dot(jnp.concatenate([xnb, xlo, xnb], axis=-1), wr_ref[...], preferred_element_type=F32)
              for xnb, xlo in zip(xnbs, xlos)]
    gates = jnp.concatenate([_route(l) for l in logits], axis=0)
    xnb = jnp.concatenate(xnbs, axis=0)
    lane = lax.broadcasted_iota(jnp.int32, gates.shape, 1)
    for r, h1 in zip(rows, h1s):
        acc_ref[r, :] = h1

    def experts(j, _):
        es = [j * EXPERTS_PER_TRIP + k for k in range(EXPERTS_PER_TRIP)]
        gs = [jnp.dot(xnb, wg_ref[0, e], preferred_element_type=F32) for e in es]
        us = [jnp.dot(xnb, wu_ref[0, e], preferred_element_type=F32) for e in es]
        hids = []
        for e, g, u in zip(es, gs, us):
            gate_e = jnp.sum(jnp.where(lane == N_GROUPS + e, gates, 0.0), axis=-1, keepdims=True)
            hids.append(((g * jax.nn.sigmoid(g)) * u * gate_e).astype(BF16))
        wd = wd_ref[0, pl.ds(j * EXPERTS_PER_TRIP, EXPERTS_PER_TRIP)]
        acc_ref[...] += jnp.dot(jnp.concatenate(hids, axis=-1),
                                wd.reshape(EXPERTS_PER_TRIP * d_expert, wd.shape[-1]),
                                preferred_element_type=F32)
        return 0

    lax.fori_loop(0, N_EXPERTS // EXPERTS_PER_TRIP, experts, 0)
    h2s = [acc_ref[r, :] for r in rows]
    x3s = [(_rms_unit(h2) * gple_ref[...]).astype(BF16) for h2 in h2s]
    ple_gates = [jax.nn.sigmoid(jnp.dot(x3, wpg_ref[0], preferred_element_type=F32)) for x3 in x3s]
    projs = [jnp.dot(p_ref[0, r, :].astype(BF16), wpp_ref[0], preferred_element_type=F32) for r in rows]
    for r, h2, proj, gate in zip(rows, h2s, projs, ple_gates):
        out_ref[r, :] = h2 + proj * gate


def _channel(x, o, p, layer, w, stacks):
    n, d = x.shape
    tm = min(n, 512)
    row = lambda cols: pl.BlockSpec((tm, cols), lambda i: (i, 0))
    small = [w["wo"], w["gffn"], w["wr"]]
    moe = [stacks["wg"], stacks["wu"], stacks["wd"]]
    ple = [stacks["wpg"], stacks["wpp"]]
    return pl.pallas_call(
        _channel_kernel,
        grid=(n // tm,),
        in_specs=([row(d), row(d), pl.BlockSpec((1, tm, p.shape[2]), lambda i: (layer, i, 0))]
                  + [_resident(a.shape) for a in small] + [_resident_layer(a.shape, layer) for a in moe]
                  + [_resident(w["gple"].shape)] + [_resident_layer(a.shape, layer) for a in ple]),
        out_specs=row(d),
        out_shape=jax.ShapeDtypeStruct((n, d), F32),
        scratch_shapes=[pltpu.VMEM((tm, d), F32)],
        compiler_params=_cparams(("parallel",)),
        name="channel",
    )(x, o, p, *small, *moe, w["gple"], *ple)


PROJ_B_ROW_SPLITS = 2


def _proj_b_kernel(h_ref, ga_ref, wq_ref, gq_ref, gkv_ref, wkv_ref, gk_ref, gsum_ref, gexp_ref,
                   q_ref, k_ref, v_ref, *, n_heads, n_kv):
    tm = h_ref.shape[1]
    splits = PROJ_B_ROW_SPLITS if tm % (16 * PROJ_B_ROW_SPLITS) == 0 else 1
    rows = [slice(i * tm // splits, (i + 1) * tm // splits) for i in range(splits)]
    ys = [_rms_unit(h_ref[0, r, :]) for r in rows]
    qs = [jnp.dot((y * ga_ref[...]).astype(BF16), wq_ref[...], preferred_element_type=F32) for y in ys]
    kvs = [jnp.dot((y * gkv_ref[...]).astype(BF16), wkv_ref[...], preferred_element_type=F32) for y in ys]
    mss = [jnp.dot((q * q).astype(BF16), gsum_ref[...], preferred_element_type=F32) * (1.0 / HEAD_DIM)
           for q in qs]
    inv_fulls = [jnp.dot(_split_hl(lax.rsqrt(ms + EPS)), gexp_ref[...], preferred_element_type=F32)
                 for ms in mss]
    for r, q, inv_full in zip(rows, qs, inv_fulls):
        qn = (q * inv_full * gq_ref[...]) * (HEAD_DIM ** -0.5)
        for h in range(n_heads):
            q_ref[0, h, r, :] = qn[:, h * HEAD_DIM:(h + 1) * HEAD_DIM].astype(BF16)
    for r, kv in zip(rows, kvs):
        for h in range(n_kv):
            kh = kv[:, h * HEAD_DIM:(h + 1) * HEAD_DIM]
            k_ref[0, h, r, :] = _rms_unit(kh) * gk_ref[...]
            v_ref[0, h, r, :] = kv[:, (n_kv + h) * HEAD_DIM:(n_kv + h + 1) * HEAD_DIM]


def _proj_b(h, ga, wq, gq, gkv, wkv, gk, gsum, gexp):
    bx, t, d = h.shape
    n_heads = wq.shape[1] // HEAD_DIM
    n_kv = wkv.shape[1] // (2 * HEAD_DIM)
    tm = min(t, 512)
    hm = lambda b, i: (b, 0, i, 0)
    ins = [ga, wq, gq, gkv, wkv, gk, gsum, gexp]
    return pl.pallas_call(
        functools.partial(_proj_b_kernel, n_heads=n_heads, n_kv=n_kv),
        grid=(bx, t // tm),
        in_specs=[pl.BlockSpec((1, tm, d), lambda b, i: (b, i, 0))] + [_resident(a.shape) for a in ins],
        out_specs=[pl.BlockSpec((1, n_heads, tm, HEAD_DIM), hm),
                   pl.BlockSpec((1, n_kv, tm, HEAD_DIM), hm),
                   pl.BlockSpec((1, n_kv, tm, HEAD_DIM), hm)],
        out_shape=[jax.ShapeDtypeStruct((bx, n_heads, t, HEAD_DIM), BF16),
                   jax.ShapeDtypeStruct((bx, n_kv, t, HEAD_DIM), F32),
                   jax.ShapeDtypeStruct((bx, n_kv, t, HEAD_DIM), F32)],
        compiler_params=_cparams(("parallel", "parallel")),
        name="proj_b",
    )(h, *ins)


def _t5_bucket(rel):
    nb = NUM_BUCKETS // 2
    max_exact = nb // 2
    n = jnp.abs(rel)
    large = max_exact + (jnp.log(jnp.maximum(n, 1).astype(jnp.float32) / max_exact)
                         / math.log(MAX_DISTANCE / max_exact) * (nb - max_exact)).astype(jnp.int32)
    large = jnp.minimum(large, nb - 1)
    return jnp.where(rel > 0, nb, 0) + jnp.where(n < max_exact, n, large)


def _bias_kernel(bucket_ref, rbt_ref, out_ref):
    bucket = bucket_ref[...].astype(F32)
    guard = 2.0 ** -10
    acc = jnp.zeros(out_ref.shape, F32)
    for b in range(NUM_BUCKETS):
        hit = (bucket >= b - guard) & (bucket < b + 1 - guard)
        acc = acc + jnp.where(hit, rbt_ref[:, b:b + 1], 0.0)
    out_ref[...] = acc


def _bias_table(rel_bias, nq, nk, key_offset):
    rel = (jnp.arange(nk, dtype=jnp.int32)[None, :] - key_offset) - jnp.arange(nq, dtype=jnp.int32)[:, None]
    bucket = _t5_bucket(rel).reshape(1, nq * nk)
    n_heads = rel_bias.shape[1]
    out = pl.pallas_call(
        _bias_kernel,
        out_shape=jax.ShapeDtypeStruct((n_heads, nq * nk), F32),
        name="bias_table",
    )(bucket, rel_bias.T)
    return out.reshape(n_heads, nq, nk)


SWA_CHUNKS_PER_TRIP = 2


def _swa_kernel(sink_ref, q_ref, k_ref, v_ref, bias_ref, o_ref, kpad_ref, vpad_ref, *, cq, wl, pad, n_chunks,
                group):
    n_kv, tk = k_ref.shape[1], k_ref.shape[2]
    if pad:
        kpad_ref[:, 0:pad, :] = jnp.zeros((n_kv, pad, HEAD_DIM), BF16)
        vpad_ref[:, 0:pad, :] = jnp.zeros((n_kv, pad, HEAD_DIM), BF16)
    kpad_ref[:, pad:pad + tk, :] = k_ref[0].astype(BF16)
    vpad_ref[:, pad:pad + tk, :] = v_ref[0].astype(BF16)

    per_trip = SWA_CHUNKS_PER_TRIP if n_chunks % SWA_CHUNKS_PER_TRIP == 0 else 1
    n_heads = n_kv * group

    def chunks(c, _):
        r0s = [pl.multiple_of((c * per_trip + i) * cq, cq) for i in range(per_trip)]
        logits = [[lax.dot_general(
            q_ref[0, kv * group:(kv + 1) * group, pl.ds(r0, cq), :].reshape(group * cq, HEAD_DIM),
            kpad_ref[kv, pl.ds(r0, wl), :], _NT, preferred_element_type=F32)
            for kv in range(n_kv)] for r0 in r0s]
        es, dens = [], []
        for r0, lg in zip(r0s, logits):
            valid = lax.broadcasted_iota(jnp.int32, (cq, wl), 1) + r0 >= pad
            for h in range(n_heads):
                l = lg[h // group][(h % group) * cq:(h % group + 1) * cq] + bias_ref[h]
                if pad:
                    l = jnp.where(valid, l, -jnp.inf)
                sink = sink_ref[h]
                m = jnp.maximum(jnp.max(l, axis=-1, keepdims=True), sink)
                e = jnp.exp(l - m)
                dens.append(jnp.sum(e, axis=-1, keepdims=True) + jnp.exp(sink - m))
                es.append(e.astype(BF16))
        pvs = [[jnp.dot(jnp.concatenate(es[i * n_heads + kv * group:i * n_heads + (kv + 1) * group], axis=0),
                        vpad_ref[kv, pl.ds(r0, wl), :], preferred_element_type=F32)
                for kv in range(n_kv)] for i, r0 in enumerate(r0s)]
        for i, r0 in enumerate(r0s):
            outs = [pvs[i][h // group][(h % group) * cq:(h % group + 1) * cq] / dens[i * n_heads + h]
                    for h in range(n_heads)]
            o_ref[0, pl.ds(r0, cq), :] = jnp.concatenate(outs, axis=-1).astype(BF16)
        return 0

    lax.fori_loop(0, n_chunks // per_trip, chunks, 0)


def _swa(q, k_win, v_win, bias, sinks, *, cq, wl, pad):
    b, n_heads, tq, _ = q.shape
    n_kv, tk = k_win.shape[1], k_win.shape[2]
    n_chunks = tq // cq
    assert (n_chunks - 1) * cq + wl == pad + tk
    kv_blk = pl.BlockSpec((1, n_kv, tk, HEAD_DIM), lambda bi: (bi, 0, 0, 0))
    return pl.pallas_call(
        functools.partial(_swa_kernel, cq=cq, wl=wl, pad=pad, n_chunks=n_chunks, group=n_heads // n_kv),
        grid=(b,),
        in_specs=[pl.BlockSpec(memory_space=pltpu.SMEM),
                  pl.BlockSpec((1, n_heads, tq, HEAD_DIM), lambda bi: (bi, 0, 0, 0)),
                  kv_blk, kv_blk, _resident(bias.shape)],
        out_specs=pl.BlockSpec((1, tq, n_heads * HEAD_DIM), lambda bi: (bi, 0, 0)),
        out_shape=jax.ShapeDtypeStruct((b, tq, n_heads * HEAD_DIM), BF16),
        scratch_shapes=[pltpu.VMEM((n_kv, pad + tk, HEAD_DIM), BF16),
                        pltpu.VMEM((n_kv, pad + tk, HEAD_DIM), BF16)],
        compiler_params=_cparams(("parallel",)),
        name="swa",
    )(sinks, q, k_win, v_win, bias)


def _row(v):
    return v.reshape(1, -1).astype(F32)


def _router_hi_hi_lo(wr):
    hi = wr.astype(BF16)
    lo = (wr - hi.astype(F32)).astype(BF16)
    return jnp.concatenate([hi, hi, lo], axis=0)


def _prep_weights(prm):
    d = prm["a_w_o"].shape[1]
    depth = prm["norm_ffn"].shape[0]
    n_a = prm["a_w_qkv"].shape[0]
    scale = HEAD_DIM ** -0.5
    w = {"channel": [], "n_a": n_a, "depth": depth}
    for i in range(depth):
        wo = prm["a_w_o"][i] if i < n_a else prm["b_w_o"][i - n_a]
        pad = ROUTER_LANES - N_GROUPS - N_EXPERTS
        wr = jnp.concatenate([prm["moe_w_group"][i], prm["moe_w_router"][i].reshape(d, N_EXPERTS),
                              jnp.zeros((d, pad), F32)], axis=1)
        w["channel"].append({"wo": wo.astype(BF16), "gffn": _row(prm["norm_ffn"][i]),
                             "wr": _router_hi_hi_lo(wr), "gple": _row(prm["norm_ple"][i])})
    w["stacks"] = {"wg": prm["moe_w_gate"].astype(BF16), "wu": prm["moe_w_up"].astype(BF16),
                   "wd": prm["moe_w_down"].astype(BF16), "wpg": prm["ple_w_gate"].astype(BF16),
                   "wpp": prm["ple_w_proj"].astype(BF16)}
    w["qkv"] = []
    for i in range(n_a):
        wq = prm["a_w_qkv"][i]
        hd = wq.shape[1] // 3
        w["qkv"].append(((wq[:, :hd] * scale).astype(BF16),
                         wq[:, hd:].T.astype(BF16)))
    n_heads = prm["b_w_q"].shape[2] // HEAD_DIM
    head_of = jnp.arange(n_heads * HEAD_DIM, dtype=jnp.int32) // HEAD_DIM
    lanes = jnp.arange(ROUTER_LANES, dtype=jnp.int32)
    member = (head_of[:, None] == lanes[None, :]).astype(BF16)
    w["gsum"] = member
    w["gexp"] = jnp.concatenate([member.T, member.T], axis=0)
    w["wq_b"] = [prm["b_w_q"][j].astype(BF16) for j in range(depth - n_a)]
    w["gq_b"] = [_row(jnp.tile(prm["b_q_norm"][j], n_heads)) for j in range(depth - n_a)]
    w["wkv"] = prm["b_w_kv"].astype(BF16)
    return w


def _assert_sample_window_visible(past_len, s, tk):
    q_chunk = [(past_len + i) // CHUNK for i in range(s)]
    k_pos = [past_len + s - tk + j for j in range(tk)]
    ok = all(kp >= 0 and qc - WIN_CHUNKS <= kp // CHUNK <= qc for qc in q_chunk for kp in k_pos)
    if not ok:
        raise NotImplementedError("sample window with masked keys")


def _run_trunk(x, p, prm, w, sb_cache_k=None, sb_cache_v=None, swa_cache_k=None, swa_cache_v=None):
    bx, t, d = x.shape
    n_a, depth = w["n_a"], w["depth"]
    sample = sb_cache_k is not None
    h = x
    sb_k, sb_v = [], []
    k_win = v_win = None
    q_b = None
    for i in range(depth):
        if i < n_a:
            q, kt, vt = _proj_a(h, _row(prm["norm_attn"][i]), *w["qkv"][i])
            sb_k.append(kt)
            sb_v.append(vt)
            if sample:
                o = _sb_sample(q, kt, vt, jnp.swapaxes(sb_cache_k, -1, -2), jnp.swapaxes(sb_cache_v, -1, -2), i)
            else:
                o = _sb_prompt(q, kt, vt)
        else:
            j = i - n_a
            if j > 0:
                raise NotImplementedError("one B layer supported")
            if sample:
                tk = k_win.shape[2]
                _assert_sample_window_visible(sb_cache_k.shape[3], t, tk)
                o = _swa(q_b, k_win, v_win, _bias_table(prm["rel_bias"], t, tk, tk - t),
                         prm["b_sinks"][j], cq=t, wl=tk, pad=0)
            else:
                wl = WINDOW + CHUNK
                o = _swa(q_b, k_win, v_win, _bias_table(prm["rel_bias"], CHUNK, wl, WINDOW),
                         prm["b_sinks"][j], cq=CHUNK, wl=wl, pad=WINDOW)
        h = _channel(h.reshape(bx * t, d), o.reshape(bx * t, d), p.reshape(depth, bx * t, -1), i,
                     w["channel"][i], w["stacks"]).reshape(bx, t, d)
        if i == n_a - 1:
            q_b, k_s, v_s = _proj_b(h, _row(prm["norm_attn"][n_a]), w["wq_b"][0], w["gq_b"][0],
                                    _row(prm["kv_norm"]), w["wkv"], _row(prm["b_k_norm"]),
                                    w["gsum"], w["gexp"])
            if sample:
                k_win = jnp.concatenate([swa_cache_k, k_s], axis=2)
                v_win = jnp.concatenate([swa_cache_v, v_s], axis=2)
            else:
                k_win, v_win = k_s, v_s
    sb_k = jnp.swapaxes(jnp.stack(sb_k), -1, -2)
    sb_v = jnp.swapaxes(jnp.stack(sb_v), -1, -2)
    return h, sb_k, sb_v, k_win[:, :, -WINDOW:], v_win[:, :, -WINDOW:]


def kernel(x_prompt, x_sample, p_prompt, p_sample, cache_sb_k, cache_sb_v, cache_swa_k, cache_swa_v, norm_attn, norm_ffn, norm_ple, a_w_qkv, a_w_o, kv_norm, b_w_kv, b_k_norm, b_w_q, b_q_norm, b_sinks, b_w_o, rel_bias, moe_w_group, moe_w_router, moe_w_gate, moe_w_up, moe_w_down, ple_w_proj, ple_w_gate):
    prm = {
        "norm_attn": norm_attn, "norm_ffn": norm_ffn, "norm_ple": norm_ple,
        "a_w_qkv": a_w_qkv, "a_w_o": a_w_o, "kv_norm": kv_norm, "b_w_kv": b_w_kv, "b_k_norm": b_k_norm,
        "b_w_q": b_w_q, "b_q_norm": b_q_norm, "b_sinks": b_sinks, "b_w_o": b_w_o, "rel_bias": rel_bias,
        "moe_w_group": moe_w_group, "moe_w_router": moe_w_router, "moe_w_gate": moe_w_gate,
        "moe_w_up": moe_w_up, "moe_w_down": moe_w_down, "ple_w_proj": ple_w_proj, "ple_w_gate": ple_w_gate,
    }
    w = _prep_weights(prm)
    y_p, sb_k_p, sb_v_p, swa_k_p, swa_v_p = _run_trunk(x_prompt, p_prompt, prm, w)
    y_s, sb_k_s, sb_v_s, swa_k_s, swa_v_s = _run_trunk(x_sample, p_sample, prm, w, cache_sb_k, cache_sb_v,
                                                       cache_swa_k, cache_swa_v)
    return (y_p, y_s, sb_k_p, sb_v_p, swa_k_p, swa_v_p, sb_k_s, sb_v_s, swa_k_s, swa_v_s)
```

```python
import functools
import math

import jax
import jax.numpy as jnp
from jax import lax
from jax.experimental import pallas as pl
from jax.experimental.pallas import tpu as pltpu

F32 = jnp.float32
BF16 = jnp.bfloat16

HEAD_DIM = 64
CHUNK = 64
WINDOW = 128
WIN_CHUNKS = WINDOW // CHUNK
NUM_BUCKETS = 32
MAX_DISTANCE = 128
N_GROUPS = 4
EXPERTS_PER_GROUP = 4
N_EXPERTS = N_GROUPS * EXPERTS_PER_GROUP
EPS = 1e-6
LOG2E = 1.4426950408889634
SB_BLOCK = 128
ROUTER_LANES = 128
VMEM_LIMIT = 56 * 1024 * 1024

_NT = (((1,), (1,)), ((), ()))


def _cparams(sem):
    return pltpu.CompilerParams(dimension_semantics=sem, vmem_limit_bytes=VMEM_LIMIT)


def _rms_unit(x):
    return x * lax.rsqrt(jnp.mean(x * x, axis=-1, keepdims=True) + EPS)


def _split_hl(a):
    hi = a.astype(BF16)
    lo = (a - hi.astype(F32)).astype(BF16)
    return jnp.concatenate([hi, lo], axis=-1)


def _resident(shape):
    nd = len(shape)
    return pl.BlockSpec(shape, lambda *_: (0,) * nd, pipeline_mode=pl.Buffered(1))


def _resident_layer(shape, layer):
    nd = len(shape)
    return pl.BlockSpec((1,) + tuple(shape[1:]), lambda *_: (layer,) + (0,) * (nd - 1),
                        pipeline_mode=pl.Buffered(1))


def _proj_a_kernel(x_ref, g_ref, wq_ref, wkvt_ref, q_ref, kt_ref, vt_ref, *, n_heads):
    xn = (_rms_unit(x_ref[0]) * g_ref[...]).astype(BF16)
    heads_per_dot = 4
    width = heads_per_dot * HEAD_DIM
    for c in range(n_heads // heads_per_dot):
        r = jnp.dot(xn, wq_ref[:, c * width:(c + 1) * width], preferred_element_type=F32)
        for hh in range(heads_per_dot):
            q_ref[0, c * heads_per_dot + hh] = r[:, hh * HEAD_DIM:(hh + 1) * HEAD_DIM].astype(BF16)
    for c in range(2 * n_heads // heads_per_dot):
        r = lax.dot_general(wkvt_ref[c * width:(c + 1) * width, :], xn, _NT, preferred_element_type=F32)
        for hh in range(heads_per_dot):
            h = c * heads_per_dot + hh
            piece = r[hh * HEAD_DIM:(hh + 1) * HEAD_DIM, :]
            if h < n_heads:
                kt_ref[0, h] = piece
            else:
                vt_ref[0, h - n_heads] = piece


def _proj_a(x, g, wq, wkvt):
    bx, t, d = x.shape
    n_heads = wq.shape[1] // HEAD_DIM
    tm = min(t, 256)
    return pl.pallas_call(
        functools.partial(_proj_a_kernel, n_heads=n_heads),
        grid=(bx, t // tm),
        in_specs=[pl.BlockSpec((1, tm, d), lambda b, i: (b, i, 0)),
                  _resident((1, d)), _resident(wq.shape), _resident(wkvt.shape)],
        out_specs=[pl.BlockSpec((1, n_heads, tm, HEAD_DIM), lambda b, i: (b, 0, i, 0)),
                   pl.BlockSpec((1, n_heads, HEAD_DIM, tm), lambda b, i: (b, 0, 0, i)),
                   pl.BlockSpec((1, n_heads, HEAD_DIM, tm), lambda b, i: (b, 0, 0, i))],
        out_shape=[jax.ShapeDtypeStruct((bx, n_heads, t, HEAD_DIM), BF16),
                   jax.ShapeDtypeStruct((bx, n_heads, HEAD_DIM, t), F32),
                   jax.ShapeDtypeStruct((bx, n_heads, HEAD_DIM, t), F32)],
        compiler_params=_cparams(("parallel", "parallel")),
        name="proj_a",
    )(x, g, wq, wkvt)


SB_DEAD = -104.0
SB_WINDOW_BLOCKS = 3
SB_SAMPLE_WINDOW = 256


def _suffix_neg_ones(n):
    r = lax.broadcasted_iota(jnp.int32, (n, n), 0)
    c = lax.broadcasted_iota(jnp.int32, (n, n), 1)
    return jnp.where(r >= c, -1.0, 0.0).astype(BF16)


def _strict_mask(n):
    row = lax.broadcasted_iota(jnp.int32, (n, n), 0)
    col = lax.broadcasted_iota(jnp.int32, (n, n), 1)
    return col < row


def _softplus(z):
    return jnp.maximum(z, 0.0) + jnp.log(1.0 + jnp.exp2(jnp.abs(z) * -LOG2E))


def _sb_strips(qs, kts, vts, carries, accs, uu, masks):
    n = len(qs)
    blk = uu.shape[1]
    nb = kts[0].shape[1] // blk
    zs = [jnp.dot(qs[i], kts[i], preferred_element_type=F32) for i in range(n)]
    sps = [_softplus(z) for z in zs]
    ws = [[None] * nb for _ in range(n)]
    carries = list(carries)
    for b in reversed(range(nb)):
        sl = slice(b * blk, (b + 1) * blk)
        spbs = [sp[:, sl] if masks[b] is None else jnp.where(masks[b], sp[:, sl], 0.0) for sp in sps]
        sufs = [jnp.dot(spb.astype(BF16), uu, preferred_element_type=F32) for spb in spbs]
        for i in range(n):
            wb = jnp.exp(zs[i][:, sl] + sufs[i] + carries[i])
            if masks[b] is not None:
                wb = jnp.where(masks[b], wb, 0.0)
            ws[i][b] = wb.astype(BF16)
            carries[i] = carries[i] + sufs[i][:, 0:1]
    wcat = [w[0] if nb == 1 else jnp.concatenate(w, axis=-1) for w in ws]
    accs = [accs[i] + lax.dot_general(wcat[i], vts[i], _NT, preferred_element_type=F32) for i in range(n)]
    return carries, accs


SB_STAGES = 5
SB_OLD_ROWS = 64


def _sb_prompt_fast_kernel(q_ref, kt_ref, vt_ref, o_ref, flag_ref, z_ref, sp_ref, suf_ref, w_ref, *, heads, groups):
    t = q_ref.shape[2]
    blk = SB_BLOCK
    old = SB_OLD_ROWS
    nq = t // blk
    win = SB_WINDOW_BLOCKS
    wl = win * blk
    ring = z_ref.shape[0]
    first = win - 1
    per_group = nq - first
    units = groups * per_group
    width = heads * HEAD_DIM
    uu = _suffix_neg_ones(blk)
    strict = _strict_mask(blk)
    z_ref[...] = jnp.zeros(z_ref.shape, F32)
    sp_ref[...] = jnp.zeros(sp_ref.shape, BF16)
    suf_ref[...] = jnp.zeros(suf_ref.shape, F32)
    w_ref[...] = jnp.zeros(w_ref.shape, BF16)

    hs = range(groups * heads)
    for i in range(first):
        _, accs = _sb_strips([q_ref[0, h, i * blk:(i + 1) * blk, :] for h in hs],
                             [kt_ref[0, h, :, 0:(i + 1) * blk].astype(BF16) for h in hs],
                             [vt_ref[0, h, :, 0:(i + 1) * blk].astype(BF16) for h in hs],
                             [jnp.zeros((blk, 1), F32)] * len(hs), [jnp.zeros((blk, HEAD_DIM), F32)] * len(hs),
                             uu, [None] * i + [strict])
        o_ref[0, i * blk:(i + 1) * blk, :] = jnp.concatenate(accs, axis=-1).astype(BF16)

    def where_is(n):
        u = jnp.clip(n, 0, units - 1)
        g = u // per_group
        i = first + u - g * per_group
        q0 = pl.multiple_of(i * blk, blk)
        k0 = pl.multiple_of((i - first) * blk, blk)
        return g * heads, q0, k0, lax.rem(n + ring * SB_STAGES, ring)

    def body(n, worst):
        h0, q0, k0, _ = where_is(n - 4)
        accs = []
        for h in range(heads):
            vt = vt_ref[0, h0 + h, :, pl.ds(k0, wl)].astype(BF16)
            accs.append(lax.dot_general(w_ref[h], vt, _NT, preferred_element_type=F32))
        o_ref[0, pl.ds(q0, blk), pl.ds(pl.multiple_of(h0 * HEAD_DIM, width), width)] = (
            jnp.concatenate(accs, axis=-1).astype(BF16))

        _, _, _, slot = where_is(n - 3)
        counts = n - 3 >= 0
        for h in range(heads):
            s_old, s_prev, s_own = suf_ref[h, 0:old], suf_ref[h, old:old + blk], suf_ref[h, old + blk:]
            w_own = jnp.where(strict, jnp.exp(z_ref[slot, h, :, 2 * blk:] + s_own), 0.0)
            carry = s_own[:, 0:1]
            w_prev = jnp.exp(z_ref[slot, h, :, blk:2 * blk] + s_prev + carry)
            carry = carry + s_prev[:, 0:1]
            w_old = jnp.exp(z_ref[slot, h, 0:old, 0:blk] + s_old + carry[0:old])
            w_ref[h, :, 2 * blk:] = w_own.astype(BF16)
            w_ref[h, :, blk:2 * blk] = w_prev.astype(BF16)
            w_ref[h, 0:old, 0:blk] = w_old.astype(BF16)
            edge = jnp.concatenate([carry[0:old] + s_old[:, 0:1], carry[old:]], axis=0)
            worst = jnp.maximum(worst, jnp.where(counts, edge, -jnp.inf))

        for h in range(heads):
            suf_ref[h] = jnp.dot(sp_ref[h], uu, preferred_element_type=F32)

        _, _, _, slot = where_is(n - 1)
        for h in range(heads):
            sp_new = _softplus(z_ref[slot, h, :, blk:])
            sp_ref[h, 0:old] = _softplus(z_ref[slot, h, 0:old, 0:blk]).astype(BF16)
            sp_ref[h, old:old + blk] = sp_new[:, 0:blk].astype(BF16)
            sp_ref[h, old + blk:] = jnp.where(strict, sp_new[:, blk:], 0.0).astype(BF16)

        h0, q0, k0, slot = where_is(n)
        for h in range(heads):
            q = q_ref[0, h0 + h, pl.ds(q0, blk), :]
            kt = kt_ref[0, h0 + h, :, pl.ds(k0, wl)].astype(BF16)
            z_ref[slot, h, :, blk:] = jnp.dot(q, kt[:, blk:], preferred_element_type=F32)
            z_ref[slot, h, 0:old, 0:blk] = jnp.dot(q[0:old], kt[:, 0:blk], preferred_element_type=F32)
        return worst

    worst = lax.fori_loop(0, units + SB_STAGES - 1, body, jnp.full((blk, 1), -jnp.inf, F32))
    flag_ref[...] = jnp.broadcast_to(jnp.max(worst), flag_ref.shape)


def _sb_prompt_full_kernel(q_ref, kt_ref, vt_ref, o_ref, *, heads):
    t = q_ref.shape[2]
    blk = SB_BLOCK
    uu = _suffix_neg_ones(blk)
    strict = _strict_mask(blk)

    def q_block(i, _):
        q0 = pl.multiple_of(i * blk, blk)
        qs = [q_ref[0, h, pl.ds(q0, blk), :] for h in range(heads)]

        def strips(k0, carries, accs, masks):
            return _sb_strips(qs, [kt_ref[0, h, :, pl.ds(k0, blk)].astype(BF16) for h in range(heads)],
                              [vt_ref[0, h, :, pl.ds(k0, blk)].astype(BF16) for h in range(heads)],
                              carries, accs, uu, masks)

        state = strips(q0, [jnp.zeros((blk, 1), F32)] * heads, [jnp.zeros((blk, HEAD_DIM), F32)] * heads,
                       [strict])

        def k_block(jj, st):
            carries, accs = strips(pl.multiple_of((i - 1 - jj) * blk, blk), st[0], st[1], [None])
            return tuple(carries), tuple(accs)

        _, accs = lax.fori_loop(0, i, k_block, (tuple(state[0]), tuple(state[1])))
        o_ref[0, pl.ds(q0, blk), :] = jnp.concatenate(list(accs), axis=-1).astype(BF16)
        return 0

    lax.fori_loop(0, t // blk, q_block, 0)


def _sb_prompt(q, kt, vt):
    b, n_heads, t, _ = q.shape
    wl = SB_WINDOW_BLOCKS * SB_BLOCK
    assert t >= wl
    o_shape = jax.ShapeDtypeStruct((b, t, n_heads * HEAD_DIM), BF16)

    def specs(hb):
        q_blk = pl.BlockSpec((1, hb, t, HEAD_DIM), lambda bi, hg: (bi, hg, 0, 0))
        kv_blk = pl.BlockSpec((1, hb, HEAD_DIM, t), lambda bi, hg: (bi, hg, 0, 0))
        o_blk = pl.BlockSpec((1, t, hb * HEAD_DIM), lambda bi, hg: (bi, 0, hg))
        return (b, n_heads // hb), [q_blk, kv_blk, kv_blk], o_blk

    heads, groups = 4, 2
    rows = SB_OLD_ROWS + 2 * SB_BLOCK
    grid, in_specs, o_blk = specs(heads * groups)
    o_fast, flags = pl.pallas_call(
        functools.partial(_sb_prompt_fast_kernel, heads=heads, groups=groups),
        grid=grid,
        in_specs=in_specs,
        out_specs=[o_blk, pl.BlockSpec((1, 1, 8, 128), lambda bi, hg: (bi, hg, 0, 0))],
        out_shape=[o_shape, jax.ShapeDtypeStruct((b, grid[1], 8, 128), F32)],
        scratch_shapes=[pltpu.VMEM((SB_STAGES - 1, heads, SB_BLOCK, wl), F32),
                        pltpu.VMEM((heads, rows, SB_BLOCK), BF16),
                        pltpu.VMEM((heads, rows, SB_BLOCK), F32),
                        pltpu.VMEM((heads, SB_BLOCK, wl), BF16)],
        compiler_params=_cparams(("parallel", "parallel")),
        name="sb_prompt_fast",
    )(q, kt, vt)

    def full():
        grid, in_specs, o_blk = specs(4)
        return pl.pallas_call(
            functools.partial(_sb_prompt_full_kernel, heads=4),
            grid=grid,
            in_specs=in_specs,
            out_specs=o_blk,
            out_shape=o_shape,
            compiler_params=_cparams(("parallel", "parallel")),
            name="sb_prompt_full",
        )(q, kt, vt)

    return lax.cond(jnp.max(flags) > SB_DEAD, full, lambda: o_fast)


def _sb_sample_kernel(q_ref, ktn_ref, vtn_ref, ktc_ref, vtc_ref, o_ref, *maybe_flag, heads, strip):
    s = q_ref.shape[2]
    width = ktc_ref.shape[4]
    blk = SB_BLOCK
    uu = _suffix_neg_ones(blk)
    uu_new = _suffix_neg_ones(s)
    strict = _strict_mask(s)

    qs = [q_ref[0, h] for h in range(heads)]
    state = _sb_strips(qs, [ktn_ref[0, h].astype(BF16) for h in range(heads)],
                       [vtn_ref[0, h].astype(BF16) for h in range(heads)],
                       [jnp.zeros((s, 1), F32)] * heads, [jnp.zeros((s, HEAD_DIM), F32)] * heads,
                       uu_new, [strict])

    def k_strip(jj, st):
        k0 = pl.multiple_of(width - (jj + 1) * strip * blk, blk)
        carries, accs = _sb_strips(
            qs, [ktc_ref[0, 0, h, :, pl.ds(k0, strip * blk)].astype(BF16) for h in range(heads)],
            [vtc_ref[0, 0, h, :, pl.ds(k0, strip * blk)].astype(BF16) for h in range(heads)],
            st[0], st[1], uu, [None] * strip)
        return tuple(carries), tuple(accs)

    carries, accs = lax.fori_loop(0, width // (strip * blk), k_strip, (tuple(state[0]), tuple(state[1])))
    o_ref[0] = jnp.concatenate(list(accs), axis=-1).astype(BF16)
    if maybe_flag:
        worst = carries[0]
        for h in range(1, heads):
            worst = jnp.maximum(worst, carries[h])
        maybe_flag[0][...] = jnp.broadcast_to(jnp.max(worst), maybe_flag[0].shape)


def _sb_sample(q, kt_new, vt_new, cache_kt, cache_vt, layer):
    b, n_heads, s, _ = q.shape
    past = cache_kt.shape[4]
    width = min(past, SB_SAMPLE_WINDOW)
    assert past % width == 0 and width % SB_BLOCK == 0
    last = past // width - 1
    o_shape = jax.ShapeDtypeStruct((b, s, n_heads * HEAD_DIM), BF16)

    def specs(heads, cache_width, cache_block):
        q_blk = pl.BlockSpec((1, heads, s, HEAD_DIM), lambda bi, hg: (bi, hg, 0, 0))
        new_blk = pl.BlockSpec((1, heads, HEAD_DIM, s), lambda bi, hg: (bi, hg, 0, 0))
        cache_blk = pl.BlockSpec((1, 1, heads, HEAD_DIM, cache_width),
                                 lambda bi, hg: (layer, bi, hg, 0, cache_block))
        o_blk = pl.BlockSpec((1, s, heads * HEAD_DIM), lambda bi, hg: (bi, 0, hg))
        return (b, n_heads // heads), [q_blk, new_blk, new_blk, cache_blk, cache_blk], o_blk

    heads = n_heads
    grid, in_specs, o_blk = specs(heads, width, last)
    o_fast, flags = pl.pallas_call(
        functools.partial(_sb_sample_kernel, heads=heads, strip=width // SB_BLOCK),
        grid=grid,
        in_specs=in_specs,
        out_specs=[o_blk, pl.BlockSpec((1, 1, 8, 128), lambda bi, hg: (bi, hg, 0, 0))],
        out_shape=[o_shape, jax.ShapeDtypeStruct((b, n_heads // heads, 8, 128), F32)],
        compiler_params=_cparams(("parallel", "parallel")),
        name="sb_sample_fast",
    )(q, kt_new, vt_new, cache_kt, cache_vt)
    if width == past:
        return o_fast

    def full():
        heads = 4
        grid, in_specs, o_blk = specs(heads, past, 0)
        return pl.pallas_call(
            functools.partial(_sb_sample_kernel, heads=heads, strip=1),
            grid=grid,
            in_specs=in_specs,
            out_specs=o_blk,
            out_shape=o_shape,
            compiler_params=_cparams(("parallel", "parallel")),
            name="sb_sample_full",
        )(q, kt_new, vt_new, cache_kt, cache_vt)

    return lax.cond(jnp.max(flags) > SB_DEAD, full, lambda: o_fast)


def _route(logits):
    tm = logits.shape[0]
    lane = lax.broadcasted_iota(jnp.int32, (tm, ROUTER_LANES), 1)
    lane_f = lane.astype(F32)
    neg = -jnp.inf
    first = lambda hit: jnp.min(jnp.where(hit, lane_f, float(ROUTER_LANES)), axis=-1, keepdims=True)

    gl = jnp.where(lane < N_GROUPS, logits, neg)
    g_max = jnp.max(gl, axis=-1, keepdims=True)
    g_top = 1.0 / jnp.sum(jnp.exp(gl - g_max), axis=-1, keepdims=True)
    g_idx = first(gl == g_max)

    lo = N_GROUPS + g_idx * EXPERTS_PER_GROUP
    in_group = (lane_f >= lo) & (lane_f < lo + EXPERTS_PER_GROUP)
    sel = jnp.where(in_group, logits, neg)
    t1 = jnp.max(sel, axis=-1, keepdims=True)
    i1 = first(sel == t1)
    sel2 = jnp.where(lane_f == i1, neg, sel)
    t2 = jnp.max(sel2, axis=-1, keepdims=True)
    i2 = first(sel2 == t2)
    e2 = jnp.exp(t2 - t1)
    den = 1.0 + e2
    w1 = (1.0 / den) * g_top
    w2 = (e2 / den) * g_top
    return jnp.where(lane_f == i1, w1, 0.0) + jnp.where(lane_f == i2, w2, 0.0)


CHANNEL_ROW_SPLITS = 2
EXPERTS_PER_TRIP = 8


def _channel_kernel(x_ref, o_ref, p_ref, wo_ref, gffn_ref, wr_ref, wg_ref, wu_ref, wd_ref, gple_ref,
                    wpg_ref, wpp_ref, out_ref, acc_ref):
    tm = x_ref.shape[0]
    d_expert = wd_ref.shape[2]
    half = tm // CHANNEL_ROW_SPLITS
    rows = [slice(i * half, (i + 1) * half) for i in range(CHANNEL_ROW_SPLITS)]

    h1s = [x_ref[r, :] + jnp.dot(o_ref[r, :], wo_ref[...], preferred_element_type=F32) for r in rows]
    xns = [_rms_unit(h1) * gffn_ref[...] for h1 in h1s]
    xnbs = [xn.astype(BF16) for xn in xns]
    xlos = [(xn - xnb.astype(F32)).astype(BF16) for xn, xnb in zip(xns, xnbs)]
    logits = [jnp.dot(jnp.concatenate([xnb, xlo, xnb], axis=-1), wr_ref[...], preferred_element_type=F32)
              for xnb, xlo in zip(xnbs, xlos)]
    gates = jnp.concatenate([_route(l) for l in logits], axis=0)
    xnb = jnp.concatenate(xnbs, axis=0)
    lane = lax.broadcasted_iota(jnp.int32, gates.shape, 1)
    for r, h1 in zip(rows, h1s):
        acc_ref[r, :] = h1

    def experts(j, _):
        es = [j * EXPERTS_PER_TRIP + k for k in range(EXPERTS_PER_TRIP)]
        gs = [jnp.dot(xnb, wg_ref[0, e], preferred_element_type=F32) for e in es]
        us = [jnp.dot(xnb, wu_ref[0, e], preferred_element_type=F32) for e in es]
        hids = []
        for e, g, u in zip(es, gs, us):
            gate_e = jnp.sum(jnp.where(lane == N_GROUPS + e, gates, 0.0), axis=-1, keepdims=True)
            hids.append(((g * jax.nn.sigmoid(g)) * u * gate_e).astype(BF16))
        wd = wd_ref[0, pl.ds(j * EXPERTS_PER_TRIP, EXPERTS_PER_TRIP)]
        acc_ref[...] += jnp.dot(jnp.concatenate(hids, axis=-1),
                                wd.reshape(EXPERTS_PER_TRIP * d_expert, wd.shape[-1]),
                                preferred_element_type=F32)
        return 0

    lax.fori_loop(0, N_EXPERTS // EXPERTS_PER_TRIP, experts, 0)
    h2s = [acc_ref[r, :] for r in rows]
    x3s = [(_rms_unit(h2) * gple_ref[...]).astype(BF16) for h2 in h2s]
    ple_gates = [jax.nn.sigmoid(jnp.dot(x3, wpg_ref[0], preferred_element_type=F32)) for x3 in x3s]
    projs = [jnp.dot(p_ref[0, r, :].astype(BF16), wpp_ref[0], preferred_element_type=F32) for r in rows]
    for r, h2, proj, gate in zip(rows, h2s, projs, ple_gates):
        out_ref[r, :] = h2 + proj * gate


def _channel(x, o, p, layer, w, stacks):
    n, d = x.shape
    tm = min(n, 512)
    row = lambda cols: pl.BlockSpec((tm, cols), lambda i: (i, 0))
    small = [w["wo"], w["gffn"], w["wr"]]
    moe = [stacks["wg"], stacks["wu"], stacks["wd"]]
    ple = [stacks["wpg"], stacks["wpp"]]
    return pl.pallas_call(
        _channel_kernel,
        grid=(n // tm,),
        in_specs=([row(d), row(d), pl.BlockSpec((1, tm, p.shape[2]), lambda i: (layer, i, 0))]
                  + [_resident(a.shape) for a in small] + [_resident_layer(a.shape, layer) for a in moe]
                  + [_resident(w["gple"].shape)] + [_resident_layer(a.shape, layer) for a in ple]),
        out_specs=row(d),
        out_shape=jax.ShapeDtypeStruct((n, d), F32),
        scratch_shapes=[pltpu.VMEM((tm, d), F32)],
        compiler_params=_cparams(("parallel",)),
        name="channel",
    )(x, o, p, *small, *moe, w["gple"], *ple)


PROJ_B_ROW_SPLITS = 2


def _proj_b_kernel(h_ref, ga_ref, wq_ref, gq_ref, gkv_ref, wkv_ref, gk_ref, gsum_ref, gexp_ref,
                   q_ref, k_ref, v_ref, *, n_heads, n_kv):
    tm = h_ref.shape[1]
    splits = PROJ_B_ROW_SPLITS if tm % (16 * PROJ_B_ROW_SPLITS) == 0 else 1
    rows = [slice(i * tm // splits, (i + 1) * tm // splits) for i in range(splits)]
    ys = [_rms_unit(h_ref[0, r, :]) for r in rows]
    qs = [jnp.dot((y * ga_ref[...]).astype(BF16), wq_ref[...], preferred_element_type=F32) for y in ys]
    kvs = [jnp.dot((y * gkv_ref[...]).astype(BF16), wkv_ref[...], preferred_element_type=F32) for y in ys]
    mss = [jnp.dot((q * q).astype(BF16), gsum_ref[...], preferred_element_type=F32) * (1.0 / HEAD_DIM)
           for q in qs]
    inv_fulls = [jnp.dot(_split_hl(lax.rsqrt(ms + EPS)), gexp_ref[...], preferred_element_type=F32)
                 for ms in mss]
    for r, q, inv_full in zip(rows, qs, inv_fulls):
        qn = (q * inv_full * gq_ref[...]) * (HEAD_DIM ** -0.5)
        for h in range(n_heads):
            q_ref[0, h, r, :] = qn[:, h * HEAD_DIM:(h + 1) * HEAD_DIM].astype(BF16)
    for r, kv in zip(rows, kvs):
        for h in range(n_kv):
            kh = kv[:, h * HEAD_DIM:(h + 1) * HEAD_DIM]
            k_ref[0, h, r, :] = _rms_unit(kh) * gk_ref[...]
            v_ref[0, h, r, :] = kv[:, (n_kv + h) * HEAD_DIM:(n_kv + h + 1) * HEAD_DIM]


def _proj_b(h, ga, wq, gq, gkv, wkv, gk, gsum, gexp):
    bx, t, d = h.shape
    n_heads = wq.shape[1] // HEAD_DIM
    n_kv = wkv.shape[1] // (2 * HEAD_DIM)
    tm = min(t, 512)
    hm = lambda b, i: (b, 0, i, 0)
    ins = [ga, wq, gq, gkv, wkv, gk, gsum, gexp]
    return pl.pallas_call(
        functools.partial(_proj_b_kernel, n_heads=n_heads, n_kv=n_kv),
        grid=(bx, t // tm),
        in_specs=[pl.BlockSpec((1, tm, d), lambda b, i: (b, i, 0))] + [_resident(a.shape) for a in ins],
        out_specs=[pl.BlockSpec((1, n_heads, tm, HEAD_DIM), hm),
                   pl.BlockSpec((1, n_kv, tm, HEAD_DIM), hm),
                   pl.BlockSpec((1, n_kv, tm, HEAD_DIM), hm)],
        out_shape=[jax.ShapeDtypeStruct((bx, n_heads, t, HEAD_DIM), BF16),
                   jax.ShapeDtypeStruct((bx, n_kv, t, HEAD_DIM), F32),
                   jax.ShapeDtypeStruct((bx, n_kv, t, HEAD_DIM), F32)],
        compiler_params=_cparams(("parallel", "parallel")),
        name="proj_b",
    )(h, *ins)


def _t5_bucket(rel):
    nb = NUM_BUCKETS // 2
    max_exact = nb // 2
    n = jnp.abs(rel)
    large = max_exact + (jnp.log(jnp.maximum(n, 1).astype(jnp.float32) / max_exact)
                         / math.log(MAX_DISTANCE / max_exact) * (nb - max_exact)).astype(jnp.int32)
    large = jnp.minimum(large, nb - 1)
    return jnp.where(rel > 0, nb, 0) + jnp.where(n < max_exact, n, large)


def _bias_kernel(bucket_ref, rbt_ref, out_ref):
    bucket = bucket_ref[...].astype(F32)
    guard = 2.0 ** -10
    acc = jnp.zeros(out_ref.shape, F32)
    for b in range(NUM_BUCKETS):
        hit = (bucket >= b - guard) & (bucket < b + 1 - guard)
        acc = acc + jnp.where(hit, rbt_ref[:, b:b + 1], 0.0)
    out_ref[...] = acc


def _bias_table(rel_bias, nq, nk, key_offset):
    rel = (jnp.arange(nk, dtype=jnp.int32)[None, :] - key_offset) - jnp.arange(nq, dtype=jnp.int32)[:, None]
    bucket = _t5_bucket(rel).reshape(1, nq * nk)
    n_heads = rel_bias.shape[1]
    out = pl.pallas_call(
        _bias_kernel,
        out_shape=jax.ShapeDtypeStruct((n_heads, nq * nk), F32),
        name="bias_table",
    )(bucket, rel_bias.T)
    return out.reshape(n_heads, nq, nk)


SWA_CHUNKS_PER_TRIP = 2


def _swa_kernel(sink_ref, q_ref, k_ref, v_ref, bias_ref, o_ref, kpad_ref, vpad_ref, *, cq, wl, pad, n_chunks,
                group):
    n_kv, tk = k_ref.shape[1], k_ref.shape[2]
    if pad:
        kpad_ref[:, 0:pad, :] = jnp.zeros((n_kv, pad, HEAD_DIM), BF16)
        vpad_ref[:, 0:pad, :] = jnp.zeros((n_kv, pad, HEAD_DIM), BF16)
    kpad_ref[:, pad:pad + tk, :] = k_ref[0].astype(BF16)
    vpad_ref[:, pad:pad + tk, :] = v_ref[0].astype(BF16)

    per_trip = SWA_CHUNKS_PER_TRIP if n_chunks % SWA_CHUNKS_PER_TRIP == 0 else 1
    n_heads = n_kv * group

    def chunks(c, masked):
        r0s = [pl.multiple_of((c * per_trip + i) * cq, cq) for i in range(per_trip)]
        logits = [[lax.dot_general(
            q_ref[0, kv * group:(kv + 1) * group, pl.ds(r0, cq), :].reshape(group * cq, HEAD_DIM),
            kpad_ref[kv, pl.ds(r0, wl), :], _NT, preferred_element_type=F32)
            for kv in range(n_kv)] for r0 in r0s]
        es, dens = [], []
        for r0, lg in zip(r0s, logits):
            valid = lax.broadcasted_iota(jnp.int32, (cq, wl), 1) + r0 >= pad
            for h in range(n_heads):
                l = lg[h // group][(h % group) * cq:(h % group + 1) * cq] + bias_ref[h]
                if masked:
                    l = jnp.where(valid, l, -jnp.inf)
                sink = sink_ref[h]
                m = jnp.maximum(jnp.max(l, axis=-1, keepdims=True), sink)
                e = jnp.exp(l - m)
                dens.append(jnp.sum(e, axis=-1, keepdims=True) + jnp.exp(sink - m))
                es.append(e.astype(BF16))
        pvs = [[jnp.dot(jnp.concatenate(es[i * n_heads + kv * group:i * n_heads + (kv + 1) * group], axis=0),
                        vpad_ref[kv, pl.ds(r0, wl), :], preferred_element_type=F32)
                for kv in range(n_kv)] for i, r0 in enumerate(r0s)]
        for i, r0 in enumerate(r0s):
            outs = [pvs[i][h // group][(h % group) * cq:(h % group + 1) * cq] / dens[i * n_heads + h]
                    for h in range(n_heads)]
            o_ref[0, pl.ds(r0, cq), :] = jnp.concatenate(outs, axis=-1).astype(BF16)
        return 0

    trips = n_chunks // per_trip
    masked_trips = min(trips, -(-pad // (cq * per_trip)))
    for c in range(masked_trips):
        chunks(c, True)
    lax.fori_loop(masked_trips, trips, lambda c, _: chunks(c, False), 0)


def _swa(q, k_win, v_win, bias, sinks, *, cq, wl, pad):
    b, n_heads, tq, _ = q.shape
    n_kv, tk = k_win.shape[1], k_win.shape[2]
    n_chunks = tq // cq
    assert (n_chunks - 1) * cq + wl == pad + tk
    kv_blk = pl.BlockSpec((1, n_kv, tk, HEAD_DIM), lambda bi: (bi, 0, 0, 0))
    return pl.pallas_call(
        functools.partial(_swa_kernel, cq=cq, wl=wl, pad=pad, n_chunks=n_chunks, group=n_heads // n_kv),
        grid=(b,),
        in_specs=[pl.BlockSpec(memory_space=pltpu.SMEM),
                  pl.BlockSpec((1, n_heads, tq, HEAD_DIM), lambda bi: (bi, 0, 0, 0)),
                  kv_blk, kv_blk, _resident(bias.shape)],
        out_specs=pl.BlockSpec((1, tq, n_heads * HEAD_DIM), lambda bi: (bi, 0, 0)),
        out_shape=jax.ShapeDtypeStruct((b, tq, n_heads * HEAD_DIM), BF16),
        scratch_shapes=[pltpu.VMEM((n_kv, pad + tk, HEAD_DIM), BF16),
                        pltpu.VMEM((n_kv, pad + tk, HEAD_DIM), BF16)],
        compiler_params=_cparams(("parallel",)),
        name="swa",
    )(sinks, q, k_win, v_win, bias)


def _row(v):
    return v.reshape(1, -1).astype(F32)


def _router_hi_hi_lo(wr):
    hi = wr.astype(BF16)
    lo = (wr - hi.astype(F32)).astype(BF16)
    return jnp.concatenate([hi, hi, lo], axis=0)


def _prep_weights(prm):
    d = prm["a_w_o"].shape[1]
    depth = prm["norm_ffn"].shape[0]
    n_a = prm["a_w_qkv"].shape[0]
    scale = HEAD_DIM ** -0.5
    w = {"channel": [], "n_a": n_a, "depth": depth}
    for i in range(depth):
        wo = prm["a_w_o"][i] if i < n_a else prm["b_w_o"][i - n_a]
        pad = ROUTER_LANES - N_GROUPS - N_EXPERTS
        wr = jnp.concatenate([prm["moe_w_group"][i], prm["moe_w_router"][i].reshape(d, N_EXPERTS),
                              jnp.zeros((d, pad), F32)], axis=1)
        w["channel"].append({"wo": wo.astype(BF16), "gffn": _row(prm["norm_ffn"][i]),
                             "wr": _router_hi_hi_lo(wr), "gple": _row(prm["norm_ple"][i])})
    w["stacks"] = {"wg": prm["moe_w_gate"].astype(BF16), "wu": prm["moe_w_up"].astype(BF16),
                   "wd": prm["moe_w_down"].astype(BF16), "wpg": prm["ple_w_gate"].astype(BF16),
                   "wpp": prm["ple_w_proj"].astype(BF16)}
    w["qkv"] = []
    for i in range(n_a):
        wq = prm["a_w_qkv"][i]
        hd = wq.shape[1] // 3
        w["qkv"].append(((wq[:, :hd] * scale).astype(BF16),
                         wq[:, hd:].T.astype(BF16)))
    n_heads = prm["b_w_q"].shape[2] // HEAD_DIM
    head_of = jnp.arange(n_heads * HEAD_DIM, dtype=jnp.int32) // HEAD_DIM
    lanes = jnp.arange(ROUTER_LANES, dtype=jnp.int32)
    member = (head_of[:, None] == lanes[None, :]).astype(BF16)
    w["gsum"] = member
    w["gexp"] = jnp.concatenate([member.T, member.T], axis=0)
    w["wq_b"] = [prm["b_w_q"][j].astype(BF16) for j in range(depth - n_a)]
    w["gq_b"] = [_row(jnp.tile(prm["b_q_norm"][j], n_heads)) for j in range(depth - n_a)]
    w["wkv"] = prm["b_w_kv"].astype(BF16)
    return w


def _assert_sample_window_visible(past_len, s, tk):
    q_chunk = [(past_len + i) // CHUNK for i in range(s)]
    k_pos = [past_len + s - tk + j for j in range(tk)]
    ok = all(kp >= 0 and qc - WIN_CHUNKS <= kp // CHUNK <= qc for qc in q_chunk for kp in k_pos)
    if not ok:
        raise NotImplementedError("sample window with masked keys")


def _fold(x):
    return x.reshape(1, -1, x.shape[-1])


def _unfold_rows(a, b):
    _, n_heads, rows, dh = a.shape
    return a.reshape(n_heads, b, rows // b, dh).transpose(1, 0, 2, 3)


def _unfold_cols(a, b):
    _, n_heads, dh, rows = a.shape
    return a.reshape(n_heads, dh, b, rows // b).transpose(2, 0, 1, 3)


def _run_trunk(x, p, prm, w, sb_cache_k=None, sb_cache_v=None, swa_cache_k=None, swa_cache_v=None):
    bx, t, d = x.shape
    n_a, depth = w["n_a"], w["depth"]
    sample = sb_cache_k is not None
    h = x
    sb_k, sb_v = [], []
    k_win = v_win = None
    q_b = None
    for i in range(depth):
        if i < n_a:
            if sample:
                q, kt, vt = _proj_a(_fold(h), _row(prm["norm_attn"][i]), *w["qkv"][i])
                q, kt, vt = _unfold_rows(q, bx), _unfold_cols(kt, bx), _unfold_cols(vt, bx)
            else:
                q, kt, vt = _proj_a(h, _row(prm["norm_attn"][i]), *w["qkv"][i])
            sb_k.append(kt)
            sb_v.append(vt)
            if sample:
                o = _sb_sample(q, kt, vt, jnp.swapaxes(sb_cache_k, -1, -2), jnp.swapaxes(sb_cache_v, -1, -2), i)
            else:
                o = _sb_prompt(q, kt, vt)
        else:
            j = i - n_a
            if j > 0:
                raise NotImplementedError("one B layer supported")
            if sample:
                tk = k_win.shape[2]
                _assert_sample_window_visible(sb_cache_k.shape[3], t, tk)
                o = _swa(q_b, k_win, v_win, _bias_table(prm["rel_bias"], t, tk, tk - t),
                         prm["b_sinks"][j], cq=t, wl=tk, pad=0)
            else:
                wl = WINDOW + CHUNK
                o = _swa(q_b, k_win, v_win, _bias_table(prm["rel_bias"], CHUNK, wl, WINDOW),
                         prm["b_sinks"][j], cq=CHUNK, wl=wl, pad=WINDOW)
        h = _channel(h.reshape(bx * t, d), o.reshape(bx * t, d), p.reshape(depth, bx * t, -1), i,
                     w["channel"][i], w["stacks"]).reshape(bx, t, d)
        if i == n_a - 1:
            q_b, k_s, v_s = _proj_b(_fold(h) if sample else h, _row(prm["norm_attn"][n_a]), w["wq_b"][0],
                                    w["gq_b"][0], _row(prm["kv_norm"]), w["wkv"], _row(prm["b_k_norm"]),
                                    w["gsum"], w["gexp"])
            if sample:
                q_b, k_s, v_s = _unfold_rows(q_b, bx), _unfold_rows(k_s, bx), _unfold_rows(v_s, bx)
                k_win = jnp.concatenate([swa_cache_k, k_s], axis=2)
                v_win = jnp.concatenate([swa_cache_v, v_s], axis=2)
            else:
                k_win, v_win = k_s, v_s
    sb_k = jnp.swapaxes(jnp.stack(sb_k), -1, -2)
    sb_v = jnp.swapaxes(jnp.stack(sb_v), -1, -2)
    return h, sb_k, sb_v, k_win[:, :, -WINDOW:], v_win[:, :, -WINDOW:]


def kernel(x_prompt, x_sample, p_prompt, p_sample, cache_sb_k, cache_sb_v, cache_swa_k, cache_swa_v, norm_attn, norm_ffn, norm_ple, a_w_qkv, a_w_o, kv_norm, b_w_kv, b_k_norm, b_w_q, b_q_norm, b_sinks, b_w_o, rel_bias, moe_w_group, moe_w_router, moe_w_gate, moe_w_up, moe_w_down, ple_w_proj, ple_w_gate):
    prm = {
        "norm_attn": norm_attn, "norm_ffn": norm_ffn, "norm_ple": norm_ple,
        "a_w_qkv": a_w_qkv, "a_w_o": a_w_o, "kv_norm": kv_norm, "b_w_kv": b_w_kv, "b_k_norm": b_k_norm,
        "b_w_q": b_w_q, "b_q_norm": b_q_norm, "b_sinks": b_sinks, "b_w_o": b_w_o, "rel_bias": rel_bias,
        "moe_w_group": moe_w_group, "moe_w_router": moe_w_router, "moe_w_gate": moe_w_gate,
        "moe_w_up": moe_w_up, "moe_w_down": moe_w_down, "ple_w_proj": ple_w_proj, "ple_w_gate": ple_w_gate,
    }
    w = _prep_weights(prm)
    y_p, sb_k_p, sb_v_p, swa_k_p, swa_v_p = _run_trunk(x_prompt, p_prompt, prm, w)
    y_s, sb_k_s, sb_v_s, swa_k_s, swa_v_s = _run_trunk(x_sample, p_sample, prm, w, cache_sb_k, cache_sb_v,
                                                       cache_swa_k, cache_swa_v)
    return (y_p, y_s, sb_k_p, sb_v_p, swa_k_p, swa_v_p, sb_k_s, sb_v_s, swa_k_s, swa_v_s)
```

```python
import functools
import math

import jax
import jax.numpy as jnp
from jax import lax
from jax.experimental import pallas as pl
from jax.experimental.pallas import tpu as pltpu

F32 = jnp.float32
BF16 = jnp.bfloat16

HEAD_DIM = 64
CHUNK = 64
WINDOW = 128
WIN_CHUNKS = WINDOW // CHUNK
NUM_BUCKETS = 32
MAX_DISTANCE = 128
N_GROUPS = 4
EXPERTS_PER_GROUP = 4
N_EXPERTS = N_GROUPS * EXPERTS_PER_GROUP
EPS = 1e-6
LOG2E = 1.4426950408889634
SB_BLOCK = 128
ROUTER_LANES = 128
VMEM_LIMIT = 56 * 1024 * 1024

_NT = (((1,), (1,)), ((), ()))


def _cparams(sem):
    return pltpu.CompilerParams(dimension_semantics=sem, vmem_limit_bytes=VMEM_LIMIT)


def _rms_unit(x):
    return x * lax.rsqrt(jnp.mean(x * x, axis=-1, keepdims=True) + EPS)


def _split_hl(a):
    hi = a.astype(BF16)
    lo = (a - hi.astype(F32)).astype(BF16)
    return jnp.concatenate([hi, lo], axis=-1)


def _resident(shape):
    nd = len(shape)
    return pl.BlockSpec(shape, lambda *_: (0,) * nd, pipeline_mode=pl.Buffered(1))


def _resident_layer(shape, layer):
    nd = len(shape)
    return pl.BlockSpec((1,) + tuple(shape[1:]), lambda *_: (layer,) + (0,) * (nd - 1),
                        pipeline_mode=pl.Buffered(1))


def _proj_a_kernel(x_ref, g_ref, wq_ref, wkvt_ref, q_ref, kt_ref, vt_ref, *, n_heads):
    xn = (_rms_unit(x_ref[0]) * g_ref[...]).astype(BF16)
    heads_per_dot = 4
    width = heads_per_dot * HEAD_DIM
    for c in range(n_heads // heads_per_dot):
        r = jnp.dot(xn, wq_ref[:, c * width:(c + 1) * width], preferred_element_type=F32)
        for hh in range(heads_per_dot):
            q_ref[0, c * heads_per_dot + hh] = r[:, hh * HEAD_DIM:(hh + 1) * HEAD_DIM].astype(BF16)
    for c in range(2 * n_heads // heads_per_dot):
        r = lax.dot_general(wkvt_ref[c * width:(c + 1) * width, :], xn, _NT, preferred_element_type=F32)
        for hh in range(heads_per_dot):
            h = c * heads_per_dot + hh
            piece = r[hh * HEAD_DIM:(hh + 1) * HEAD_DIM, :]
            if h < n_heads:
                kt_ref[0, h] = piece
            else:
                vt_ref[0, h - n_heads] = piece


def _proj_a(x, g, wq, wkvt):
    bx, t, d = x.shape
    n_heads = wq.shape[1] // HEAD_DIM
    tm = min(t, 256)
    return pl.pallas_call(
        functools.partial(_proj_a_kernel, n_heads=n_heads),
        grid=(bx, t // tm),
        in_specs=[pl.BlockSpec((1, tm, d), lambda b, i: (b, i, 0)),
                  _resident((1, d)), _resident(wq.shape), _resident(wkvt.shape)],
        out_specs=[pl.BlockSpec((1, n_heads, tm, HEAD_DIM), lambda b, i: (b, 0, i, 0)),
                   pl.BlockSpec((1, n_heads, HEAD_DIM, tm), lambda b, i: (b, 0, 0, i)),
                   pl.BlockSpec((1, n_heads, HEAD_DIM, tm), lambda b, i: (b, 0, 0, i))],
        out_shape=[jax.ShapeDtypeStruct((bx, n_heads, t, HEAD_DIM), BF16),
                   jax.ShapeDtypeStruct((bx, n_heads, HEAD_DIM, t), F32),
                   jax.ShapeDtypeStruct((bx, n_heads, HEAD_DIM, t), F32)],
        compiler_params=_cparams(("parallel", "parallel")),
        name="proj_a",
    )(x, g, wq, wkvt)


SB_DEAD = -104.0
SB_WINDOW_BLOCKS = 3
SB_SAMPLE_WINDOW = 256


def _suffix_neg_ones(n):
    r = lax.broadcasted_iota(jnp.int32, (n, n), 0)
    c = lax.broadcasted_iota(jnp.int32, (n, n), 1)
    return jnp.where(r >= c, -1.0, 0.0).astype(BF16)


def _strict_mask(n):
    row = lax.broadcasted_iota(jnp.int32, (n, n), 0)
    col = lax.broadcasted_iota(jnp.int32, (n, n), 1)
    return col < row


def _softplus(z):
    return jnp.maximum(z, 0.0) + jnp.log(1.0 + jnp.exp2(jnp.abs(z) * -LOG2E))


def _sb_strips(qs, kts, vts, carries, accs, uu, masks):
    n = len(qs)
    blk = uu.shape[1]
    nb = kts[0].shape[1] // blk
    zs = [jnp.dot(qs[i], kts[i], preferred_element_type=F32) for i in range(n)]
    sps = [_softplus(z) for z in zs]
    ws = [[None] * nb for _ in range(n)]
    carries = list(carries)
    for b in reversed(range(nb)):
        sl = slice(b * blk, (b + 1) * blk)
        spbs = [sp[:, sl] if masks[b] is None else jnp.where(masks[b], sp[:, sl], 0.0) for sp in sps]
        sufs = [jnp.dot(spb.astype(BF16), uu, preferred_element_type=F32) for spb in spbs]
        for i in range(n):
            wb = jnp.exp(zs[i][:, sl] + sufs[i] + carries[i])
            if masks[b] is not None:
                wb = jnp.where(masks[b], wb, 0.0)
            ws[i][b] = wb.astype(BF16)
            carries[i] = carries[i] + sufs[i][:, 0:1]
    wcat = [w[0] if nb == 1 else jnp.concatenate(w, axis=-1) for w in ws]
    accs = [accs[i] + lax.dot_general(wcat[i], vts[i], _NT, preferred_element_type=F32) for i in range(n)]
    return carries, accs


SB_STAGES = 5
SB_OLD_ROWS = 64


def _sb_prompt_fast_kernel(q_ref, kt_ref, vt_ref, o_ref, flag_ref, z_ref, sp_ref, suf_ref, w_ref, *, heads, groups):
    t = q_ref.shape[2]
    blk = SB_BLOCK
    old = SB_OLD_ROWS
    nq = t // blk
    win = SB_WINDOW_BLOCKS
    wl = win * blk
    ring = z_ref.shape[0]
    first = win - 1
    per_group = nq - first
    units = groups * per_group
    width = heads * HEAD_DIM
    uu = _suffix_neg_ones(blk)
    strict = _strict_mask(blk)
    z_ref[...] = jnp.zeros(z_ref.shape, F32)
    sp_ref[...] = jnp.zeros(sp_ref.shape, BF16)
    suf_ref[...] = jnp.zeros(suf_ref.shape, F32)
    w_ref[...] = jnp.zeros(w_ref.shape, BF16)

    hs = range(groups * heads)
    for i in range(first):
        _, accs = _sb_strips([q_ref[0, h, i * blk:(i + 1) * blk, :] for h in hs],
                             [kt_ref[0, h, :, 0:(i + 1) * blk].astype(BF16) for h in hs],
                             [vt_ref[0, h, :, 0:(i + 1) * blk].astype(BF16) for h in hs],
                             [jnp.zeros((blk, 1), F32)] * len(hs), [jnp.zeros((blk, HEAD_DIM), F32)] * len(hs),
                             uu, [None] * i + [strict])
        o_ref[0, i * blk:(i + 1) * blk, :] = jnp.concatenate(accs, axis=-1).astype(BF16)

    def where_is(n):
        u = jnp.clip(n, 0, units - 1)
        g = u // per_group
        i = first + u - g * per_group
        q0 = pl.multiple_of(i * blk, blk)
        k0 = pl.multiple_of((i - first) * blk, blk)
        return g * heads, q0, k0, lax.rem(n + ring * SB_STAGES, ring)

    def body(n, worst):
        h0, q0, k0, _ = where_is(n - 4)
        accs = []
        for h in range(heads):
            vt = vt_ref[0, h0 + h, :, pl.ds(k0, wl)].astype(BF16)
            accs.append(lax.dot_general(w_ref[h], vt, _NT, preferred_element_type=F32))
        o_ref[0, pl.ds(q0, blk), pl.ds(pl.multiple_of(h0 * HEAD_DIM, width), width)] = (
            jnp.concatenate(accs, axis=-1).astype(BF16))

        _, _, _, slot = where_is(n - 3)
        counts = n - 3 >= 0
        for h in range(heads):
            s_old, s_prev, s_own = suf_ref[h, 0:old], suf_ref[h, old:old + blk], suf_ref[h, old + blk:]
            w_own = jnp.where(strict, jnp.exp(z_ref[slot, h, :, 2 * blk:] + s_own), 0.0)
            carry = s_own[:, 0:1]
            w_prev = jnp.exp(z_ref[slot, h, :, blk:2 * blk] + s_prev + carry)
            carry = carry + s_prev[:, 0:1]
            w_old = jnp.exp(z_ref[slot, h, 0:old, 0:blk] + s_old + carry[0:old])
            w_ref[h, :, 2 * blk:] = w_own.astype(BF16)
            w_ref[h, :, blk:2 * blk] = w_prev.astype(BF16)
            w_ref[h, 0:old, 0:blk] = w_old.astype(BF16)
            edge = jnp.concatenate([carry[0:old] + s_old[:, 0:1], carry[old:]], axis=0)
            worst = jnp.maximum(worst, jnp.where(counts, edge, -jnp.inf))

        for h in range(heads):
            suf_ref[h] = jnp.dot(sp_ref[h], uu, preferred_element_type=F32)

        _, _, _, slot = where_is(n - 1)
        for h in range(heads):
            sp_new = _softplus(z_ref[slot, h, :, blk:])
            sp_ref[h, 0:old] = _softplus(z_ref[slot, h, 0:old, 0:blk]).astype(BF16)
            sp_ref[h, old:old + blk] = sp_new[:, 0:blk].astype(BF16)
            sp_ref[h, old + blk:] = jnp.where(strict, sp_new[:, blk:], 0.0).astype(BF16)

        h0, q0, k0, slot = where_is(n)
        for h in range(heads):
            q = q_ref[0, h0 + h, pl.ds(q0, blk), :]
            kt = kt_ref[0, h0 + h, :, pl.ds(k0, wl)].astype(BF16)
            z_ref[slot, h, :, blk:] = jnp.dot(q, kt[:, blk:], preferred_element_type=F32)
            z_ref[slot, h, 0:old, 0:blk] = jnp.dot(q[0:old], kt[:, 0:blk], preferred_element_type=F32)
        return worst

    worst = lax.fori_loop(0, units + SB_STAGES - 1, body, jnp.full((blk, 1), -jnp.inf, F32))
    flag_ref[...] = jnp.broadcast_to(jnp.max(worst), flag_ref.shape)


def _sb_prompt_full_kernel(q_ref, kt_ref, vt_ref, o_ref, *, heads):
    t = q_ref.shape[2]
    blk = SB_BLOCK
    uu = _suffix_neg_ones(blk)
    strict = _strict_mask(blk)

    def q_block(i, _):
        q0 = pl.multiple_of(i * blk, blk)
        qs = [q_ref[0, h, pl.ds(q0, blk), :] for h in range(heads)]

        def strips(k0, carries, accs, masks):
            return _sb_strips(qs, [kt_ref[0, h, :, pl.ds(k0, blk)].astype(BF16) for h in range(heads)],
                              [vt_ref[0, h, :, pl.ds(k0, blk)].astype(BF16) for h in range(heads)],
                              carries, accs, uu, masks)

        state = strips(q0, [jnp.zeros((blk, 1), F32)] * heads, [jnp.zeros((blk, HEAD_DIM), F32)] * heads,
                       [strict])

        def k_block(jj, st):
            carries, accs = strips(pl.multiple_of((i - 1 - jj) * blk, blk), st[0], st[1], [None])
            return tuple(carries), tuple(accs)

        _, accs = lax.fori_loop(0, i, k_block, (tuple(state[0]), tuple(state[1])))
        o_ref[0, pl.ds(q0, blk), :] = jnp.concatenate(list(accs), axis=-1).astype(BF16)
        return 0

    lax.fori_loop(0, t // blk, q_block, 0)


def _sb_prompt(q, kt, vt):
    b, n_heads, t, _ = q.shape
    wl = SB_WINDOW_BLOCKS * SB_BLOCK
    assert t >= wl
    o_shape = jax.ShapeDtypeStruct((b, t, n_heads * HEAD_DIM), BF16)

    def specs(hb):
        q_blk = pl.BlockSpec((1, hb, t, HEAD_DIM), lambda bi, hg: (bi, hg, 0, 0))
        kv_blk = pl.BlockSpec((1, hb, HEAD_DIM, t), lambda bi, hg: (bi, hg, 0, 0))
        o_blk = pl.BlockSpec((1, t, hb * HEAD_DIM), lambda bi, hg: (bi, 0, hg))
        return (b, n_heads // hb), [q_blk, kv_blk, kv_blk], o_blk

    heads, groups = 4, 2
    rows = SB_OLD_ROWS + 2 * SB_BLOCK
    grid, in_specs, o_blk = specs(heads * groups)
    o_fast, flags = pl.pallas_call(
        functools.partial(_sb_prompt_fast_kernel, heads=heads, groups=groups),
        grid=grid,
        in_specs=in_specs,
        out_specs=[o_blk, pl.BlockSpec((1, 1, 8, 128), lambda bi, hg: (bi, hg, 0, 0))],
        out_shape=[o_shape, jax.ShapeDtypeStruct((b, grid[1], 8, 128), F32)],
        scratch_shapes=[pltpu.VMEM((SB_STAGES - 1, heads, SB_BLOCK, wl), F32),
                        pltpu.VMEM((heads, rows, SB_BLOCK), BF16),
                        pltpu.VMEM((heads, rows, SB_BLOCK), F32),
                        pltpu.VMEM((heads, SB_BLOCK, wl), BF16)],
        compiler_params=_cparams(("parallel", "parallel")),
        name="sb_prompt_fast",
    )(q, kt, vt)

    def full():
        grid, in_specs, o_blk = specs(4)
        return pl.pallas_call(
            functools.partial(_sb_prompt_full_kernel, heads=4),
            grid=grid,
            in_specs=in_specs,
            out_specs=o_blk,
            out_shape=o_shape,
            compiler_params=_cparams(("parallel", "parallel")),
            name="sb_prompt_full",
        )(q, kt, vt)

    return lax.cond(jnp.max(flags) > SB_DEAD, full, lambda: o_fast)


def _sb_sample_kernel(q_ref, ktn_ref, vtn_ref, ktc_ref, vtc_ref, o_ref, *maybe_flag, heads, strip):
    s = q_ref.shape[2]
    width = ktc_ref.shape[4]
    blk = SB_BLOCK
    uu = _suffix_neg_ones(blk)
    uu_new = _suffix_neg_ones(s)
    strict = _strict_mask(s)

    qs = [q_ref[0, h] for h in range(heads)]
    state = _sb_strips(qs, [ktn_ref[0, h].astype(BF16) for h in range(heads)],
                       [vtn_ref[0, h].astype(BF16) for h in range(heads)],
                       [jnp.zeros((s, 1), F32)] * heads, [jnp.zeros((s, HEAD_DIM), F32)] * heads,
                       uu_new, [strict])

    def k_strip(jj, st):
        k0 = pl.multiple_of(width - (jj + 1) * strip * blk, blk)
        carries, accs = _sb_strips(
            qs, [ktc_ref[0, 0, h, :, pl.ds(k0, strip * blk)].astype(BF16) for h in range(heads)],
            [vtc_ref[0, 0, h, :, pl.ds(k0, strip * blk)].astype(BF16) for h in range(heads)],
            st[0], st[1], uu, [None] * strip)
        return tuple(carries), tuple(accs)

    carries, accs = lax.fori_loop(0, width // (strip * blk), k_strip, (tuple(state[0]), tuple(state[1])))
    o_ref[0] = jnp.concatenate(list(accs), axis=-1).astype(BF16)
    if maybe_flag:
        worst = carries[0]
        for h in range(1, heads):
            worst = jnp.maximum(worst, carries[h])
        maybe_flag[0][...] = jnp.broadcast_to(jnp.max(worst), maybe_flag[0].shape)


def _sb_sample(q, kt_new, vt_new, cache_kt, cache_vt, layer):
    b, n_heads, s, _ = q.shape
    past = cache_kt.shape[4]
    width = min(past, SB_SAMPLE_WINDOW)
    assert past % width == 0 and width % SB_BLOCK == 0
    last = past // width - 1
    o_shape = jax.ShapeDtypeStruct((b, s, n_heads * HEAD_DIM), BF16)

    def specs(heads, cache_width, cache_block):
        q_blk = pl.BlockSpec((1, heads, s, HEAD_DIM), lambda bi, hg: (bi, hg, 0, 0))
        new_blk = pl.BlockSpec((1, heads, HEAD_DIM, s), lambda bi, hg: (bi, hg, 0, 0))
        cache_blk = pl.BlockSpec((1, 1, heads, HEAD_DIM, cache_width),
                                 lambda bi, hg: (layer, bi, hg, 0, cache_block))
        o_blk = pl.BlockSpec((1, s, heads * HEAD_DIM), lambda bi, hg: (bi, 0, hg))
        return (b, n_heads // heads), [q_blk, new_blk, new_blk, cache_blk, cache_blk], o_blk

    heads = n_heads
    grid, in_specs, o_blk = specs(heads, width, last)
    o_fast, flags = pl.pallas_call(
        functools.partial(_sb_sample_kernel, heads=heads, strip=width // SB_BLOCK),
        grid=grid,
        in_specs=in_specs,
        out_specs=[o_blk, pl.BlockSpec((1, 1, 8, 128), lambda bi, hg: (bi, hg, 0, 0))],
        out_shape=[o_shape, jax.ShapeDtypeStruct((b, n_heads // heads, 8, 128), F32)],
        compiler_params=_cparams(("parallel", "parallel")),
        name="sb_sample_fast",
    )(q, kt_new, vt_new, cache_kt, cache_vt)
    if width == past:
        return o_fast

    def full():
        heads = 4
        grid, in_specs, o_blk = specs(heads, past, 0)
        return pl.pallas_call(
            functools.partial(_sb_sample_kernel, heads=heads, strip=1),
            grid=grid,
            in_specs=in_specs,
            out_specs=o_blk,
            out_shape=o_shape,
            compiler_params=_cparams(("parallel", "parallel")),
            name="sb_sample_full",
        )(q, kt_new, vt_new, cache_kt, cache_vt)

    return lax.cond(jnp.max(flags) > SB_DEAD, full, lambda: o_fast)


def _route(logits):
    tm = logits.shape[0]
    lane = lax.broadcasted_iota(jnp.int32, (tm, ROUTER_LANES), 1)
    lane_f = lane.astype(F32)
    neg = -jnp.inf
    first = lambda hit: jnp.min(jnp.where(hit, lane_f, float(ROUTER_LANES)), axis=-1, keepdims=True)

    gl = jnp.where(lane < N_GROUPS, logits, neg)
    g_max = jnp.max(gl, axis=-1, keepdims=True)
    g_top = 1.0 / jnp.sum(jnp.exp(gl - g_max), axis=-1, keepdims=True)
    g_idx = first(gl == g_max)

    lo = N_GROUPS + g_idx * EXPERTS_PER_GROUP
    in_group = (lane_f >= lo) & (lane_f < lo + EXPERTS_PER_GROUP)
    sel = jnp.where(in_group, logits, neg)
    t1 = jnp.max(sel, axis=-1, keepdims=True)
    i1 = first(sel == t1)
    sel2 = jnp.where(lane_f == i1, neg, sel)
    t2 = jnp.max(sel2, axis=-1, keepdims=True)
    i2 = first(sel2 == t2)
    e2 = jnp.exp(t2 - t1)
    den = 1.0 + e2
    w1 = (1.0 / den) * g_top
    w2 = (e2 / den) * g_top
    return jnp.where(lane_f == i1, w1, 0.0) + jnp.where(lane_f == i2, w2, 0.0)


CHANNEL_ROW_SPLITS = 2
EXPERTS_PER_TRIP = 16


def _channel_kernel(x_ref, o_ref, p_ref, wo_ref, gffn_ref, wr_ref, wg_ref, wu_ref, wd_ref, gple_ref,
                    wpg_ref, wpp_ref, out_ref, acc_ref):
    tm = x_ref.shape[0]
    d_expert = wd_ref.shape[2]
    half = tm // CHANNEL_ROW_SPLITS
    rows = [slice(i * half, (i + 1) * half) for i in range(CHANNEL_ROW_SPLITS)]

    h1s = [x_ref[r, :] + jnp.dot(o_ref[r, :], wo_ref[...], preferred_element_type=F32) for r in rows]
    xns = [_rms_unit(h1) * gffn_ref[...] for h1 in h1s]
    xnbs = [xn.astype(BF16) for xn in xns]
    xlos = [(xn - xnb.astype(F32)).astype(BF16) for xn, xnb in zip(xns, xnbs)]
    logits = []
    for xnb, xlo in zip(xnbs, xlos):
        hi_lo = jnp.dot(xnb, wr_ref[...], preferred_element_type=F32)
        lo_hi = jnp.dot(xlo, wr_ref[:, 0:ROUTER_LANES], preferred_element_type=F32)
        logits.append(hi_lo[:, 0:ROUTER_LANES] + hi_lo[:, ROUTER_LANES:] + lo_hi)
    gates = jnp.concatenate([_route(l) for l in logits], axis=0)
    xnb = jnp.concatenate(xnbs, axis=0)
    lane = lax.broadcasted_iota(jnp.int32, gates.shape, 1)
    for r, h1 in zip(rows, h1s):
        acc_ref[r, :] = h1

    def experts(j, _):
        es = [j * EXPERTS_PER_TRIP + k for k in range(EXPERTS_PER_TRIP)]
        gs = [jnp.dot(xnb, wg_ref[0, e], preferred_element_type=F32) for e in es]
        us = [jnp.dot(xnb, wu_ref[0, e], preferred_element_type=F32) for e in es]
        hids = []
        for e, g, u in zip(es, gs, us):
            gate_e = jnp.sum(jnp.where(lane == N_GROUPS + e, gates, 0.0), axis=-1, keepdims=True)
            hids.append(((g * jax.nn.sigmoid(g)) * u * gate_e).astype(BF16))
        wd = wd_ref[0, pl.ds(j * EXPERTS_PER_TRIP, EXPERTS_PER_TRIP)]
        acc_ref[...] += jnp.dot(jnp.concatenate(hids, axis=-1),
                                wd.reshape(EXPERTS_PER_TRIP * d_expert, wd.shape[-1]),
                                preferred_element_type=F32)
        return 0

    lax.fori_loop(0, N_EXPERTS // EXPERTS_PER_TRIP, experts, 0)
    h2s = [acc_ref[r, :] for r in rows]
    x3s = [(_rms_unit(h2) * gple_ref[...]).astype(BF16) for h2 in h2s]
    ple_gates = [jax.nn.sigmoid(jnp.dot(x3, wpg_ref[0], preferred_element_type=F32)) for x3 in x3s]
    projs = [jnp.dot(p_ref[0, r, :].astype(BF16), wpp_ref[0], preferred_element_type=F32) for r in rows]
    for r, h2, proj, gate in zip(rows, h2s, projs, ple_gates):
        out_ref[r, :] = h2 + proj * gate


def _channel(x, o, p, layer, w, stacks):
    n, d = x.shape
    tm = min(n, 512)
    row = lambda cols: pl.BlockSpec((tm, cols), lambda i: (i, 0))
    small = [w["wo"], w["gffn"], w["wr"]]
    moe = [stacks["wg"], stacks["wu"], stacks["wd"]]
    ple = [stacks["wpg"], stacks["wpp"]]
    return pl.pallas_call(
        _channel_kernel,
        grid=(n // tm,),
        in_specs=([row(d), row(d), pl.BlockSpec((1, tm, p.shape[2]), lambda i: (layer, i, 0))]
                  + [_resident(a.shape) for a in small] + [_resident_layer(a.shape, layer) for a in moe]
                  + [_resident(w["gple"].shape)] + [_resident_layer(a.shape, layer) for a in ple]),
        out_specs=row(d),
        out_shape=jax.ShapeDtypeStruct((n, d), F32),
        scratch_shapes=[pltpu.VMEM((tm, d), F32)],
        compiler_params=_cparams(("parallel",)),
        name="channel",
    )(x, o, p, *small, *moe, w["gple"], *ple)


PROJ_B_ROW_SPLITS = 2


def _proj_b_kernel(h_ref, ga_ref, wq_ref, gq_ref, gkv_ref, wkv_ref, gk_ref, gsum_ref, gexp_ref,
                   q_ref, k_ref, v_ref, *, n_heads, n_kv):
    tm = h_ref.shape[1]
    splits = PROJ_B_ROW_SPLITS if tm % (16 * PROJ_B_ROW_SPLITS) == 0 else 1
    rows = [slice(i * tm // splits, (i + 1) * tm // splits) for i in range(splits)]
    ys = [_rms_unit(h_ref[0, r, :]) for r in rows]
    qs = [jnp.dot((y * ga_ref[...]).astype(BF16), wq_ref[...], preferred_element_type=F32) for y in ys]
    kvs = [jnp.dot((y * gkv_ref[...]).astype(BF16), wkv_ref[...], preferred_element_type=F32) for y in ys]
    mss = [jnp.dot((q * q).astype(BF16), gsum_ref[...], preferred_element_type=F32) * (1.0 / HEAD_DIM)
           for q in qs]
    inv_fulls = [jnp.dot(_split_hl(lax.rsqrt(ms + EPS)), gexp_ref[...], preferred_element_type=F32)
                 for ms in mss]
    for r, q, inv_full in zip(rows, qs, inv_fulls):
        qn = (q * inv_full * gq_ref[...]) * (HEAD_DIM ** -0.5)
        for h in range(n_heads):
            q_ref[0, h, r, :] = qn[:, h * HEAD_DIM:(h + 1) * HEAD_DIM].astype(BF16)
    for r, kv in zip(rows, kvs):
        for h in range(n_kv):
            kh = kv[:, h * HEAD_DIM:(h + 1) * HEAD_DIM]
            k_ref[0, h, r, :] = _rms_unit(kh) * gk_ref[...]
            v_ref[0, h, r, :] = kv[:, (n_kv + h) * HEAD_DIM:(n_kv + h + 1) * HEAD_DIM]


def _proj_b(h, ga, wq, gq, gkv, wkv, gk, gsum, gexp):
    bx, t, d = h.shape
    n_heads = wq.shape[1] // HEAD_DIM
    n_kv = wkv.shape[1] // (2 * HEAD_DIM)
    tm = min(t, 512)
    hm = lambda b, i: (b, 0, i, 0)
    ins = [ga, wq, gq, gkv, wkv, gk, gsum, gexp]
    return pl.pallas_call(
        functools.partial(_proj_b_kernel, n_heads=n_heads, n_kv=n_kv),
        grid=(bx, t // tm),
        in_specs=[pl.BlockSpec((1, tm, d), lambda b, i: (b, i, 0))] + [_resident(a.shape) for a in ins],
        out_specs=[pl.BlockSpec((1, n_heads, tm, HEAD_DIM), hm),
                   pl.BlockSpec((1, n_kv, tm, HEAD_DIM), hm),
                   pl.BlockSpec((1, n_kv, tm, HEAD_DIM), hm)],
        out_shape=[jax.ShapeDtypeStruct((bx, n_heads, t, HEAD_DIM), BF16),
                   jax.ShapeDtypeStruct((bx, n_kv, t, HEAD_DIM), F32),
                   jax.ShapeDtypeStruct((bx, n_kv, t, HEAD_DIM), F32)],
        compiler_params=_cparams(("parallel", "parallel")),
        name="proj_b",
    )(h, *ins)


def _t5_bucket(rel):
    nb = NUM_BUCKETS // 2
    max_exact = nb // 2
    n = jnp.abs(rel)
    large = max_exact + (jnp.log(jnp.maximum(n, 1).astype(jnp.float32) / max_exact)
                         / math.log(MAX_DISTANCE / max_exact) * (nb - max_exact)).astype(jnp.int32)
    large = jnp.minimum(large, nb - 1)
    return jnp.where(rel > 0, nb, 0) + jnp.where(n < max_exact, n, large)


def _bias_kernel(bucket_ref, rbt_ref, out_ref):
    bucket = bucket_ref[...].astype(F32)
    guard = 2.0 ** -10
    acc = jnp.zeros(out_ref.shape, F32)
    for b in range(NUM_BUCKETS):
        hit = (bucket >= b - guard) & (bucket < b + 1 - guard)
        acc = acc + jnp.where(hit, rbt_ref[:, b:b + 1], 0.0)
    out_ref[...] = acc


def _bias_table(rel_bias, nq, nk, key_offset):
    rel = (jnp.arange(nk, dtype=jnp.int32)[None, :] - key_offset) - jnp.arange(nq, dtype=jnp.int32)[:, None]
    bucket = _t5_bucket(rel).reshape(1, nq * nk)
    n_heads = rel_bias.shape[1]
    out = pl.pallas_call(
        _bias_kernel,
        out_shape=jax.ShapeDtypeStruct((n_heads, nq * nk), F32),
        name="bias_table",
    )(bucket, rel_bias.T)
    return out.reshape(n_heads, nq, nk)


SWA_CHUNKS_PER_TRIP = 2


def _swa_kernel(sink_ref, q_ref, k_ref, v_ref, bias_ref, o_ref, kpad_ref, vpad_ref, *, cq, wl, pad, n_chunks,
                group):
    n_kv, tk = k_ref.shape[1], k_ref.shape[2]
    if pad:
        kpad_ref[:, 0:pad, :] = jnp.zeros((n_kv, pad, HEAD_DIM), BF16)
        vpad_ref[:, 0:pad, :] = jnp.zeros((n_kv, pad, HEAD_DIM), BF16)
    kpad_ref[:, pad:pad + tk, :] = k_ref[0].astype(BF16)
    vpad_ref[:, pad:pad + tk, :] = v_ref[0].astype(BF16)

    per_trip = SWA_CHUNKS_PER_TRIP if n_chunks % SWA_CHUNKS_PER_TRIP == 0 else 1
    n_heads = n_kv * group

    def chunks(c, masked):
        r0s = [pl.multiple_of((c * per_trip + i) * cq, cq) for i in range(per_trip)]
        logits = [[lax.dot_general(
            q_ref[0, kv * group:(kv + 1) * group, pl.ds(r0, cq), :].reshape(group * cq, HEAD_DIM),
            kpad_ref[kv, pl.ds(r0, wl), :], _NT, preferred_element_type=F32)
            for kv in range(n_kv)] for r0 in r0s]
        es, dens = [], []
        for r0, lg in zip(r0s, logits):
            valid = lax.broadcasted_iota(jnp.int32, (cq, wl), 1) + r0 >= pad
            for h in range(n_heads):
                l = lg[h // group][(h % group) * cq:(h % group + 1) * cq] + bias_ref[h]
                if masked:
                    l = jnp.where(valid, l, -jnp.inf)
                sink = sink_ref[h]
                m = jnp.maximum(jnp.max(l, axis=-1, keepdims=True), sink)
                e = jnp.exp(l - m)
                dens.append(jnp.sum(e, axis=-1, keepdims=True) + jnp.exp(sink - m))
                es.append(e.astype(BF16))
        pvs = [[jnp.dot(jnp.concatenate(es[i * n_heads + kv * group:i * n_heads + (kv + 1) * group], axis=0),
                        vpad_ref[kv, pl.ds(r0, wl), :], preferred_element_type=F32)
                for kv in range(n_kv)] for i, r0 in enumerate(r0s)]
        for i, r0 in enumerate(r0s):
            outs = [pvs[i][h // group][(h % group) * cq:(h % group + 1) * cq] / dens[i * n_heads + h]
                    for h in range(n_heads)]
            o_ref[0, pl.ds(r0, cq), :] = jnp.concatenate(outs, axis=-1).astype(BF16)
        return 0

    trips = n_chunks // per_trip
    masked_trips = min(trips, -(-pad // (cq * per_trip)))
    for c in range(masked_trips):
        chunks(c, True)
    lax.fori_loop(masked_trips, trips, lambda c, _: chunks(c, False), 0)


def _swa(q, k_win, v_win, bias, sinks, *, cq, wl, pad):
    b, n_heads, tq, _ = q.shape
    n_kv, tk = k_win.shape[1], k_win.shape[2]
    n_chunks = tq // cq
    assert (n_chunks - 1) * cq + wl == pad + tk
    kv_blk = pl.BlockSpec((1, n_kv, tk, HEAD_DIM), lambda bi: (bi, 0, 0, 0))
    return pl.pallas_call(
        functools.partial(_swa_kernel, cq=cq, wl=wl, pad=pad, n_chunks=n_chunks, group=n_heads // n_kv),
        grid=(b,),
        in_specs=[pl.BlockSpec(memory_space=pltpu.SMEM),
                  pl.BlockSpec((1, n_heads, tq, HEAD_DIM), lambda bi: (bi, 0, 0, 0)),
                  kv_blk, kv_blk, _resident(bias.shape)],
        out_specs=pl.BlockSpec((1, tq, n_heads * HEAD_DIM), lambda bi: (bi, 0, 0)),
        out_shape=jax.ShapeDtypeStruct((b, tq, n_heads * HEAD_DIM), BF16),
        scratch_shapes=[pltpu.VMEM((n_kv, pad + tk, HEAD_DIM), BF16),
                        pltpu.VMEM((n_kv, pad + tk, HEAD_DIM), BF16)],
        compiler_params=_cparams(("parallel",)),
        name="swa",
    )(sinks, q, k_win, v_win, bias)


def _row(v):
    return v.reshape(1, -1).astype(F32)


def _router_hi_lo(wr):
    hi = wr.astype(BF16)
    lo = (wr - hi.astype(F32)).astype(BF16)
    return jnp.concatenate([hi, lo], axis=1)


def _prep_weights(prm):
    d = prm["a_w_o"].shape[1]
    depth = prm["norm_ffn"].shape[0]
    n_a = prm["a_w_qkv"].shape[0]
    scale = HEAD_DIM ** -0.5
    w = {"channel": [], "n_a": n_a, "depth": depth}
    for i in range(depth):
        wo = prm["a_w_o"][i] if i < n_a else prm["b_w_o"][i - n_a]
        pad = ROUTER_LANES - N_GROUPS - N_EXPERTS
        wr = jnp.concatenate([prm["moe_w_group"][i], prm["moe_w_router"][i].reshape(d, N_EXPERTS),
                              jnp.zeros((d, pad), F32)], axis=1)
        w["channel"].append({"wo": wo.astype(BF16), "gffn": _row(prm["norm_ffn"][i]),
                             "wr": _router_hi_lo(wr), "gple": _row(prm["norm_ple"][i])})
    w["stacks"] = {"wg": prm["moe_w_gate"].astype(BF16), "wu": prm["moe_w_up"].astype(BF16),
                   "wd": prm["moe_w_down"].astype(BF16), "wpg": prm["ple_w_gate"].astype(BF16),
                   "wpp": prm["ple_w_proj"].astype(BF16)}
    w["qkv"] = []
    for i in range(n_a):
        wq = prm["a_w_qkv"][i]
        hd = wq.shape[1] // 3
        w["qkv"].append(((wq[:, :hd] * scale).astype(BF16),
                         wq[:, hd:].T.astype(BF16)))
    n_heads = prm["b_w_q"].shape[2] // HEAD_DIM
    head_of = jnp.arange(n_heads * HEAD_DIM, dtype=jnp.int32) // HEAD_DIM
    lanes = jnp.arange(ROUTER_LANES, dtype=jnp.int32)
    member = (head_of[:, None] == lanes[None, :]).astype(BF16)
    w["gsum"] = member
    w["gexp"] = jnp.concatenate([member.T, member.T], axis=0)
    w["wq_b"] = [prm["b_w_q"][j].astype(BF16) for j in range(depth - n_a)]
    w["gq_b"] = [_row(jnp.tile(prm["b_q_norm"][j], n_heads)) for j in range(depth - n_a)]
    w["wkv"] = prm["b_w_kv"].astype(BF16)
    return w


def _assert_sample_window_visible(past_len, s, tk):
    q_chunk = [(past_len + i) // CHUNK for i in range(s)]
    k_pos = [past_len + s - tk + j for j in range(tk)]
    ok = all(kp >= 0 and qc - WIN_CHUNKS <= kp // CHUNK <= qc for qc in q_chunk for kp in k_pos)
    if not ok:
        raise NotImplementedError("sample window with masked keys")


def _fold(x):
    return x.reshape(1, -1, x.shape[-1])


def _unfold_rows(a, b):
    _, n_heads, rows, dh = a.shape
    return a.reshape(n_heads, b, rows // b, dh).transpose(1, 0, 2, 3)


def _unfold_cols(a, b):
    _, n_heads, dh, rows = a.shape
    return a.reshape(n_heads, dh, b, rows // b).transpose(2, 0, 1, 3)


def _run_trunk(x, p, prm, w, sb_cache_k=None, sb_cache_v=None, swa_cache_k=None, swa_cache_v=None):
    bx, t, d = x.shape
    n_a, depth = w["n_a"], w["depth"]
    sample = sb_cache_k is not None
    h = x
    sb_k, sb_v = [], []
    k_win = v_win = None
    q_b = None
    for i in range(depth):
        if i < n_a:
            if sample:
                q, kt, vt = _proj_a(_fold(h), _row(prm["norm_attn"][i]), *w["qkv"][i])
                q, kt, vt = _unfold_rows(q, bx), _unfold_cols(kt, bx), _unfold_cols(vt, bx)
            else:
                q, kt, vt = _proj_a(h, _row(prm["norm_attn"][i]), *w["qkv"][i])
            sb_k.append(kt)
            sb_v.append(vt)
            if sample:
                o = _sb_sample(q, kt, vt, jnp.swapaxes(sb_cache_k, -1, -2), jnp.swapaxes(sb_cache_v, -1, -2), i)
            else:
                o = _sb_prompt(q, kt, vt)
        else:
            j = i - n_a
            if j > 0:
                raise NotImplementedError("one B layer supported")
            if sample:
                tk = k_win.shape[2]
                _assert_sample_window_visible(sb_cache_k.shape[3], t, tk)
                o = _swa(q_b, k_win, v_win, _bias_table(prm["rel_bias"], t, tk, tk - t),
                         prm["b_sinks"][j], cq=t, wl=tk, pad=0)
            else:
                wl = WINDOW + CHUNK
                o = _swa(q_b, k_win, v_win, _bias_table(prm["rel_bias"], CHUNK, wl, WINDOW),
                         prm["b_sinks"][j], cq=CHUNK, wl=wl, pad=WINDOW)
        h = _channel(h.reshape(bx * t, d), o.reshape(bx * t, d), p.reshape(depth, bx * t, -1), i,
                     w["channel"][i], w["stacks"]).reshape(bx, t, d)
        if i == n_a - 1:
            q_b, k_s, v_s = _proj_b(_fold(h) if sample else h, _row(prm["norm_attn"][n_a]), w["wq_b"][0],
                                    w["gq_b"][0], _row(prm["kv_norm"]), w["wkv"], _row(prm["b_k_norm"]),
                                    w["gsum"], w["gexp"])
            if sample:
                q_b, k_s, v_s = _unfold_rows(q_b, bx), _unfold_rows(k_s, bx), _unfold_rows(v_s, bx)
                k_win = jnp.concatenate([swa_cache_k, k_s], axis=2)
                v_win = jnp.concatenate([swa_cache_v, v_s], axis=2)
            else:
                k_win, v_win = k_s, v_s
    sb_k = jnp.swapaxes(jnp.stack(sb_k), -1, -2)
    sb_v = jnp.swapaxes(jnp.stack(sb_v), -1, -2)
    return h, sb_k, sb_v, k_win[:, :, -WINDOW:], v_win[:, :, -WINDOW:]


def kernel(x_prompt, x_sample, p_prompt, p_sample, cache_sb_k, cache_sb_v, cache_swa_k, cache_swa_v, norm_attn, norm_ffn, norm_ple, a_w_qkv, a_w_o, kv_norm, b_w_kv, b_k_norm, b_w_q, b_q_norm, b_sinks, b_w_o, rel_bias, moe_w_group, moe_w_router, moe_w_gate, moe_w_up, moe_w_down, ple_w_proj, ple_w_gate):
    prm = {
        "norm_attn": norm_attn, "norm_ffn": norm_ffn, "norm_ple": norm_ple,
        "a_w_qkv": a_w_qkv, "a_w_o": a_w_o, "kv_norm": kv_norm, "b_w_kv": b_w_kv, "b_k_norm": b_k_norm,
        "b_w_q": b_w_q, "b_q_norm": b_q_norm, "b_sinks": b_sinks, "b_w_o": b_w_o, "rel_bias": rel_bias,
        "moe_w_group": moe_w_group, "moe_w_router": moe_w_router, "moe_w_gate": moe_w_gate,
        "moe_w_up": moe_w_up, "moe_w_down": moe_w_down, "ple_w_proj": ple_w_proj, "ple_w_gate": ple_w_gate,
    }
    w = _prep_weights(prm)
    y_p, sb_k_p, sb_v_p, swa_k_p, swa_v_p = _run_trunk(x_prompt, p_prompt, prm, w)
    y_s, sb_k_s, sb_v_s, swa_k_s, swa_v_s = _run_trunk(x_sample, p_sample, prm, w, cache_sb_k, cache_sb_v,
                                                       cache_swa_k, cache_swa_v)
    return (y_p, y_s, sb_k_p, sb_v_p, swa_k_p, swa_v_p, sb_k_s, sb_v_s, swa_k_s, swa_v_s)
```

```python
import functools
import math

import jax
import jax.numpy as jnp
from jax import lax
from jax.experimental import pallas as pl
from jax.experimental.pallas import tpu as pltpu

F32 = jnp.float32
BF16 = jnp.bfloat16

HEAD_DIM = 64
CHUNK = 64
WINDOW = 128
WIN_CHUNKS = WINDOW // CHUNK
NUM_BUCKETS = 32
MAX_DISTANCE = 128
N_GROUPS = 4
EXPERTS_PER_GROUP = 4
N_EXPERTS = N_GROUPS * EXPERTS_PER_GROUP
EPS = 1e-6
LOG2E = 1.4426950408889634
SB_BLOCK = 128
ROUTER_LANES = 128
VMEM_LIMIT = 56 * 1024 * 1024

_NT = (((1,), (1,)), ((), ()))


def _cparams(sem):
    return pltpu.CompilerParams(dimension_semantics=sem, vmem_limit_bytes=VMEM_LIMIT)


def _rms_unit(x):
    return x * lax.rsqrt(jnp.mean(x * x, axis=-1, keepdims=True) + EPS)


def _split_hl(a):
    hi = a.astype(BF16)
    lo = (a - hi.astype(F32)).astype(BF16)
    return jnp.concatenate([hi, lo], axis=-1)


def _resident(shape):
    nd = len(shape)
    return pl.BlockSpec(shape, lambda *_: (0,) * nd, pipeline_mode=pl.Buffered(1))


def _resident_layer(shape, layer):
    nd = len(shape)
    return pl.BlockSpec((1,) + tuple(shape[1:]), lambda *_: (layer,) + (0,) * (nd - 1),
                        pipeline_mode=pl.Buffered(1))


def _proj_a_kernel(x_ref, g_ref, wq_ref, wkvt_ref, q_ref, kt_ref, vt_ref, *, n_heads):
    xn = (_rms_unit(x_ref[0]) * g_ref[...]).astype(BF16)
    heads_per_dot = 4
    width = heads_per_dot * HEAD_DIM
    for c in range(n_heads // heads_per_dot):
        r = jnp.dot(xn, wq_ref[:, c * width:(c + 1) * width], preferred_element_type=F32)
        for hh in range(heads_per_dot):
            q_ref[0, c * heads_per_dot + hh] = r[:, hh * HEAD_DIM:(hh + 1) * HEAD_DIM].astype(BF16)
    for c in range(2 * n_heads // heads_per_dot):
        r = lax.dot_general(wkvt_ref[c * width:(c + 1) * width, :], xn, _NT, preferred_element_type=F32)
        for hh in range(heads_per_dot):
            h = c * heads_per_dot + hh
            piece = r[hh * HEAD_DIM:(hh + 1) * HEAD_DIM, :]
            if h < n_heads:
                kt_ref[0, h] = piece
            else:
                vt_ref[0, h - n_heads] = piece


def _proj_a(x, g, wq, wkvt):
    bx, t, d = x.shape
    n_heads = wq.shape[1] // HEAD_DIM
    tm = min(t, 512)
    return pl.pallas_call(
        functools.partial(_proj_a_kernel, n_heads=n_heads),
        grid=(bx, t // tm),
        in_specs=[pl.BlockSpec((1, tm, d), lambda b, i: (b, i, 0)),
                  _resident((1, d)), _resident(wq.shape), _resident(wkvt.shape)],
        out_specs=[pl.BlockSpec((1, n_heads, tm, HEAD_DIM), lambda b, i: (b, 0, i, 0)),
                   pl.BlockSpec((1, n_heads, HEAD_DIM, tm), lambda b, i: (b, 0, 0, i)),
                   pl.BlockSpec((1, n_heads, HEAD_DIM, tm), lambda b, i: (b, 0, 0, i))],
        out_shape=[jax.ShapeDtypeStruct((bx, n_heads, t, HEAD_DIM), BF16),
                   jax.ShapeDtypeStruct((bx, n_heads, HEAD_DIM, t), F32),
                   jax.ShapeDtypeStruct((bx, n_heads, HEAD_DIM, t), F32)],
        compiler_params=_cparams(("parallel", "parallel")),
        name="proj_a",
    )(x, g, wq, wkvt)


SB_DEAD = -104.0
SB_WINDOW_BLOCKS = 3
SB_SAMPLE_WINDOW = 256


def _suffix_neg_ones(n):
    r = lax.broadcasted_iota(jnp.int32, (n, n), 0)
    c = lax.broadcasted_iota(jnp.int32, (n, n), 1)
    return jnp.where(r >= c, -1.0, 0.0).astype(BF16)


def _strict_mask(n):
    row = lax.broadcasted_iota(jnp.int32, (n, n), 0)
    col = lax.broadcasted_iota(jnp.int32, (n, n), 1)
    return col < row


def _softplus(z):
    return jnp.maximum(z, 0.0) + jnp.log(1.0 + jnp.exp2(jnp.abs(z) * -LOG2E))


def _sb_strips(qs, kts, vts, carries, accs, uu, masks):
    n = len(qs)
    blk = uu.shape[1]
    nb = kts[0].shape[1] // blk
    zs = [jnp.dot(qs[i], kts[i], preferred_element_type=F32) for i in range(n)]
    sps = [_softplus(z) for z in zs]
    ws = [[None] * nb for _ in range(n)]
    carries = list(carries)
    for b in reversed(range(nb)):
        sl = slice(b * blk, (b + 1) * blk)
        spbs = [sp[:, sl] if masks[b] is None else jnp.where(masks[b], sp[:, sl], 0.0) for sp in sps]
        sufs = [jnp.dot(spb.astype(BF16), uu, preferred_element_type=F32) for spb in spbs]
        for i in range(n):
            wb = jnp.exp(zs[i][:, sl] + sufs[i] + carries[i])
            if masks[b] is not None:
                wb = jnp.where(masks[b], wb, 0.0)
            ws[i][b] = wb.astype(BF16)
            carries[i] = carries[i] + sufs[i][:, 0:1]
    wcat = [w[0] if nb == 1 else jnp.concatenate(w, axis=-1) for w in ws]
    accs = [accs[i] + lax.dot_general(wcat[i], vts[i], _NT, preferred_element_type=F32) for i in range(n)]
    return carries, accs


SB_STAGES = 5
SB_OLD_ROWS = 64


def _sb_prompt_fast_kernel(q_ref, kt_ref, vt_ref, o_ref, flag_ref, z_ref, sp_ref, suf_ref, w_ref, *, heads, groups):
    t = q_ref.shape[2]
    blk = SB_BLOCK
    old = SB_OLD_ROWS
    nq = t // blk
    win = SB_WINDOW_BLOCKS
    wl = win * blk
    ring = z_ref.shape[0]
    first = win - 1
    per_group = nq - first
    units = groups * per_group
    width = heads * HEAD_DIM
    uu = _suffix_neg_ones(blk)
    strict = _strict_mask(blk)
    z_ref[...] = jnp.zeros(z_ref.shape, F32)
    sp_ref[...] = jnp.zeros(sp_ref.shape, BF16)
    suf_ref[...] = jnp.zeros(suf_ref.shape, F32)
    w_ref[...] = jnp.zeros(w_ref.shape, BF16)

    hs = range(groups * heads)
    for i in range(first):
        _, accs = _sb_strips([q_ref[0, h, i * blk:(i + 1) * blk, :] for h in hs],
                             [kt_ref[0, h, :, 0:(i + 1) * blk].astype(BF16) for h in hs],
                             [vt_ref[0, h, :, 0:(i + 1) * blk].astype(BF16) for h in hs],
                             [jnp.zeros((blk, 1), F32)] * len(hs), [jnp.zeros((blk, HEAD_DIM), F32)] * len(hs),
                             uu, [None] * i + [strict])
        o_ref[0, i * blk:(i + 1) * blk, :] = jnp.concatenate(accs, axis=-1).astype(BF16)

    def where_is(n):
        u = jnp.clip(n, 0, units - 1)
        g = u // per_group
        i = first + u - g * per_group
        q0 = pl.multiple_of(i * blk, blk)
        k0 = pl.multiple_of((i - first) * blk, blk)
        return g * heads, q0, k0, lax.rem(n + ring * SB_STAGES, ring)

    def body(n, worst):
        h0, q0, k0, _ = where_is(n - 4)
        accs = []
        for h in range(heads):
            vt = vt_ref[0, h0 + h, :, pl.ds(k0, wl)].astype(BF16)
            accs.append(lax.dot_general(w_ref[h], vt, _NT, preferred_element_type=F32))
        o_ref[0, pl.ds(q0, blk), pl.ds(pl.multiple_of(h0 * HEAD_DIM, width), width)] = (
            jnp.concatenate(accs, axis=-1).astype(BF16))

        _, _, _, slot = where_is(n - 3)
        counts = n - 3 >= 0
        for h in range(heads):
            s_old, s_prev, s_own = suf_ref[h, 0:old], suf_ref[h, old:old + blk], suf_ref[h, old + blk:]
            w_own = jnp.where(strict, jnp.exp(z_ref[slot, h, :, 2 * blk:] + s_own), 0.0)
            carry = s_own[:, 0:1]
            w_prev = jnp.exp(z_ref[slot, h, :, blk:2 * blk] + s_prev + carry)
            carry = carry + s_prev[:, 0:1]
            w_old = jnp.exp(z_ref[slot, h, 0:old, 0:blk] + s_old + carry[0:old])
            w_ref[h, :, 2 * blk:] = w_own.astype(BF16)
            w_ref[h, :, blk:2 * blk] = w_prev.astype(BF16)
            w_ref[h, 0:old, 0:blk] = w_old.astype(BF16)
            edge = jnp.concatenate([carry[0:old] + s_old[:, 0:1], carry[old:]], axis=0)
            worst = jnp.maximum(worst, jnp.where(counts, edge, -jnp.inf))

        for h in range(heads):
            suf_ref[h] = jnp.dot(sp_ref[h], uu, preferred_element_type=F32)

        _, _, _, slot = where_is(n - 1)
        for h in range(heads):
            sp_new = _softplus(z_ref[slot, h, :, blk:])
            sp_ref[h, 0:old] = _softplus(z_ref[slot, h, 0:old, 0:blk]).astype(BF16)
            sp_ref[h, old:old + blk] = sp_new[:, 0:blk].astype(BF16)
            sp_ref[h, old + blk:] = jnp.where(strict, sp_new[:, blk:], 0.0).astype(BF16)

        h0, q0, k0, slot = where_is(n)
        for h in range(heads):
            q = q_ref[0, h0 + h, pl.ds(q0, blk), :]
            kt = kt_ref[0, h0 + h, :, pl.ds(k0, wl)].astype(BF16)
            z_ref[slot, h, :, blk:] = jnp.dot(q, kt[:, blk:], preferred_element_type=F32)
            z_ref[slot, h, 0:old, 0:blk] = jnp.dot(q[0:old], kt[:, 0:blk], preferred_element_type=F32)
        return worst

    worst = lax.fori_loop(0, units + SB_STAGES - 1, body, jnp.full((blk, 1), -jnp.inf, F32))
    flag_ref[...] = jnp.broadcast_to(jnp.max(worst), flag_ref.shape)


def _sb_prompt_full_kernel(q_ref, kt_ref, vt_ref, o_ref, *, heads):
    t = q_ref.shape[2]
    blk = SB_BLOCK
    uu = _suffix_neg_ones(blk)
    strict = _strict_mask(blk)

    def q_block(i, _):
        q0 = pl.multiple_of(i * blk, blk)
        qs = [q_ref[0, h, pl.ds(q0, blk), :] for h in range(heads)]

        def strips(k0, carries, accs, masks):
            return _sb_strips(qs, [kt_ref[0, h, :, pl.ds(k0, blk)].astype(BF16) for h in range(heads)],
                              [vt_ref[0, h, :, pl.ds(k0, blk)].astype(BF16) for h in range(heads)],
                              carries, accs, uu, masks)

        state = strips(q0, [jnp.zeros((blk, 1), F32)] * heads, [jnp.zeros((blk, HEAD_DIM), F32)] * heads,
                       [strict])

        def k_block(jj, st):
            carries, accs = strips(pl.multiple_of((i - 1 - jj) * blk, blk), st[0], st[1], [None])
            return tuple(carries), tuple(accs)

        _, accs = lax.fori_loop(0, i, k_block, (tuple(state[0]), tuple(state[1])))
        o_ref[0, pl.ds(q0, blk), :] = jnp.concatenate(list(accs), axis=-1).astype(BF16)
        return 0

    lax.fori_loop(0, t // blk, q_block, 0)


def _sb_prompt(q, kt, vt):
    b, n_heads, t, _ = q.shape
    wl = SB_WINDOW_BLOCKS * SB_BLOCK
    assert t >= wl
    o_shape = jax.ShapeDtypeStruct((b, t, n_heads * HEAD_DIM), BF16)

    def specs(hb):
        q_blk = pl.BlockSpec((1, hb, t, HEAD_DIM), lambda bi, hg: (bi, hg, 0, 0))
        kv_blk = pl.BlockSpec((1, hb, HEAD_DIM, t), lambda bi, hg: (bi, hg, 0, 0))
        o_blk = pl.BlockSpec((1, t, hb * HEAD_DIM), lambda bi, hg: (bi, 0, hg))
        return (b, n_heads // hb), [q_blk, kv_blk, kv_blk], o_blk

    heads, groups = 4, 2
    rows = SB_OLD_ROWS + 2 * SB_BLOCK
    grid, in_specs, o_blk = specs(heads * groups)
    o_fast, flags = pl.pallas_call(
        functools.partial(_sb_prompt_fast_kernel, heads=heads, groups=groups),
        grid=grid,
        in_specs=in_specs,
        out_specs=[o_blk, pl.BlockSpec((1, 1, 8, 128), lambda bi, hg: (bi, hg, 0, 0))],
        out_shape=[o_shape, jax.ShapeDtypeStruct((b, grid[1], 8, 128), F32)],
        scratch_shapes=[pltpu.VMEM((SB_STAGES - 1, heads, SB_BLOCK, wl), F32),
                        pltpu.VMEM((heads, rows, SB_BLOCK), BF16),
                        pltpu.VMEM((heads, rows, SB_BLOCK), F32),
                        pltpu.VMEM((heads, SB_BLOCK, wl), BF16)],
        compiler_params=_cparams(("parallel", "parallel")),
        name="sb_prompt_fast",
    )(q, kt, vt)

    def full():
        grid, in_specs, o_blk = specs(4)
        return pl.pallas_call(
            functools.partial(_sb_prompt_full_kernel, heads=4),
            grid=grid,
            in_specs=in_specs,
            out_specs=o_blk,
            out_shape=o_shape,
            compiler_params=_cparams(("parallel", "parallel")),
            name="sb_prompt_full",
        )(q, kt, vt)

    return lax.cond(jnp.max(flags) > SB_DEAD, full, lambda: o_fast)


def _sb_sample_kernel(q_ref, ktn_ref, vtn_ref, ktc_ref, vtc_ref, o_ref, *maybe_flag, heads, strip):
    s = q_ref.shape[2]
    width = ktc_ref.shape[4]
    blk = SB_BLOCK
    uu = _suffix_neg_ones(blk)
    uu_new = _suffix_neg_ones(s)
    strict = _strict_mask(s)

    qs = [q_ref[0, h] for h in range(heads)]
    state = _sb_strips(qs, [ktn_ref[0, h].astype(BF16) for h in range(heads)],
                       [vtn_ref[0, h].astype(BF16) for h in range(heads)],
                       [jnp.zeros((s, 1), F32)] * heads, [jnp.zeros((s, HEAD_DIM), F32)] * heads,
                       uu_new, [strict])

    def k_strip(jj, st):
        k0 = pl.multiple_of(width - (jj + 1) * strip * blk, blk)
        carries, accs = _sb_strips(
            qs, [ktc_ref[0, 0, h, :, pl.ds(k0, strip * blk)].astype(BF16) for h in range(heads)],
            [vtc_ref[0, 0, h, :, pl.ds(k0, strip * blk)].astype(BF16) for h in range(heads)],
            st[0], st[1], uu, [None] * strip)
        return tuple(carries), tuple(accs)

    carries, accs = lax.fori_loop(0, width // (strip * blk), k_strip, (tuple(state[0]), tuple(state[1])))
    o_ref[0] = jnp.concatenate(list(accs), axis=-1).astype(BF16)
    if maybe_flag:
        worst = carries[0]
        for h in range(1, heads):
            worst = jnp.maximum(worst, carries[h])
        maybe_flag[0][...] = jnp.broadcast_to(jnp.max(worst), maybe_flag[0].shape)


def _sb_sample(q, kt_new, vt_new, cache_kt, cache_vt, layer):
    b, n_heads, s, _ = q.shape
    past = cache_kt.shape[4]
    width = min(past, SB_SAMPLE_WINDOW)
    assert past % width == 0 and width % SB_BLOCK == 0
    last = past // width - 1
    o_shape = jax.ShapeDtypeStruct((b, s, n_heads * HEAD_DIM), BF16)

    def specs(heads, cache_width, cache_block):
        q_blk = pl.BlockSpec((1, heads, s, HEAD_DIM), lambda bi, hg: (bi, hg, 0, 0))
        new_blk = pl.BlockSpec((1, heads, HEAD_DIM, s), lambda bi, hg: (bi, hg, 0, 0))
        cache_blk = pl.BlockSpec((1, 1, heads, HEAD_DIM, cache_width),
                                 lambda bi, hg: (layer, bi, hg, 0, cache_block))
        o_blk = pl.BlockSpec((1, s, heads * HEAD_DIM), lambda bi, hg: (bi, 0, hg))
        return (b, n_heads // heads), [q_blk, new_blk, new_blk, cache_blk, cache_blk], o_blk

    heads = n_heads
    grid, in_specs, o_blk = specs(heads, width, last)
    o_fast, flags = pl.pallas_call(
        functools.partial(_sb_sample_kernel, heads=heads, strip=width // SB_BLOCK),
        grid=grid,
        in_specs=in_specs,
        out_specs=[o_blk, pl.BlockSpec((1, 1, 8, 128), lambda bi, hg: (bi, hg, 0, 0))],
        out_shape=[o_shape, jax.ShapeDtypeStruct((b, n_heads // heads, 8, 128), F32)],
        compiler_params=_cparams(("parallel", "parallel")),
        name="sb_sample_fast",
    )(q, kt_new, vt_new, cache_kt, cache_vt)
    if width == past:
        return o_fast

    def full():
        heads = 4
        grid, in_specs, o_blk = specs(heads, past, 0)
        return pl.pallas_call(
            functools.partial(_sb_sample_kernel, heads=heads, strip=1),
            grid=grid,
            in_specs=in_specs,
            out_specs=o_blk,
            out_shape=o_shape,
            compiler_params=_cparams(("parallel", "parallel")),
            name="sb_sample_full",
        )(q, kt_new, vt_new, cache_kt, cache_vt)

    return lax.cond(jnp.max(flags) > SB_DEAD, full, lambda: o_fast)


def _route(logits):
    tm = logits.shape[0]
    lane = lax.broadcasted_iota(jnp.int32, (tm, ROUTER_LANES), 1)
    lane_f = lane.astype(F32)
    neg = -jnp.inf
    first = lambda hit: jnp.min(jnp.where(hit, lane_f, float(ROUTER_LANES)), axis=-1, keepdims=True)

    gl = jnp.where(lane < N_GROUPS, logits, neg)
    g_max = jnp.max(gl, axis=-1, keepdims=True)
    g_top = 1.0 / jnp.sum(jnp.exp(gl - g_max), axis=-1, keepdims=True)
    g_idx = first(gl == g_max)

    lo = N_GROUPS + g_idx * EXPERTS_PER_GROUP
    in_group = (lane_f >= lo) & (lane_f < lo + EXPERTS_PER_GROUP)
    sel = jnp.where(in_group, logits, neg)
    t1 = jnp.max(sel, axis=-1, keepdims=True)
    i1 = first(sel == t1)
    sel2 = jnp.where(lane_f == i1, neg, sel)
    t2 = jnp.max(sel2, axis=-1, keepdims=True)
    i2 = first(sel2 == t2)
    e2 = jnp.exp(t2 - t1)
    den = 1.0 + e2
    w1 = (1.0 / den) * g_top
    w2 = (e2 / den) * g_top
    return jnp.where(lane_f == i1, w1, 0.0) + jnp.where(lane_f == i2, w2, 0.0)


CHANNEL_ROW_SPLITS = 2
EXPERTS_PER_TRIP = 16


def _channel_kernel(x_ref, o_ref, p_ref, wo_ref, gffn_ref, wr_ref, wg_ref, wu_ref, wd_ref, gple_ref,
                    wpg_ref, wpp_ref, out_ref, acc_ref):
    tm = x_ref.shape[0]
    d_expert = wd_ref.shape[2]
    half = tm // CHANNEL_ROW_SPLITS
    rows = [slice(i * half, (i + 1) * half) for i in range(CHANNEL_ROW_SPLITS)]

    h1s = [x_ref[r, :] + jnp.dot(o_ref[r, :], wo_ref[...], preferred_element_type=F32) for r in rows]
    xns = [_rms_unit(h1) * gffn_ref[...] for h1 in h1s]
    xnbs = [xn.astype(BF16) for xn in xns]
    xlos = [(xn - xnb.astype(F32)).astype(BF16) for xn, xnb in zip(xns, xnbs)]
    logits = []
    for xnb, xlo in zip(xnbs, xlos):
        hi_lo = jnp.dot(xnb, wr_ref[...], preferred_element_type=F32)
        lo_hi = jnp.dot(xlo, wr_ref[:, 0:ROUTER_LANES], preferred_element_type=F32)
        logits.append(hi_lo[:, 0:ROUTER_LANES] + hi_lo[:, ROUTER_LANES:] + lo_hi)
    gates = jnp.concatenate([_route(l) for l in logits], axis=0)
    xnb = jnp.concatenate(xnbs, axis=0)
    lane = lax.broadcasted_iota(jnp.int32, gates.shape, 1)
    for r, h1 in zip(rows, h1s):
        acc_ref[r, :] = h1

    def experts(j, _):
        es = [j * EXPERTS_PER_TRIP + k for k in range(EXPERTS_PER_TRIP)]
        gs = [jnp.dot(xnb, wg_ref[0, e], preferred_element_type=F32) for e in es]
        us = [jnp.dot(xnb, wu_ref[0, e], preferred_element_type=F32) for e in es]
        hids = []
        for e, g, u in zip(es, gs, us):
            gate_e = jnp.sum(jnp.where(lane == N_GROUPS + e, gates, 0.0), axis=-1, keepdims=True)
            hids.append(((g * jax.nn.sigmoid(g)) * u * gate_e).astype(BF16))
        wd = wd_ref[0, pl.ds(j * EXPERTS_PER_TRIP, EXPERTS_PER_TRIP)]
        acc_ref[...] += jnp.dot(jnp.concatenate(hids, axis=-1),
                                wd.reshape(EXPERTS_PER_TRIP * d_expert, wd.shape[-1]),
                                preferred_element_type=F32)
        return 0

    lax.fori_loop(0, N_EXPERTS // EXPERTS_PER_TRIP, experts, 0)
    h2s = [acc_ref[r, :] for r in rows]
    x3s = [(_rms_unit(h2) * gple_ref[...]).astype(BF16) for h2 in h2s]
    ple_gates = [jax.nn.sigmoid(jnp.dot(x3, wpg_ref[0], preferred_element_type=F32)) for x3 in x3s]
    projs = [jnp.dot(p_ref[0, r, :].astype(BF16), wpp_ref[0], preferred_element_type=F32) for r in rows]
    for r, h2, proj, gate in zip(rows, h2s, projs, ple_gates):
        out_ref[r, :] = h2 + proj * gate


def _channel(x, o, p, layer, w, stacks):
    n, d = x.shape
    tm = min(n, 512)
    row = lambda cols: pl.BlockSpec((tm, cols), lambda i: (i, 0))
    small = [w["wo"], w["gffn"], w["wr"]]
    moe = [stacks["wg"], stacks["wu"], stacks["wd"]]
    ple = [stacks["wpg"], stacks["wpp"]]
    return pl.pallas_call(
        _channel_kernel,
        grid=(n // tm,),
        in_specs=([row(d), row(d), pl.BlockSpec((1, tm, p.shape[2]), lambda i: (layer, i, 0))]
                  + [_resident(a.shape) for a in small] + [_resident_layer(a.shape, layer) for a in moe]
                  + [_resident(w["gple"].shape)] + [_resident_layer(a.shape, layer) for a in ple]),
        out_specs=row(d),
        out_shape=jax.ShapeDtypeStruct((n, d), F32),
        scratch_shapes=[pltpu.VMEM((tm, d), F32)],
        compiler_params=_cparams(("parallel",)),
        name="channel",
    )(x, o, p, *small, *moe, w["gple"], *ple)


PROJ_B_ROW_SPLITS = 2


def _proj_b_kernel(h_ref, ga_ref, wq_ref, gq_ref, gkv_ref, wkv_ref, gk_ref, gsum_ref, gexp_ref,
                   q_ref, k_ref, v_ref, *, n_heads, n_kv):
    tm = h_ref.shape[1]
    splits = PROJ_B_ROW_SPLITS if tm % (16 * PROJ_B_ROW_SPLITS) == 0 else 1
    rows = [slice(i * tm // splits, (i + 1) * tm // splits) for i in range(splits)]
    ys = [_rms_unit(h_ref[0, r, :]) for r in rows]
    qs = [jnp.dot((y * ga_ref[...]).astype(BF16), wq_ref[...], preferred_element_type=F32) for y in ys]
    kvs = [jnp.dot((y * gkv_ref[...]).astype(BF16), wkv_ref[...], preferred_element_type=F32) for y in ys]
    mss = [jnp.dot((q * q).astype(BF16), gsum_ref[...], preferred_element_type=F32) * (1.0 / HEAD_DIM)
           for q in qs]
    inv_fulls = [jnp.dot(_split_hl(lax.rsqrt(ms + EPS)), gexp_ref[...], preferred_element_type=F32)
                 for ms in mss]
    for r, q, inv_full in zip(rows, qs, inv_fulls):
        qn = (q * inv_full * gq_ref[...]) * (HEAD_DIM ** -0.5)
        for h in range(n_heads):
            q_ref[0, h, r, :] = qn[:, h * HEAD_DIM:(h + 1) * HEAD_DIM].astype(BF16)
    for r, kv in zip(rows, kvs):
        for h in range(n_kv):
            kh = kv[:, h * HEAD_DIM:(h + 1) * HEAD_DIM]
            k_ref[0, h, r, :] = _rms_unit(kh) * gk_ref[...]
            v_ref[0, h, r, :] = kv[:, (n_kv + h) * HEAD_DIM:(n_kv + h + 1) * HEAD_DIM]


def _proj_b(h, ga, wq, gq, gkv, wkv, gk, gsum, gexp):
    bx, t, d = h.shape
    n_heads = wq.shape[1] // HEAD_DIM
    n_kv = wkv.shape[1] // (2 * HEAD_DIM)
    tm = min(t, 512)
    hm = lambda b, i: (b, 0, i, 0)
    ins = [ga, wq, gq, gkv, wkv, gk, gsum, gexp]
    return pl.pallas_call(
        functools.partial(_proj_b_kernel, n_heads=n_heads, n_kv=n_kv),
        grid=(bx, t // tm),
        in_specs=[pl.BlockSpec((1, tm, d), lambda b, i: (b, i, 0))] + [_resident(a.shape) for a in ins],
        out_specs=[pl.BlockSpec((1, n_heads, tm, HEAD_DIM), hm),
                   pl.BlockSpec((1, n_kv, tm, HEAD_DIM), hm),
                   pl.BlockSpec((1, n_kv, tm, HEAD_DIM), hm)],
        out_shape=[jax.ShapeDtypeStruct((bx, n_heads, t, HEAD_DIM), BF16),
                   jax.ShapeDtypeStruct((bx, n_kv, t, HEAD_DIM), F32),
                   jax.ShapeDtypeStruct((bx, n_kv, t, HEAD_DIM), F32)],
        compiler_params=_cparams(("parallel", "parallel")),
        name="proj_b",
    )(h, *ins)


def _t5_bucket(rel):
    nb = NUM_BUCKETS // 2
    max_exact = nb // 2
    n = jnp.abs(rel)
    large = max_exact + (jnp.log(jnp.maximum(n, 1).astype(jnp.float32) / max_exact)
                         / math.log(MAX_DISTANCE / max_exact) * (nb - max_exact)).astype(jnp.int32)
    large = jnp.minimum(large, nb - 1)
    return jnp.where(rel > 0, nb, 0) + jnp.where(n < max_exact, n, large)


def _bias_kernel(bucket_ref, rbt_ref, out_ref):
    bucket = bucket_ref[...].astype(F32)
    guard = 2.0 ** -10
    acc = jnp.zeros(out_ref.shape, F32)
    for b in range(NUM_BUCKETS):
        hit = (bucket >= b - guard) & (bucket < b + 1 - guard)
        acc = acc + jnp.where(hit, rbt_ref[:, b:b + 1], 0.0)
    out_ref[...] = jnp.where(bucket < 0, -jnp.inf, acc)


def _bias_table(rel_bias, nq, nk, key_offset, chunk_window):
    qi = jnp.arange(nq, dtype=jnp.int32)[:, None]
    kj = jnp.arange(nk, dtype=jnp.int32)[None, :] - key_offset
    bucket = _t5_bucket(kj - qi)
    if chunk_window:
        qc, kc = qi // CHUNK, kj // CHUNK
        bucket = jnp.where((kc <= qc) & (kc >= qc - WIN_CHUNKS), bucket, -1)
    bucket = bucket.reshape(1, nq * nk)
    n_heads = rel_bias.shape[1]
    out = pl.pallas_call(
        _bias_kernel,
        out_shape=jax.ShapeDtypeStruct((n_heads, nq * nk), F32),
        name="bias_table",
    )(bucket, rel_bias.T)
    return out.reshape(n_heads, nq, nk)


SWA_ROWS_PER_TRIP = 128


def _swa_kernel(sink_ref, q_ref, k_ref, v_ref, bias_ref, o_ref, kpad_ref, vpad_ref, *, cq, wl, pad, n_chunks,
                group):
    n_kv, tk = k_ref.shape[1], k_ref.shape[2]
    if pad:
        kpad_ref[:, 0:pad, :] = jnp.zeros((n_kv, pad, HEAD_DIM), BF16)
        vpad_ref[:, 0:pad, :] = jnp.zeros((n_kv, pad, HEAD_DIM), BF16)
    kpad_ref[:, pad:pad + tk, :] = k_ref[0].astype(BF16)
    vpad_ref[:, pad:pad + tk, :] = v_ref[0].astype(BF16)

    per_trip = max(1, SWA_ROWS_PER_TRIP // cq)
    per_trip = per_trip if n_chunks % per_trip == 0 else 1
    n_heads = n_kv * group

    def chunks(c, masked):
        r0s = [pl.multiple_of((c * per_trip + i) * cq, cq) for i in range(per_trip)]
        logits = [[lax.dot_general(
            q_ref[0, kv * group:(kv + 1) * group, pl.ds(r0, cq), :].reshape(group * cq, HEAD_DIM),
            kpad_ref[kv, pl.ds(r0, wl), :], _NT, preferred_element_type=F32)
            for kv in range(n_kv)] for r0 in r0s]
        es, dens = [], []
        for r0, lg in zip(r0s, logits):
            valid = lax.broadcasted_iota(jnp.int32, (cq, wl), 1) + r0 >= pad
            for h in range(n_heads):
                l = lg[h // group][(h % group) * cq:(h % group + 1) * cq] + bias_ref[h]
                if masked:
                    l = jnp.where(valid, l, -jnp.inf)
                sink = sink_ref[h]
                m = jnp.maximum(jnp.max(l, axis=-1, keepdims=True), sink)
                e = jnp.exp(l - m)
                dens.append(jnp.sum(e, axis=-1, keepdims=True) + jnp.exp(sink - m))
                es.append(e.astype(BF16))
        pvs = [[jnp.dot(jnp.concatenate(es[i * n_heads + kv * group:i * n_heads + (kv + 1) * group], axis=0),
                        vpad_ref[kv, pl.ds(r0, wl), :], preferred_element_type=F32)
                for kv in range(n_kv)] for i, r0 in enumerate(r0s)]
        for i, r0 in enumerate(r0s):
            outs = [pvs[i][h // group][(h % group) * cq:(h % group + 1) * cq] / dens[i * n_heads + h]
                    for h in range(n_heads)]
            o_ref[0, pl.ds(r0, cq), :] = jnp.concatenate(outs, axis=-1).astype(BF16)
        return 0

    trips = n_chunks // per_trip
    masked_trips = min(trips, -(-pad // (cq * per_trip)))
    for c in range(masked_trips):
        chunks(c, True)
    lax.fori_loop(masked_trips, trips, lambda c, _: chunks(c, False), 0)


def _swa(q, k_win, v_win, bias, sinks, *, cq, wl, pad):
    b, n_heads, tq, _ = q.shape
    n_kv, tk = k_win.shape[1], k_win.shape[2]
    n_chunks = tq // cq
    assert (n_chunks - 1) * cq + wl == pad + tk
    kv_blk = pl.BlockSpec((1, n_kv, tk, HEAD_DIM), lambda bi: (bi, 0, 0, 0))
    return pl.pallas_call(
        functools.partial(_swa_kernel, cq=cq, wl=wl, pad=pad, n_chunks=n_chunks, group=n_heads // n_kv),
        grid=(b,),
        in_specs=[pl.BlockSpec(memory_space=pltpu.SMEM),
                  pl.BlockSpec((1, n_heads, tq, HEAD_DIM), lambda bi: (bi, 0, 0, 0)),
                  kv_blk, kv_blk, _resident(bias.shape)],
        out_specs=pl.BlockSpec((1, tq, n_heads * HEAD_DIM), lambda bi: (bi, 0, 0)),
        out_shape=jax.ShapeDtypeStruct((b, tq, n_heads * HEAD_DIM), BF16),
        scratch_shapes=[pltpu.VMEM((n_kv, pad + tk, HEAD_DIM), BF16),
                        pltpu.VMEM((n_kv, pad + tk, HEAD_DIM), BF16)],
        compiler_params=_cparams(("parallel",)),
        name="swa",
    )(sinks, q, k_win, v_win, bias)


def _row(v):
    return v.reshape(1, -1).astype(F32)


def _router_hi_lo(wr):
    hi = wr.astype(BF16)
    lo = (wr - hi.astype(F32)).astype(BF16)
    return jnp.concatenate([hi, lo], axis=1)


def _prep_weights(prm):
    d = prm["a_w_o"].shape[1]
    depth = prm["norm_ffn"].shape[0]
    n_a = prm["a_w_qkv"].shape[0]
    scale = HEAD_DIM ** -0.5
    w = {"channel": [], "n_a": n_a, "depth": depth}
    for i in range(depth):
        wo = prm["a_w_o"][i] if i < n_a else prm["b_w_o"][i - n_a]
        pad = ROUTER_LANES - N_GROUPS - N_EXPERTS
        wr = jnp.concatenate([prm["moe_w_group"][i], prm["moe_w_router"][i].reshape(d, N_EXPERTS),
                              jnp.zeros((d, pad), F32)], axis=1)
        w["channel"].append({"wo": wo.astype(BF16), "gffn": _row(prm["norm_ffn"][i]),
                             "wr": _router_hi_lo(wr), "gple": _row(prm["norm_ple"][i])})
    w["stacks"] = {"wg": prm["moe_w_gate"].astype(BF16), "wu": prm["moe_w_up"].astype(BF16),
                   "wd": prm["moe_w_down"].astype(BF16), "wpg": prm["ple_w_gate"].astype(BF16),
                   "wpp": prm["ple_w_proj"].astype(BF16)}
    w["qkv"] = []
    for i in range(n_a):
        wq = prm["a_w_qkv"][i]
        hd = wq.shape[1] // 3
        w["qkv"].append(((wq[:, :hd] * scale).astype(BF16),
                         wq[:, hd:].T.astype(BF16)))
    n_heads = prm["b_w_q"].shape[2] // HEAD_DIM
    head_of = jnp.arange(n_heads * HEAD_DIM, dtype=jnp.int32) // HEAD_DIM
    lanes = jnp.arange(ROUTER_LANES, dtype=jnp.int32)
    member = (head_of[:, None] == lanes[None, :]).astype(BF16)
    w["gsum"] = member
    w["gexp"] = jnp.concatenate([member.T, member.T], axis=0)
    w["wq_b"] = [prm["b_w_q"][j].astype(BF16) for j in range(depth - n_a)]
    w["gq_b"] = [_row(jnp.tile(prm["b_q_norm"][j], n_heads)) for j in range(depth - n_a)]
    w["wkv"] = prm["b_w_kv"].astype(BF16)
    return w


def _assert_sample_window_visible(past_len, s, tk):
    q_chunk = [(past_len + i) // CHUNK for i in range(s)]
    k_pos = [past_len + s - tk + j for j in range(tk)]
    ok = all(kp >= 0 and qc - WIN_CHUNKS <= kp // CHUNK <= qc for qc in q_chunk for kp in k_pos)
    if not ok:
        raise NotImplementedError("sample window with masked keys")


def _fold(x):
    return x.reshape(1, -1, x.shape[-1])


def _unfold_rows(a, b):
    _, n_heads, rows, dh = a.shape
    return a.reshape(n_heads, b, rows // b, dh).transpose(1, 0, 2, 3)


def _unfold_cols(a, b):
    _, n_heads, dh, rows = a.shape
    return a.reshape(n_heads, dh, b, rows // b).transpose(2, 0, 1, 3)


def _run_trunk(x, p, prm, w, sb_cache_k=None, sb_cache_v=None, swa_cache_k=None, swa_cache_v=None):
    bx, t, d = x.shape
    n_a, depth = w["n_a"], w["depth"]
    sample = sb_cache_k is not None
    h = x
    sb_k, sb_v = [], []
    k_win = v_win = None
    q_b = None
    for i in range(depth):
        if i < n_a:
            if sample:
                q, kt, vt = _proj_a(_fold(h), _row(prm["norm_attn"][i]), *w["qkv"][i])
                q, kt, vt = _unfold_rows(q, bx), _unfold_cols(kt, bx), _unfold_cols(vt, bx)
            else:
                q, kt, vt = _proj_a(h, _row(prm["norm_attn"][i]), *w["qkv"][i])
            sb_k.append(kt)
            sb_v.append(vt)
            if sample:
                o = _sb_sample(q, kt, vt, jnp.swapaxes(sb_cache_k, -1, -2), jnp.swapaxes(sb_cache_v, -1, -2), i)
            else:
                o = _sb_prompt(q, kt, vt)
        else:
            j = i - n_a
            if j > 0:
                raise NotImplementedError("one B layer supported")
            if sample:
                tk = k_win.shape[2]
                _assert_sample_window_visible(sb_cache_k.shape[3], t, tk)
                o = _swa(q_b, k_win, v_win, _bias_table(prm["rel_bias"], t, tk, tk - t, False),
                         prm["b_sinks"][j], cq=t, wl=tk, pad=0)
            else:
                cq = 2 * CHUNK
                wl = WINDOW + cq
                assert t % cq == 0
                o = _swa(q_b, k_win, v_win, _bias_table(prm["rel_bias"], cq, wl, WINDOW, True),
                         prm["b_sinks"][j], cq=cq, wl=wl, pad=WINDOW)
        h = _channel(h.reshape(bx * t, d), o.reshape(bx * t, d), p.reshape(depth, bx * t, -1), i,
                     w["channel"][i], w["stacks"]).reshape(bx, t, d)
        if i == n_a - 1:
            q_b, k_s, v_s = _proj_b(_fold(h) if sample else h, _row(prm["norm_attn"][n_a]), w["wq_b"][0],
                                    w["gq_b"][0], _row(prm["kv_norm"]), w["wkv"], _row(prm["b_k_norm"]),
                                    w["gsum"], w["gexp"])
            if sample:
                q_b, k_s, v_s = _unfold_rows(q_b, bx), _unfold_rows(k_s, bx), _unfold_rows(v_s, bx)
                k_win = jnp.concatenate([swa_cache_k, k_s], axis=2)
                v_win = jnp.concatenate([swa_cache_v, v_s], axis=2)
            else:
                k_win, v_win = k_s, v_s
    sb_k = jnp.swapaxes(jnp.stack(sb_k), -1, -2)
    sb_v = jnp.swapaxes(jnp.stack(sb_v), -1, -2)
    return h, sb_k, sb_v, k_win[:, :, -WINDOW:], v_win[:, :, -WINDOW:]


def kernel(x_prompt, x_sample, p_prompt, p_sample, cache_sb_k, cache_sb_v, cache_swa_k, cache_swa_v, norm_attn, norm_ffn, norm_ple, a_w_qkv, a_w_o, kv_norm, b_w_kv, b_k_norm, b_w_q, b_q_norm, b_sinks, b_w_o, rel_bias, moe_w_group, moe_w_router, moe_w_gate, moe_w_up, moe_w_down, ple_w_proj, ple_w_gate):
    prm = {
        "norm_attn": norm_attn, "norm_ffn": norm_ffn, "norm_ple": norm_ple,
        "a_w_qkv": a_w_qkv, "a_w_o": a_w_o, "kv_norm": kv_norm, "b_w_kv": b_w_kv, "b_k_norm": b_k_norm,
        "b_w_q": b_w_q, "b_q_norm": b_q_norm, "b_sinks": b_sinks, "b_w_o": b_w_o, "rel_bias": rel_bias,
        "moe_w_group": moe_w_group, "moe_w_router": moe_w_router, "moe_w_gate": moe_w_gate,
        "moe_w_up": moe_w_up, "moe_w_down": moe_w_down, "ple_w_proj": ple_w_proj, "ple_w_gate": ple_w_gate,
    }
    w = _prep_weights(prm)
    y_p, sb_k_p, sb_v_p, swa_k_p, swa_v_p = _run_trunk(x_prompt, p_prompt, prm, w)
    y_s, sb_k_s, sb_v_s, swa_k_s, swa_v_s = _run_trunk(x_sample, p_sample, prm, w, cache_sb_k, cache_sb_v,
                                                       cache_swa_k, cache_swa_v)
    return (y_p, y_s, sb_k_p, sb_v_p, swa_k_p, swa_v_p, sb_k_s, sb_v_s, swa_k_s, swa_v_s)
```

```python
import functools
import math

import jax
import jax.numpy as jnp
from jax import lax
from jax.experimental import pallas as pl
from jax.experimental.pallas import tpu as pltpu

F32 = jnp.float32
BF16 = jnp.bfloat16

HEAD_DIM = 64
CHUNK = 64
WINDOW = 128
WIN_CHUNKS = WINDOW // CHUNK
NUM_BUCKETS = 32
MAX_DISTANCE = 128
N_GROUPS = 4
EXPERTS_PER_GROUP = 4
N_EXPERTS = N_GROUPS * EXPERTS_PER_GROUP
EPS = 1e-6
LOG2E = 1.4426950408889634
SB_BLOCK = 128
ROUTER_LANES = 128
VMEM_LIMIT = 63 * 1024 * 1024

_NT = (((1,), (1,)), ((), ()))


def _cparams(sem):
    return pltpu.CompilerParams(dimension_semantics=sem, vmem_limit_bytes=VMEM_LIMIT)


def _rms_unit(x):
    return x * lax.rsqrt(jnp.mean(x * x, axis=-1, keepdims=True) + EPS)


def _split_hl(a):
    hi = a.astype(BF16)
    lo = (a - hi.astype(F32)).astype(BF16)
    return jnp.concatenate([hi, lo], axis=-1)


def _resident(shape):
    nd = len(shape)
    return pl.BlockSpec(shape, lambda *_: (0,) * nd, pipeline_mode=pl.Buffered(1))


def _resident_layer(shape, layer):
    nd = len(shape)
    return pl.BlockSpec((1,) + tuple(shape[1:]), lambda *_: (layer,) + (0,) * (nd - 1),
                        pipeline_mode=pl.Buffered(1))


def _proj_a_kernel(x_ref, g_ref, wq_ref, wkvt_ref, q_ref, kt_ref, vt_ref, *, n_heads):
    xn = (_rms_unit(x_ref[0]) * g_ref[...]).astype(BF16)
    heads_per_dot = 4
    width = heads_per_dot * HEAD_DIM
    for c in range(n_heads // heads_per_dot):
        r = jnp.dot(xn, wq_ref[:, c * width:(c + 1) * width], preferred_element_type=F32)
        for hh in range(heads_per_dot):
            q_ref[0, c * heads_per_dot + hh] = r[:, hh * HEAD_DIM:(hh + 1) * HEAD_DIM].astype(BF16)
    for c in range(2 * n_heads // heads_per_dot):
        r = lax.dot_general(wkvt_ref[c * width:(c + 1) * width, :], xn, _NT, preferred_element_type=F32)
        for hh in range(heads_per_dot):
            h = c * heads_per_dot + hh
            piece = r[hh * HEAD_DIM:(hh + 1) * HEAD_DIM, :]
            if h < n_heads:
                kt_ref[0, h] = piece
            else:
                vt_ref[0, h - n_heads] = piece


def _proj_a(x, g, wq, wkvt):
    bx, t, d = x.shape
    n_heads = wq.shape[1] // HEAD_DIM
    tm = min(t, 512)
    return pl.pallas_call(
        functools.partial(_proj_a_kernel, n_heads=n_heads),
        grid=(bx, t // tm),
        in_specs=[pl.BlockSpec((1, tm, d), lambda b, i: (b, i, 0)),
                  _resident((1, d)), _resident(wq.shape), _resident(wkvt.shape)],
        out_specs=[pl.BlockSpec((1, n_heads, tm, HEAD_DIM), lambda b, i: (b, 0, i, 0)),
                   pl.BlockSpec((1, n_heads, HEAD_DIM, tm), lambda b, i: (b, 0, 0, i)),
                   pl.BlockSpec((1, n_heads, HEAD_DIM, tm), lambda b, i: (b, 0, 0, i))],
        out_shape=[jax.ShapeDtypeStruct((bx, n_heads, t, HEAD_DIM), BF16),
                   jax.ShapeDtypeStruct((bx, n_heads, HEAD_DIM, t), F32),
                   jax.ShapeDtypeStruct((bx, n_heads, HEAD_DIM, t), F32)],
        compiler_params=_cparams(("parallel", "parallel")),
        name="proj_a",
    )(x, g, wq, wkvt)


SB_DEAD = -104.0
SB_WINDOW_BLOCKS = 3
SB_SAMPLE_WINDOW = 256


def _suffix_neg_ones(n):
    r = lax.broadcasted_iota(jnp.int32, (n, n), 0)
    c = lax.broadcasted_iota(jnp.int32, (n, n), 1)
    return jnp.where(r >= c, -1.0, 0.0).astype(BF16)


def _strict_mask(n):
    row = lax.broadcasted_iota(jnp.int32, (n, n), 0)
    col = lax.broadcasted_iota(jnp.int32, (n, n), 1)
    return col < row


def _softplus(z):
    return jnp.maximum(z, 0.0) + jnp.log(1.0 + jnp.exp2(jnp.abs(z) * -LOG2E))


def _sb_strips(qs, kts, vts, carries, accs, uu, masks):
    n = len(qs)
    blk = uu.shape[1]
    nb = kts[0].shape[1] // blk
    zs = [jnp.dot(qs[i], kts[i], preferred_element_type=F32) for i in range(n)]
    sps = [_softplus(z) for z in zs]
    ws = [[None] * nb for _ in range(n)]
    carries = list(carries)
    for b in reversed(range(nb)):
        sl = slice(b * blk, (b + 1) * blk)
        spbs = [sp[:, sl] if masks[b] is None else jnp.where(masks[b], sp[:, sl], 0.0) for sp in sps]
        sufs = [jnp.dot(spb.astype(BF16), uu, preferred_element_type=F32) for spb in spbs]
        for i in range(n):
            wb = jnp.exp(zs[i][:, sl] + sufs[i] + carries[i])
            if masks[b] is not None:
                wb = jnp.where(masks[b], wb, 0.0)
            ws[i][b] = wb.astype(BF16)
            carries[i] = carries[i] + sufs[i][:, 0:1]
    wcat = [w[0] if nb == 1 else jnp.concatenate(w, axis=-1) for w in ws]
    accs = [accs[i] + lax.dot_general(wcat[i], vts[i], _NT, preferred_element_type=F32) for i in range(n)]
    return carries, accs


SB_STAGES = 5
SB_OLD_ROWS = 64


def _sb_prompt_fast_kernel(q_ref, kt_ref, vt_ref, o_ref, flag_ref, z_ref, sp_ref, suf_ref, w_ref, *, heads, groups):
    t = q_ref.shape[2]
    blk = SB_BLOCK
    old = SB_OLD_ROWS
    nq = t // blk
    win = SB_WINDOW_BLOCKS
    wl = win * blk
    ring = z_ref.shape[0]
    first = win - 1
    per_group = nq - first
    units = groups * per_group
    width = heads * HEAD_DIM
    uu = _suffix_neg_ones(blk)
    strict = _strict_mask(blk)
    z_ref[...] = jnp.zeros(z_ref.shape, F32)
    sp_ref[...] = jnp.zeros(sp_ref.shape, BF16)
    suf_ref[...] = jnp.zeros(suf_ref.shape, F32)
    w_ref[...] = jnp.zeros(w_ref.shape, BF16)

    hs = range(groups * heads)
    for i in range(first):
        _, accs = _sb_strips([q_ref[0, h, i * blk:(i + 1) * blk, :] for h in hs],
                             [kt_ref[0, h, :, 0:(i + 1) * blk].astype(BF16) for h in hs],
                             [vt_ref[0, h, :, 0:(i + 1) * blk].astype(BF16) for h in hs],
                             [jnp.zeros((blk, 1), F32)] * len(hs), [jnp.zeros((blk, HEAD_DIM), F32)] * len(hs),
                             uu, [None] * i + [strict])
        o_ref[0, i * blk:(i + 1) * blk, :] = jnp.concatenate(accs, axis=-1).astype(BF16)

    def where_is(n):
        u = jnp.clip(n, 0, units - 1)
        g = u // per_group
        i = first + u - g * per_group
        q0 = pl.multiple_of(i * blk, blk)
        k0 = pl.multiple_of((i - first) * blk, blk)
        return g * heads, q0, k0, lax.rem(n + ring * SB_STAGES, ring)

    def body(n, worst):
        h0, q0, k0, _ = where_is(n - 4)
        accs = []
        for h in range(heads):
            vt = vt_ref[0, h0 + h, :, pl.ds(k0, wl)].astype(BF16)
            accs.append(lax.dot_general(w_ref[h], vt, _NT, preferred_element_type=F32))
        o_ref[0, pl.ds(q0, blk), pl.ds(pl.multiple_of(h0 * HEAD_DIM, width), width)] = (
            jnp.concatenate(accs, axis=-1).astype(BF16))

        _, _, _, slot = where_is(n - 3)
        counts = n - 3 >= 0
        for h in range(heads):
            s_old, s_prev, s_own = suf_ref[h, 0:old], suf_ref[h, old:old + blk], suf_ref[h, old + blk:]
            w_own = jnp.where(strict, jnp.exp(z_ref[slot, h, :, 2 * blk:] + s_own), 0.0)
            carry = s_own[:, 0:1]
            w_prev = jnp.exp(z_ref[slot, h, :, blk:2 * blk] + s_prev + carry)
            carry = carry + s_prev[:, 0:1]
            w_old = jnp.exp(z_ref[slot, h, 0:old, 0:blk] + s_old + carry[0:old])
            w_ref[h, :, 2 * blk:] = w_own.astype(BF16)
            w_ref[h, :, blk:2 * blk] = w_prev.astype(BF16)
            w_ref[h, 0:old, 0:blk] = w_old.astype(BF16)
            edge = jnp.concatenate([carry[0:old] + s_old[:, 0:1], carry[old:]], axis=0)
            worst = jnp.maximum(worst, jnp.where(counts, edge, -jnp.inf))

        for h in range(heads):
            suf_ref[h] = jnp.dot(sp_ref[h], uu, preferred_element_type=F32)

        _, _, _, slot = where_is(n - 1)
        for h in range(heads):
            sp_new = _softplus(z_ref[slot, h, :, blk:])
            sp_ref[h, 0:old] = _softplus(z_ref[slot, h, 0:old, 0:blk]).astype(BF16)
            sp_ref[h, old:old + blk] = sp_new[:, 0:blk].astype(BF16)
            sp_ref[h, old + blk:] = jnp.where(strict, sp_new[:, blk:], 0.0).astype(BF16)

        h0, q0, k0, slot = where_is(n)
        for h in range(heads):
            q = q_ref[0, h0 + h, pl.ds(q0, blk), :]
            kt = kt_ref[0, h0 + h, :, pl.ds(k0, wl)].astype(BF16)
            z_ref[slot, h, :, blk:] = jnp.dot(q, kt[:, blk:], preferred_element_type=F32)
            z_ref[slot, h, 0:old, 0:blk] = jnp.dot(q[0:old], kt[:, 0:blk], preferred_element_type=F32)
        return worst

    worst = lax.fori_loop(0, units + SB_STAGES - 1, body, jnp.full((blk, 1), -jnp.inf, F32))
    flag_ref[...] = jnp.broadcast_to(jnp.max(worst), flag_ref.shape)


def _sb_prompt_full_kernel(q_ref, kt_ref, vt_ref, o_ref, *, heads):
    t = q_ref.shape[2]
    blk = SB_BLOCK
    uu = _suffix_neg_ones(blk)
    strict = _strict_mask(blk)

    def q_block(i, _):
        q0 = pl.multiple_of(i * blk, blk)
        qs = [q_ref[0, h, pl.ds(q0, blk), :] for h in range(heads)]

        def strips(k0, carries, accs, masks):
            return _sb_strips(qs, [kt_ref[0, h, :, pl.ds(k0, blk)].astype(BF16) for h in range(heads)],
                              [vt_ref[0, h, :, pl.ds(k0, blk)].astype(BF16) for h in range(heads)],
                              carries, accs, uu, masks)

        state = strips(q0, [jnp.zeros((blk, 1), F32)] * heads, [jnp.zeros((blk, HEAD_DIM), F32)] * heads,
                       [strict])

        def k_block(jj, st):
            carries, accs = strips(pl.multiple_of((i - 1 - jj) * blk, blk), st[0], st[1], [None])
            return tuple(carries), tuple(accs)

        _, accs = lax.fori_loop(0, i, k_block, (tuple(state[0]), tuple(state[1])))
        o_ref[0, pl.ds(q0, blk), :] = jnp.concatenate(list(accs), axis=-1).astype(BF16)
        return 0

    lax.fori_loop(0, t // blk, q_block, 0)


def _sb_prompt(q, kt, vt):
    b, n_heads, t, _ = q.shape
    wl = SB_WINDOW_BLOCKS * SB_BLOCK
    assert t >= wl
    o_shape = jax.ShapeDtypeStruct((b, t, n_heads * HEAD_DIM), BF16)

    def specs(hb):
        q_blk = pl.BlockSpec((1, hb, t, HEAD_DIM), lambda bi, hg: (bi, hg, 0, 0))
        kv_blk = pl.BlockSpec((1, hb, HEAD_DIM, t), lambda bi, hg: (bi, hg, 0, 0))
        o_blk = pl.BlockSpec((1, t, hb * HEAD_DIM), lambda bi, hg: (bi, 0, hg))
        return (b, n_heads // hb), [q_blk, kv_blk, kv_blk], o_blk

    heads, groups = 4, 2
    rows = SB_OLD_ROWS + 2 * SB_BLOCK
    grid, in_specs, o_blk = specs(heads * groups)
    o_fast, flags = pl.pallas_call(
        functools.partial(_sb_prompt_fast_kernel, heads=heads, groups=groups),
        grid=grid,
        in_specs=in_specs,
        out_specs=[o_blk, pl.BlockSpec((1, 1, 8, 128), lambda bi, hg: (bi, hg, 0, 0))],
        out_shape=[o_shape, jax.ShapeDtypeStruct((b, grid[1], 8, 128), F32)],
        scratch_shapes=[pltpu.VMEM((SB_STAGES - 1, heads, SB_BLOCK, wl), F32),
                        pltpu.VMEM((heads, rows, SB_BLOCK), BF16),
                        pltpu.VMEM((heads, rows, SB_BLOCK), F32),
                        pltpu.VMEM((heads, SB_BLOCK, wl), BF16)],
        compiler_params=_cparams(("parallel", "parallel")),
        name="sb_prompt_fast",
    )(q, kt, vt)

    def full():
        grid, in_specs, o_blk = specs(4)
        return pl.pallas_call(
            functools.partial(_sb_prompt_full_kernel, heads=4),
            grid=grid,
            in_specs=in_specs,
            out_specs=o_blk,
            out_shape=o_shape,
            compiler_params=_cparams(("parallel", "parallel")),
            name="sb_prompt_full",
        )(q, kt, vt)

    return lax.cond(jnp.max(flags) > SB_DEAD, full, lambda: o_fast)


def _sb_sample_kernel(q_ref, ktn_ref, vtn_ref, ktc_ref, vtc_ref, o_ref, *maybe_flag, heads, strip):
    s = q_ref.shape[2]
    width = ktc_ref.shape[4]
    blk = SB_BLOCK
    uu = _suffix_neg_ones(blk)
    uu_new = _suffix_neg_ones(s)
    strict = _strict_mask(s)

    qs = [q_ref[0, h] for h in range(heads)]
    state = _sb_strips(qs, [ktn_ref[0, h].astype(BF16) for h in range(heads)],
                       [vtn_ref[0, h].astype(BF16) for h in range(heads)],
                       [jnp.zeros((s, 1), F32)] * heads, [jnp.zeros((s, HEAD_DIM), F32)] * heads,
                       uu_new, [strict])

    def k_strip(jj, st):
        k0 = pl.multiple_of(width - (jj + 1) * strip * blk, blk)
        carries, accs = _sb_strips(
            qs, [ktc_ref[0, 0, h, :, pl.ds(k0, strip * blk)].astype(BF16) for h in range(heads)],
            [vtc_ref[0, 0, h, :, pl.ds(k0, strip * blk)].astype(BF16) for h in range(heads)],
            st[0], st[1], uu, [None] * strip)
        return tuple(carries), tuple(accs)

    carries, accs = lax.fori_loop(0, width // (strip * blk), k_strip, (tuple(state[0]), tuple(state[1])))
    o_ref[0] = jnp.concatenate(list(accs), axis=-1).astype(BF16)
    if maybe_flag:
        worst = carries[0]
        for h in range(1, heads):
            worst = jnp.maximum(worst, carries[h])
        maybe_flag[0][...] = jnp.broadcast_to(jnp.max(worst), maybe_flag[0].shape)


def _sb_sample(q, kt_new, vt_new, cache_kt, cache_vt, layer):
    b, n_heads, s, _ = q.shape
    past = cache_kt.shape[4]
    width = min(past, SB_SAMPLE_WINDOW)
    assert past % width == 0 and width % SB_BLOCK == 0
    last = past // width - 1
    o_shape = jax.ShapeDtypeStruct((b, s, n_heads * HEAD_DIM), BF16)

    def specs(heads, cache_width, cache_block):
        q_blk = pl.BlockSpec((1, heads, s, HEAD_DIM), lambda bi, hg: (bi, hg, 0, 0))
        new_blk = pl.BlockSpec((1, heads, HEAD_DIM, s), lambda bi, hg: (bi, hg, 0, 0))
        cache_blk = pl.BlockSpec((1, 1, heads, HEAD_DIM, cache_width),
                                 lambda bi, hg: (layer, bi, hg, 0, cache_block))
        o_blk = pl.BlockSpec((1, s, heads * HEAD_DIM), lambda bi, hg: (bi, 0, hg))
        return (b, n_heads // heads), [q_blk, new_blk, new_blk, cache_blk, cache_blk], o_blk

    heads = n_heads
    grid, in_specs, o_blk = specs(heads, width, last)
    o_fast, flags = pl.pallas_call(
        functools.partial(_sb_sample_kernel, heads=heads, strip=width // SB_BLOCK),
        grid=grid,
        in_specs=in_specs,
        out_specs=[o_blk, pl.BlockSpec((1, 1, 8, 128), lambda bi, hg: (bi, hg, 0, 0))],
        out_shape=[o_shape, jax.ShapeDtypeStruct((b, n_heads // heads, 8, 128), F32)],
        compiler_params=_cparams(("parallel", "parallel")),
        name="sb_sample_fast",
    )(q, kt_new, vt_new, cache_kt, cache_vt)
    if width == past:
        return o_fast

    def full():
        heads = 4
        grid, in_specs, o_blk = specs(heads, past, 0)
        return pl.pallas_call(
            functools.partial(_sb_sample_kernel, heads=heads, strip=1),
            grid=grid,
            in_specs=in_specs,
            out_specs=o_blk,
            out_shape=o_shape,
            compiler_params=_cparams(("parallel", "parallel")),
            name="sb_sample_full",
        )(q, kt_new, vt_new, cache_kt, cache_vt)

    return lax.cond(jnp.max(flags) > SB_DEAD, full, lambda: o_fast)


def _route(logits):
    tm = logits.shape[0]
    lane = lax.broadcasted_iota(jnp.int32, (tm, ROUTER_LANES), 1)
    lane_f = lane.astype(F32)
    neg = -jnp.inf
    first = lambda hit: jnp.min(jnp.where(hit, lane_f, float(ROUTER_LANES)), axis=-1, keepdims=True)

    gl = jnp.where(lane < N_GROUPS, logits, neg)
    g_max = jnp.max(gl, axis=-1, keepdims=True)
    g_top = 1.0 / jnp.sum(jnp.exp(gl - g_max), axis=-1, keepdims=True)
    g_idx = first(gl == g_max)

    lo = N_GROUPS + g_idx * EXPERTS_PER_GROUP
    in_group = (lane_f >= lo) & (lane_f < lo + EXPERTS_PER_GROUP)
    sel = jnp.where(in_group, logits, neg)
    t1 = jnp.max(sel, axis=-1, keepdims=True)
    i1 = first(sel == t1)
    sel2 = jnp.where(lane_f == i1, neg, sel)
    t2 = jnp.max(sel2, axis=-1, keepdims=True)
    i2 = first(sel2 == t2)
    e2 = jnp.exp(t2 - t1)
    den = 1.0 + e2
    w1 = (1.0 / den) * g_top
    w2 = (e2 / den) * g_top
    return jnp.where(lane_f == i1, w1, 0.0) + jnp.where(lane_f == i2, w2, 0.0)


CHANNEL_ROW_SPLITS = 2
EXPERTS_PER_TRIP = 16


def _channel_kernel(x_ref, o_ref, p_ref, wo_ref, gffn_ref, wr_ref, wg_ref, wu_ref, wd_ref, gple_ref,
                    wpg_ref, wpp_ref, *rest, n_next):
    next_in, (out_ref, *next_out) = rest[:n_next], rest[n_next:]
    acc_ref = out_ref
    tm = x_ref.shape[0]
    d_expert = wd_ref.shape[2]
    half = tm // CHANNEL_ROW_SPLITS
    rows = [slice(i * half, (i + 1) * half) for i in range(CHANNEL_ROW_SPLITS)]

    h1s = [x_ref[r, :] + jnp.dot(o_ref[r, :], wo_ref[...], preferred_element_type=F32) for r in rows]
    xns = [_rms_unit(h1) * gffn_ref[...] for h1 in h1s]
    xnbs = [xn.astype(BF16) for xn in xns]
    xlos = [(xn - xnb.astype(F32)).astype(BF16) for xn, xnb in zip(xns, xnbs)]
    logits = []
    for xnb, xlo in zip(xnbs, xlos):
        hi_lo = jnp.dot(xnb, wr_ref[...], preferred_element_type=F32)
        lo_hi = jnp.dot(xlo, wr_ref[:, 0:ROUTER_LANES], preferred_element_type=F32)
        logits.append(hi_lo[:, 0:ROUTER_LANES] + hi_lo[:, ROUTER_LANES:] + lo_hi)
    gates = jnp.concatenate([_route(l) for l in logits], axis=0)
    xnb = jnp.concatenate(xnbs, axis=0)
    lane = lax.broadcasted_iota(jnp.int32, gates.shape, 1)
    for r, h1 in zip(rows, h1s):
        acc_ref[r, :] = h1

    def experts(j, _):
        es = [j * EXPERTS_PER_TRIP + k for k in range(EXPERTS_PER_TRIP)]
        gs = [jnp.dot(xnb, wg_ref[0, e], preferred_element_type=F32) for e in es]
        us = [jnp.dot(xnb, wu_ref[0, e], preferred_element_type=F32) for e in es]
        hids = []
        for e, g, u in zip(es, gs, us):
            gate_e = jnp.sum(jnp.where(lane == N_GROUPS + e, gates, 0.0), axis=-1, keepdims=True)
            hids.append(((g * jax.nn.sigmoid(g)) * u * gate_e).astype(BF16))
        wd = wd_ref[0, pl.ds(j * EXPERTS_PER_TRIP, EXPERTS_PER_TRIP)]
        acc_ref[...] += jnp.dot(jnp.concatenate(hids, axis=-1),
                                wd.reshape(EXPERTS_PER_TRIP * d_expert, wd.shape[-1]),
                                preferred_element_type=F32)
        return 0

    lax.fori_loop(0, N_EXPERTS // EXPERTS_PER_TRIP, experts, 0)
    h2s = [acc_ref[r, :] for r in rows]
    x3s = [(_rms_unit(h2) * gple_ref[...]).astype(BF16) for h2 in h2s]
    ple_gates = [jax.nn.sigmoid(jnp.dot(x3, wpg_ref[0], preferred_element_type=F32)) for x3 in x3s]
    projs = [jnp.dot(p_ref[0, r, :].astype(BF16), wpp_ref[0], preferred_element_type=F32) for r in rows]
    h3s = [h2 + proj * gate for h2, proj, gate in zip(h2s, projs, ple_gates)]
    for r, h3 in zip(rows, h3s):
        out_ref[r, :] = h3
    if n_next:
        _store_proj_b(rows, _proj_b_rows(h3s, *next_in), *next_out)


def _channel(x, o, p, layer, w, stacks, next_proj=None, batch=1):
    n, d = x.shape
    tm = min(n, 512)
    row = lambda cols: pl.BlockSpec((tm, cols), lambda i: (i, 0))
    small = [w["wo"], w["gffn"], w["wr"]]
    moe = [stacks["wg"], stacks["wu"], stacks["wd"]]
    ple = [stacks["wpg"], stacks["wpp"]]
    next_proj = list(next_proj or [])
    out_specs = [row(d)]
    out_shape = [jax.ShapeDtypeStruct((n, d), F32)]
    if next_proj:
        t = n // batch
        per_entry = t // tm
        assert t % tm == 0
        n_heads = next_proj[1].shape[1] // HEAD_DIM
        n_kv = next_proj[4].shape[1] // (2 * HEAD_DIM)
        hm = lambda i: (i // per_entry, 0, i % per_entry, 0)
        for heads, dtype in ((n_heads, BF16), (n_kv, F32), (n_kv, F32)):
            out_specs.append(pl.BlockSpec((1, heads, tm, HEAD_DIM), hm))
            out_shape.append(jax.ShapeDtypeStruct((batch, heads, t, HEAD_DIM), dtype))
    outs = pl.pallas_call(
        functools.partial(_channel_kernel, n_next=len(next_proj)),
        grid=(n // tm,),
        in_specs=([row(d), row(d), pl.BlockSpec((1, tm, p.shape[2]), lambda i: (layer, i, 0))]
                  + [_resident(a.shape) for a in small] + [_resident_layer(a.shape, layer) for a in moe]
                  + [_resident(w["gple"].shape)] + [_resident_layer(a.shape, layer) for a in ple]
                  + [_resident(a.shape) for a in next_proj]),
        out_specs=out_specs,
        out_shape=out_shape,
        compiler_params=_cparams(("parallel",)),
        name="channel",
    )(x, o, p, *small, *moe, w["gple"], *ple, *next_proj)
    return outs if next_proj else outs[0]


def _proj_b_rows(hs, ga_ref, wq_ref, gq_ref, gkv_ref, wkv_ref, gk_ref, gsum_ref, gexp_ref):
    n_kv = wkv_ref.shape[1] // (2 * HEAD_DIM)
    ys = [_rms_unit(h) for h in hs]
    qs = [jnp.dot((y * ga_ref[...]).astype(BF16), wq_ref[...], preferred_element_type=F32) for y in ys]
    kvs = [jnp.dot((y * gkv_ref[...]).astype(BF16), wkv_ref[...], preferred_element_type=F32) for y in ys]
    mss = [jnp.dot((q * q).astype(BF16), gsum_ref[...], preferred_element_type=F32) * (1.0 / HEAD_DIM)
           for q in qs]
    inv_fulls = [jnp.dot(_split_hl(lax.rsqrt(ms + EPS)), gexp_ref[...], preferred_element_type=F32)
                 for ms in mss]
    out = []
    for q, inv_full, kv in zip(qs, inv_fulls, kvs):
        qn = (q * inv_full * gq_ref[...]) * (HEAD_DIM ** -0.5)
        ks = [_rms_unit(kv[:, h * HEAD_DIM:(h + 1) * HEAD_DIM]) * gk_ref[...] for h in range(n_kv)]
        vs = [kv[:, (n_kv + h) * HEAD_DIM:(n_kv + h + 1) * HEAD_DIM] for h in range(n_kv)]
        out.append((qn, ks, vs))
    return out


def _store_proj_b(rows, projected, q_ref, k_ref, v_ref):
    for r, (qn, ks, vs) in zip(rows, projected):
        for h in range(q_ref.shape[1]):
            q_ref[0, h, r, :] = qn[:, h * HEAD_DIM:(h + 1) * HEAD_DIM].astype(BF16)
        for h in range(k_ref.shape[1]):
            k_ref[0, h, r, :] = ks[h]
            v_ref[0, h, r, :] = vs[h]


def _t5_bucket(rel):
    nb = NUM_BUCKETS // 2
    max_exact = nb // 2
    n = jnp.abs(rel)
    large = max_exact + (jnp.log(jnp.maximum(n, 1).astype(jnp.float32) / max_exact)
                         / math.log(MAX_DISTANCE / max_exact) * (nb - max_exact)).astype(jnp.int32)
    large = jnp.minimum(large, nb - 1)
    return jnp.where(rel > 0, nb, 0) + jnp.where(n < max_exact, n, large)


def _bias_kernel(bucket_ref, rbt_ref, out_ref):
    bucket = bucket_ref[...].astype(F32)
    guard = 2.0 ** -10
    acc = jnp.zeros(out_ref.shape, F32)
    for b in range(NUM_BUCKETS):
        hit = (bucket >= b - guard) & (bucket < b + 1 - guard)
        acc = acc + jnp.where(hit, rbt_ref[:, b:b + 1], 0.0)
    out_ref[...] = jnp.where(bucket < 0, -jnp.inf, acc)


def _bias_table(rel_bias, nq, nk, key_offset, chunk_window):
    qi = jnp.arange(nq, dtype=jnp.int32)[:, None]
    kj = jnp.arange(nk, dtype=jnp.int32)[None, :] - key_offset
    bucket = _t5_bucket(kj - qi)
    if chunk_window:
        qc, kc = qi // CHUNK, kj // CHUNK
        bucket = jnp.where((kc <= qc) & (kc >= qc - WIN_CHUNKS), bucket, -1)
    bucket = bucket.reshape(1, nq * nk)
    n_heads = rel_bias.shape[1]
    out = pl.pallas_call(
        _bias_kernel,
        out_shape=jax.ShapeDtypeStruct((n_heads, nq * nk), F32),
        name="bias_table",
    )(bucket, rel_bias.T)
    return out.reshape(n_heads, nq, nk)


SWA_ROWS_PER_TRIP = 128


def _swa_kernel(sink_ref, q_ref, k_ref, v_ref, bias_ref, o_ref, kpad_ref, vpad_ref, *, cq, wl, pad, n_chunks,
                group):
    n_kv, tk = k_ref.shape[1], k_ref.shape[2]
    if pad:
        kpad_ref[:, 0:pad, :] = jnp.zeros((n_kv, pad, HEAD_DIM), BF16)
        vpad_ref[:, 0:pad, :] = jnp.zeros((n_kv, pad, HEAD_DIM), BF16)
    kpad_ref[:, pad:pad + tk, :] = k_ref[0].astype(BF16)
    vpad_ref[:, pad:pad + tk, :] = v_ref[0].astype(BF16)

    per_trip = max(1, SWA_ROWS_PER_TRIP // cq)
    per_trip = per_trip if n_chunks % per_trip == 0 else 1
    n_heads = n_kv * group

    def chunks(c, masked):
        r0s = [pl.multiple_of((c * per_trip + i) * cq, cq) for i in range(per_trip)]
        logits = [[lax.dot_general(
            q_ref[0, kv * group:(kv + 1) * group, pl.ds(r0, cq), :].reshape(group * cq, HEAD_DIM),
            kpad_ref[kv, pl.ds(r0, wl), :], _NT, preferred_element_type=F32)
            for kv in range(n_kv)] for r0 in r0s]
        es, dens = [], []
        for r0, lg in zip(r0s, logits):
            valid = lax.broadcasted_iota(jnp.int32, (cq, wl), 1) + r0 >= pad
            for h in range(n_heads):
                l = lg[h // group][(h % group) * cq:(h % group + 1) * cq] + bias_ref[h]
                if masked:
                    l = jnp.where(valid, l, -jnp.inf)
                sink = sink_ref[h]
                m = jnp.maximum(jnp.max(l, axis=-1, keepdims=True), sink)
                e = jnp.exp(l - m)
                dens.append(jnp.sum(e, axis=-1, keepdims=True) + jnp.exp(sink - m))
                es.append(e.astype(BF16))
        pvs = [[jnp.dot(jnp.concatenate(es[i * n_heads + kv * group:i * n_heads + (kv + 1) * group], axis=0),
                        vpad_ref[kv, pl.ds(r0, wl), :], preferred_element_type=F32)
                for kv in range(n_kv)] for i, r0 in enumerate(r0s)]
        for i, r0 in enumerate(r0s):
            outs = [pvs[i][h // group][(h % group) * cq:(h % group + 1) * cq] / dens[i * n_heads + h]
                    for h in range(n_heads)]
            o_ref[0, pl.ds(r0, cq), :] = jnp.concatenate(outs, axis=-1).astype(BF16)
        return 0

    trips = n_chunks // per_trip
    masked_trips = min(trips, -(-pad // (cq * per_trip)))
    for c in range(masked_trips):
        chunks(c, True)
    lax.fori_loop(masked_trips, trips, lambda c, _: chunks(c, False), 0)


def _swa(q, k_win, v_win, bias, sinks, *, cq, wl, pad):
    b, n_heads, tq, _ = q.shape
    n_kv, tk = k_win.shape[1], k_win.shape[2]
    n_chunks = tq // cq
    assert (n_chunks - 1) * cq + wl == pad + tk
    kv_blk = pl.BlockSpec((1, n_kv, tk, HEAD_DIM), lambda bi: (bi, 0, 0, 0))
    return pl.pallas_call(
        functools.partial(_swa_kernel, cq=cq, wl=wl, pad=pad, n_chunks=n_chunks, group=n_heads // n_kv),
        grid=(b,),
        in_specs=[pl.BlockSpec(memory_space=pltpu.SMEM),
                  pl.BlockSpec((1, n_heads, tq, HEAD_DIM), lambda bi: (bi, 0, 0, 0)),
                  kv_blk, kv_blk, _resident(bias.shape)],
        out_specs=pl.BlockSpec((1, tq, n_heads * HEAD_DIM), lambda bi: (bi, 0, 0)),
        out_shape=jax.ShapeDtypeStruct((b, tq, n_heads * HEAD_DIM), BF16),
        scratch_shapes=[pltpu.VMEM((n_kv, pad + tk, HEAD_DIM), BF16),
                        pltpu.VMEM((n_kv, pad + tk, HEAD_DIM), BF16)],
        compiler_params=_cparams(("parallel",)),
        name="swa",
    )(sinks, q, k_win, v_win, bias)


def _row(v):
    return v.reshape(1, -1).astype(F32)


def _router_hi_lo(wr):
    hi = wr.astype(BF16)
    lo = (wr - hi.astype(F32)).astype(BF16)
    return jnp.concatenate([hi, lo], axis=1)


def _prep_weights(prm):
    d = prm["a_w_o"].shape[1]
    depth = prm["norm_ffn"].shape[0]
    n_a = prm["a_w_qkv"].shape[0]
    scale = HEAD_DIM ** -0.5
    w = {"channel": [], "n_a": n_a, "depth": depth}
    for i in range(depth):
        wo = prm["a_w_o"][i] if i < n_a else prm["b_w_o"][i - n_a]
        pad = ROUTER_LANES - N_GROUPS - N_EXPERTS
        wr = jnp.concatenate([prm["moe_w_group"][i], prm["moe_w_router"][i].reshape(d, N_EXPERTS),
                              jnp.zeros((d, pad), F32)], axis=1)
        w["channel"].append({"wo": wo.astype(BF16), "gffn": _row(prm["norm_ffn"][i]),
                             "wr": _router_hi_lo(wr), "gple": _row(prm["norm_ple"][i])})
    w["stacks"] = {"wg": prm["moe_w_gate"].astype(BF16), "wu": prm["moe_w_up"].astype(BF16),
                   "wd": prm["moe_w_down"].astype(BF16), "wpg": prm["ple_w_gate"].astype(BF16),
                   "wpp": prm["ple_w_proj"].astype(BF16)}
    w["qkv"] = []
    for i in range(n_a):
        wq = prm["a_w_qkv"][i]
        hd = wq.shape[1] // 3
        w["qkv"].append(((wq[:, :hd] * scale).astype(BF16),
                         wq[:, hd:].T.astype(BF16)))
    n_heads = prm["b_w_q"].shape[2] // HEAD_DIM
    head_of = jnp.arange(n_heads * HEAD_DIM, dtype=jnp.int32) // HEAD_DIM
    lanes = jnp.arange(ROUTER_LANES, dtype=jnp.int32)
    member = (head_of[:, None] == lanes[None, :]).astype(BF16)
    w["gsum"] = member
    w["gexp"] = jnp.concatenate([member.T, member.T], axis=0)
    w["wq_b"] = [prm["b_w_q"][j].astype(BF16) for j in range(depth - n_a)]
    w["gq_b"] = [_row(jnp.tile(prm["b_q_norm"][j], n_heads)) for j in range(depth - n_a)]
    w["wkv"] = prm["b_w_kv"].astype(BF16)
    return w


def _assert_sample_window_visible(past_len, s, tk):
    q_chunk = [(past_len + i) // CHUNK for i in range(s)]
    k_pos = [past_len + s - tk + j for j in range(tk)]
    ok = all(kp >= 0 and qc - WIN_CHUNKS <= kp // CHUNK <= qc for qc in q_chunk for kp in k_pos)
    if not ok:
        raise NotImplementedError("sample window with masked keys")


def _fold(x):
    return x.reshape(1, -1, x.shape[-1])


def _unfold_rows(a, b):
    _, n_heads, rows, dh = a.shape
    return a.reshape(n_heads, b, rows // b, dh).transpose(1, 0, 2, 3)


def _unfold_cols(a, b):
    _, n_heads, dh, rows = a.shape
    return a.reshape(n_heads, dh, b, rows // b).transpose(2, 0, 1, 3)


def _run_trunk(x, p, prm, w, sb_cache_k=None, sb_cache_v=None, swa_cache_k=None, swa_cache_v=None):
    bx, t, d = x.shape
    n_a, depth = w["n_a"], w["depth"]
    sample = sb_cache_k is not None
    h = x
    sb_k, sb_v = [], []
    k_win = v_win = None
    q_b = None
    for i in range(depth):
        if i < n_a:
            if sample:
                q, kt, vt = _proj_a(_fold(h), _row(prm["norm_attn"][i]), *w["qkv"][i])
                q, kt, vt = _unfold_rows(q, bx), _unfold_cols(kt, bx), _unfold_cols(vt, bx)
            else:
                q, kt, vt = _proj_a(h, _row(prm["norm_attn"][i]), *w["qkv"][i])
            sb_k.append(kt)
            sb_v.append(vt)
            if sample:
                o = _sb_sample(q, kt, vt, jnp.swapaxes(sb_cache_k, -1, -2), jnp.swapaxes(sb_cache_v, -1, -2), i)
            else:
                o = _sb_prompt(q, kt, vt)
        else:
            j = i - n_a
            if j > 0:
                raise NotImplementedError("one B layer supported")
            if sample:
                tk = k_win.shape[2]
                _assert_sample_window_visible(sb_cache_k.shape[3], t, tk)
                o = _swa(q_b, k_win, v_win, _bias_table(prm["rel_bias"], t, tk, tk - t, False),
                         prm["b_sinks"][j], cq=t, wl=tk, pad=0)
            else:
                cq = 2 * CHUNK
                wl = WINDOW + cq
                assert t % cq == 0
                o = _swa(q_b, k_win, v_win, _bias_table(prm["rel_bias"], cq, wl, WINDOW, True),
                         prm["b_sinks"][j], cq=cq, wl=wl, pad=WINDOW)
        flat = (h.reshape(bx * t, d), o.reshape(bx * t, d), p.reshape(depth, bx * t, -1), i,
                w["channel"][i], w["stacks"])
        if i != n_a - 1:
            h = _channel(*flat).reshape(bx, t, d)
        else:
            next_proj = [_row(prm["norm_attn"][n_a]), w["wq_b"][0], w["gq_b"][0], _row(prm["kv_norm"]), w["wkv"],
                         _row(prm["b_k_norm"]), w["gsum"], w["gexp"]]
            h, q_b, k_s, v_s = _channel(*flat, next_proj=next_proj, batch=1 if sample else bx)
            h = h.reshape(bx, t, d)
            if sample:
                q_b, k_s, v_s = _unfold_rows(q_b, bx), _unfold_rows(k_s, bx), _unfold_rows(v_s, bx)
                k_win = jnp.concatenate([swa_cache_k, k_s], axis=2)
                v_win = jnp.concatenate([swa_cache_v, v_s], axis=2)
            else:
                k_win, v_win = k_s, v_s
    sb_k = jnp.swapaxes(jnp.stack(sb_k), -1, -2)
    sb_v = jnp.swapaxes(jnp.stack(sb_v), -1, -2)
    return h, sb_k, sb_v, k_win[:, :, -WINDOW:], v_win[:, :, -WINDOW:]


def kernel(x_prompt, x_sample, p_prompt, p_sample, cache_sb_k, cache_sb_v, cache_swa_k, cache_swa_v, norm_attn, norm_ffn, norm_ple, a_w_qkv, a_w_o, kv_norm, b_w_kv, b_k_norm, b_w_q, b_q_norm, b_sinks, b_w_o, rel_bias, moe_w_group, moe_w_router, moe_w_gate, moe_w_up, moe_w_down, ple_w_proj, ple_w_gate):
    prm = {
        "norm_attn": norm_attn, "norm_ffn": norm_ffn, "norm_ple": norm_ple,
        "a_w_qkv": a_w_qkv, "a_w_o": a_w_o, "kv_norm": kv_norm, "b_w_kv": b_w_kv, "b_k_norm": b_k_norm,
        "b_w_q": b_w_q, "b_q_norm": b_q_norm, "b_sinks": b_sinks, "b_w_o": b_w_o, "rel_bias": rel_bias,
        "moe_w_group": moe_w_group, "moe_w_router": moe_w_router, "moe_w_gate": moe_w_gate,
        "moe_w_up": moe_w_up, "moe_w_down": moe_w_down, "ple_w_proj": ple_w_proj, "ple_w_gate": ple_w_gate,
    }
    w = _prep_weights(prm)
    y_p, sb_k_p, sb_v_p, swa_k_p, swa_v_p = _run_trunk(x_prompt, p_prompt, prm, w)
    y_s, sb_k_s, sb_v_s, swa_k_s, swa_v_s = _run_trunk(x_sample, p_sample, prm, w, cache_sb_k, cache_sb_v,
                                                       cache_swa_k, cache_swa_v)
    return (y_p, y_s, sb_k_p, sb_v_p, swa_k_p, swa_v_p, sb_k_s, sb_v_s, swa_k_s, swa_v_s)
```

```python
import functools
import math

import jax
import jax.numpy as jnp
from jax import lax
from jax.experimental import pallas as pl
from jax.experimental.pallas import tpu as pltpu

F32 = jnp.float32
BF16 = jnp.bfloat16

HEAD_DIM = 64
CHUNK = 64
WINDOW = 128
WIN_CHUNKS = WINDOW // CHUNK
NUM_BUCKETS = 32
MAX_DISTANCE = 128
N_GROUPS = 4
EXPERTS_PER_GROUP = 4
N_EXPERTS = N_GROUPS * EXPERTS_PER_GROUP
EPS = 1e-6
LOG2E = 1.4426950408889634
SB_BLOCK = 128
ROUTER_LANES = 128
VMEM_LIMIT = 63 * 1024 * 1024

_NT = (((1,), (1,)), ((), ()))


def _cparams(sem):
    return pltpu.CompilerParams(dimension_semantics=sem, vmem_limit_bytes=VMEM_LIMIT)


def _rms_unit(x):
    return x * lax.rsqrt(jnp.mean(x * x, axis=-1, keepdims=True) + EPS)


def _split_hl(a):
    hi = a.astype(BF16)
    lo = (a - hi.astype(F32)).astype(BF16)
    return jnp.concatenate([hi, lo], axis=-1)


def _resident(shape):
    nd = len(shape)
    return pl.BlockSpec(shape, lambda *_: (0,) * nd, pipeline_mode=pl.Buffered(1))


def _resident_layer(shape, layer):
    nd = len(shape)
    return pl.BlockSpec((1,) + tuple(shape[1:]), lambda *_: (layer,) + (0,) * (nd - 1),
                        pipeline_mode=pl.Buffered(1))


def _proj_a_kernel(x_ref, g_ref, wq_ref, wkvt_ref, q_ref, kt_ref, vt_ref, *, n_heads, pack_q):
    xn = (_rms_unit(x_ref[0]) * g_ref[...]).astype(BF16)
    heads_per_dot = 4
    width = heads_per_dot * HEAD_DIM
    per_store = 2 if pack_q else 1
    for c in range(n_heads // heads_per_dot):
        r = jnp.dot(xn, wq_ref[:, c * width:(c + 1) * width], preferred_element_type=F32)
        for s in range(heads_per_dot // per_store):
            lanes = slice(s * per_store * HEAD_DIM, (s + 1) * per_store * HEAD_DIM)
            q_ref[0, c * heads_per_dot // per_store + s] = r[:, lanes].astype(BF16)
    for c in range(2 * n_heads // heads_per_dot):
        r = lax.dot_general(wkvt_ref[c * width:(c + 1) * width, :], xn, _NT, preferred_element_type=F32)
        for hh in range(heads_per_dot):
            h = c * heads_per_dot + hh
            piece = r[hh * HEAD_DIM:(hh + 1) * HEAD_DIM, :]
            if h < n_heads:
                kt_ref[0, h] = piece
            else:
                vt_ref[0, h - n_heads] = piece


def _proj_a(x, g, wq, wkvt, pack_q=False):
    bx, t, d = x.shape
    n_heads = wq.shape[1] // HEAD_DIM
    tm = min(t, 512)
    q_slabs, q_width = (n_heads // 2, 2 * HEAD_DIM) if pack_q else (n_heads, HEAD_DIM)
    return pl.pallas_call(
        functools.partial(_proj_a_kernel, n_heads=n_heads, pack_q=pack_q),
        grid=(bx, t // tm),
        in_specs=[pl.BlockSpec((1, tm, d), lambda b, i: (b, i, 0)),
                  _resident((1, d)), _resident(wq.shape), _resident(wkvt.shape)],
        out_specs=[pl.BlockSpec((1, q_slabs, tm, q_width), lambda b, i: (b, 0, i, 0)),
                   pl.BlockSpec((1, n_heads, HEAD_DIM, tm), lambda b, i: (b, 0, 0, i)),
                   pl.BlockSpec((1, n_heads, HEAD_DIM, tm), lambda b, i: (b, 0, 0, i))],
        out_shape=[jax.ShapeDtypeStruct((bx, q_slabs, t, q_width), BF16),
                   jax.ShapeDtypeStruct((bx, n_heads, HEAD_DIM, t), F32),
                   jax.ShapeDtypeStruct((bx, n_heads, HEAD_DIM, t), F32)],
        compiler_params=_cparams(("parallel", "parallel")),
        name="proj_a",
    )(x, g, wq, wkvt)


SB_DEAD = -104.0
SB_WINDOW_BLOCKS = 3
SB_SAMPLE_WINDOW = 256


def _suffix_neg_ones(n):
    r = lax.broadcasted_iota(jnp.int32, (n, n), 0)
    c = lax.broadcasted_iota(jnp.int32, (n, n), 1)
    return jnp.where(r >= c, -1.0, 0.0).astype(BF16)


def _strict_mask(n):
    row = lax.broadcasted_iota(jnp.int32, (n, n), 0)
    col = lax.broadcasted_iota(jnp.int32, (n, n), 1)
    return col < row


def _pair_rows(kt, half):
    zeros = jnp.zeros_like(kt)
    return jnp.concatenate([kt, zeros] if half == 0 else [zeros, kt], axis=0)


def _softplus(z):
    return jnp.maximum(z, 0.0) + jnp.log(1.0 + jnp.exp2(jnp.abs(z) * -LOG2E))


def _sb_strips(qs, kts, vts, carries, accs, uu, masks):
    n = len(qs)
    blk = uu.shape[1]
    nb = kts[0].shape[1] // blk
    zs = [jnp.dot(qs[i], kts[i], preferred_element_type=F32) for i in range(n)]
    sps = [_softplus(z) for z in zs]
    ws = [[None] * nb for _ in range(n)]
    carries = list(carries)
    for b in reversed(range(nb)):
        sl = slice(b * blk, (b + 1) * blk)
        spbs = [sp[:, sl] if masks[b] is None else jnp.where(masks[b], sp[:, sl], 0.0) for sp in sps]
        sufs = [jnp.dot(spb.astype(BF16), uu, preferred_element_type=F32) for spb in spbs]
        for i in range(n):
            wb = jnp.exp(zs[i][:, sl] + sufs[i] + carries[i])
            if masks[b] is not None:
                wb = jnp.where(masks[b], wb, 0.0)
            ws[i][b] = wb.astype(BF16)
            carries[i] = carries[i] + sufs[i][:, 0:1]
    wcat = [w[0] if nb == 1 else jnp.concatenate(w, axis=-1) for w in ws]
    accs = [accs[i] + lax.dot_general(wcat[i], vts[i], _NT, preferred_element_type=F32) for i in range(n)]
    return carries, accs


SB_STAGES = 5
SB_OLD_ROWS = 64


def _sb_prompt_fast_kernel(q_ref, kt_ref, vt_ref, o_ref, flag_ref, z_ref, sp_ref, suf_ref, w_ref, *, heads, groups):
    t = q_ref.shape[2]
    blk = SB_BLOCK
    old = SB_OLD_ROWS
    nq = t // blk
    win = SB_WINDOW_BLOCKS
    wl = win * blk
    ring = z_ref.shape[0]
    first = win - 1
    per_group = nq - first
    units = groups * per_group
    width = heads * HEAD_DIM
    uu = _suffix_neg_ones(blk)
    strict = _strict_mask(blk)
    z_ref[...] = jnp.zeros(z_ref.shape, F32)
    sp_ref[...] = jnp.zeros(sp_ref.shape, BF16)
    suf_ref[...] = jnp.zeros(suf_ref.shape, F32)
    w_ref[...] = jnp.zeros(w_ref.shape, BF16)

    hs = range(groups * heads)
    for i in range(first):
        _, accs = _sb_strips([q_ref[0, h // 2, i * blk:(i + 1) * blk, :] for h in hs],
                             [_pair_rows(kt_ref[0, h, :, 0:(i + 1) * blk].astype(BF16), h % 2) for h in hs],
                             [vt_ref[0, h, :, 0:(i + 1) * blk].astype(BF16) for h in hs],
                             [jnp.zeros((blk, 1), F32)] * len(hs), [jnp.zeros((blk, HEAD_DIM), F32)] * len(hs),
                             uu, [None] * i + [strict])
        o_ref[0, i * blk:(i + 1) * blk, :] = jnp.concatenate(accs, axis=-1).astype(BF16)

    def where_is(n):
        u = jnp.clip(n, 0, units - 1)
        g = u // per_group
        i = first + u - g * per_group
        q0 = pl.multiple_of(i * blk, blk)
        k0 = pl.multiple_of((i - first) * blk, blk)
        return g * heads, q0, k0, lax.rem(n + ring * SB_STAGES, ring)

    def body(n, worst):
        h0, q0, k0, _ = where_is(n - 4)
        accs = []
        for h in range(heads):
            vt = vt_ref[0, h0 + h, :, pl.ds(k0, wl)].astype(BF16)
            accs.append(lax.dot_general(w_ref[h], vt, _NT, preferred_element_type=F32))
        o_ref[0, pl.ds(q0, blk), pl.ds(pl.multiple_of(h0 * HEAD_DIM, width), width)] = (
            jnp.concatenate(accs, axis=-1).astype(BF16))

        _, _, _, slot = where_is(n - 3)
        counts = n - 3 >= 0
        for h in range(heads):
            s_old, s_prev, s_own = suf_ref[h, 0:old], suf_ref[h, old:old + blk], suf_ref[h, old + blk:]
            w_own = jnp.where(strict, jnp.exp(z_ref[slot, h, :, 2 * blk:] + s_own), 0.0)
            carry = s_own[:, 0:1]
            w_prev = jnp.exp(z_ref[slot, h, :, blk:2 * blk] + s_prev + carry)
            carry = carry + s_prev[:, 0:1]
            w_old = jnp.exp(z_ref[slot, h, 0:old, 0:blk] + s_old + carry[0:old])
            w_ref[h, :, 2 * blk:] = w_own.astype(BF16)
            w_ref[h, :, blk:2 * blk] = w_prev.astype(BF16)
            w_ref[h, 0:old, 0:blk] = w_old.astype(BF16)
            edge = jnp.concatenate([carry[0:old] + s_old[:, 0:1], carry[old:]], axis=0)
            worst = jnp.maximum(worst, jnp.where(counts, edge, -jnp.inf))

        for h in range(heads):
            suf_ref[h] = jnp.dot(sp_ref[h], uu, preferred_element_type=F32)

        _, _, _, slot = where_is(n - 1)
        for h in range(heads):
            sp_new = _softplus(z_ref[slot, h, :, blk:])
            sp_ref[h, 0:old] = _softplus(z_ref[slot, h, 0:old, 0:blk]).astype(BF16)
            sp_ref[h, old:old + blk] = sp_new[:, 0:blk].astype(BF16)
            sp_ref[h, old + blk:] = jnp.where(strict, sp_new[:, blk:], 0.0).astype(BF16)

        h0, q0, k0, slot = where_is(n)
        for h in range(heads):
            q = q_ref[0, h0 // 2 + h // 2, pl.ds(q0, blk), :]
            kt = _pair_rows(kt_ref[0, h0 + h, :, pl.ds(k0, wl)].astype(BF16), h % 2)
            z_ref[slot, h, :, blk:] = jnp.dot(q, kt[:, blk:], preferred_element_type=F32)
            z_ref[slot, h, 0:old, 0:blk] = jnp.dot(q[0:old], kt[:, 0:blk], preferred_element_type=F32)
        return worst

    worst = lax.fori_loop(0, units + SB_STAGES - 1, body, jnp.full((blk, 1), -jnp.inf, F32))
    flag_ref[...] = jnp.broadcast_to(jnp.max(worst), flag_ref.shape)


def _sb_prompt_full_kernel(q_ref, kt_ref, vt_ref, o_ref, *, heads):
    t = q_ref.shape[2]
    blk = SB_BLOCK
    uu = _suffix_neg_ones(blk)
    strict = _strict_mask(blk)

    def q_block(i, _):
        q0 = pl.multiple_of(i * blk, blk)
        qs = [q_ref[0, h // 2, pl.ds(q0, blk), :] for h in range(heads)]

        def strips(k0, carries, accs, masks):
            return _sb_strips(qs, [_pair_rows(kt_ref[0, h, :, pl.ds(k0, blk)].astype(BF16), h % 2)
                                   for h in range(heads)],
                              [vt_ref[0, h, :, pl.ds(k0, blk)].astype(BF16) for h in range(heads)],
                              carries, accs, uu, masks)

        state = strips(q0, [jnp.zeros((blk, 1), F32)] * heads, [jnp.zeros((blk, HEAD_DIM), F32)] * heads,
                       [strict])

        def k_block(jj, st):
            carries, accs = strips(pl.multiple_of((i - 1 - jj) * blk, blk), st[0], st[1], [None])
            return tuple(carries), tuple(accs)

        _, accs = lax.fori_loop(0, i, k_block, (tuple(state[0]), tuple(state[1])))
        o_ref[0, pl.ds(q0, blk), :] = jnp.concatenate(list(accs), axis=-1).astype(BF16)
        return 0

    lax.fori_loop(0, t // blk, q_block, 0)


def _sb_prompt(q, kt, vt):
    b, n_heads, _, t = kt.shape
    wl = SB_WINDOW_BLOCKS * SB_BLOCK
    assert t >= wl
    o_shape = jax.ShapeDtypeStruct((b, t, n_heads * HEAD_DIM), BF16)

    def specs(hb):
        q_blk = pl.BlockSpec((1, hb // 2, t, 2 * HEAD_DIM), lambda bi, hg: (bi, hg, 0, 0))
        kv_blk = pl.BlockSpec((1, hb, HEAD_DIM, t), lambda bi, hg: (bi, hg, 0, 0))
        o_blk = pl.BlockSpec((1, t, hb * HEAD_DIM), lambda bi, hg: (bi, 0, hg))
        return (b, n_heads // hb), [q_blk, kv_blk, kv_blk], o_blk

    heads, groups = 4, 4
    rows = SB_OLD_ROWS + 2 * SB_BLOCK
    grid, in_specs, o_blk = specs(heads * groups)
    o_fast, flags = pl.pallas_call(
        functools.partial(_sb_prompt_fast_kernel, heads=heads, groups=groups),
        grid=grid,
        in_specs=in_specs,
        out_specs=[o_blk, pl.BlockSpec((1, 1, 8, 128), lambda bi, hg: (bi, hg, 0, 0))],
        out_shape=[o_shape, jax.ShapeDtypeStruct((b, grid[1], 8, 128), F32)],
        scratch_shapes=[pltpu.VMEM((SB_STAGES - 1, heads, SB_BLOCK, wl), F32),
                        pltpu.VMEM((heads, rows, SB_BLOCK), BF16),
                        pltpu.VMEM((heads, rows, SB_BLOCK), F32),
                        pltpu.VMEM((heads, SB_BLOCK, wl), BF16)],
        compiler_params=_cparams(("parallel", "parallel")),
        name="sb_prompt_fast",
    )(q, kt, vt)

    def full():
        grid, in_specs, o_blk = specs(4)
        return pl.pallas_call(
            functools.partial(_sb_prompt_full_kernel, heads=4),
            grid=grid,
            in_specs=in_specs,
            out_specs=o_blk,
            out_shape=o_shape,
            compiler_params=_cparams(("parallel", "parallel")),
            name="sb_prompt_full",
        )(q, kt, vt)

    return lax.cond(jnp.max(flags) > SB_DEAD, full, lambda: o_fast)


def _sb_sample_kernel(q_ref, ktn_ref, vtn_ref, ktc_ref, vtc_ref, o_ref, *maybe_flag, heads, strip):
    s = q_ref.shape[2]
    width = ktc_ref.shape[4]
    blk = SB_BLOCK
    uu = _suffix_neg_ones(blk)
    uu_new = _suffix_neg_ones(s)
    strict = _strict_mask(s)

    qs = [q_ref[0, h] for h in range(heads)]
    state = _sb_strips(qs, [ktn_ref[0, h].astype(BF16) for h in range(heads)],
                       [vtn_ref[0, h].astype(BF16) for h in range(heads)],
                       [jnp.zeros((s, 1), F32)] * heads, [jnp.zeros((s, HEAD_DIM), F32)] * heads,
                       uu_new, [strict])

    def k_strip(jj, st):
        k0 = pl.multiple_of(width - (jj + 1) * strip * blk, blk)
        carries, accs = _sb_strips(
            qs, [ktc_ref[0, 0, h, :, pl.ds(k0, strip * blk)].astype(BF16) for h in range(heads)],
            [vtc_ref[0, 0, h, :, pl.ds(k0, strip * blk)].astype(BF16) for h in range(heads)],
            st[0], st[1], uu, [None] * strip)
        return tuple(carries), tuple(accs)

    carries, accs = lax.fori_loop(0, width // (strip * blk), k_strip, (tuple(state[0]), tuple(state[1])))
    o_ref[0] = jnp.concatenate(list(accs), axis=-1).astype(BF16)
    if maybe_flag:
        worst = carries[0]
        for h in range(1, heads):
            worst = jnp.maximum(worst, carries[h])
        maybe_flag[0][...] = jnp.broadcast_to(jnp.max(worst), maybe_flag[0].shape)


def _sb_sample(q, kt_new, vt_new, cache_kt, cache_vt, layer):
    b, n_heads, s, _ = q.shape
    past = cache_kt.shape[4]
    width = min(past, SB_SAMPLE_WINDOW)
    assert past % width == 0 and width % SB_BLOCK == 0
    last = past // width - 1
    o_shape = jax.ShapeDtypeStruct((b, s, n_heads * HEAD_DIM), BF16)

    def specs(heads, cache_width, cache_block):
        q_blk = pl.BlockSpec((1, heads, s, HEAD_DIM), lambda bi, hg: (bi, hg, 0, 0))
        new_blk = pl.BlockSpec((1, heads, HEAD_DIM, s), lambda bi, hg: (bi, hg, 0, 0))
        cache_blk = pl.BlockSpec((1, 1, heads, HEAD_DIM, cache_width),
                                 lambda bi, hg: (layer, bi, hg, 0, cache_block))
        o_blk = pl.BlockSpec((1, s, heads * HEAD_DIM), lambda bi, hg: (bi, 0, hg))
        return (b, n_heads // heads), [q_blk, new_blk, new_blk, cache_blk, cache_blk], o_blk

    heads = n_heads
    grid, in_specs, o_blk = specs(heads, width, last)
    o_fast, flags = pl.pallas_call(
        functools.partial(_sb_sample_kernel, heads=heads, strip=width // SB_BLOCK),
        grid=grid,
        in_specs=in_specs,
        out_specs=[o_blk, pl.BlockSpec((1, 1, 8, 128), lambda bi, hg: (bi, hg, 0, 0))],
        out_shape=[o_shape, jax.ShapeDtypeStruct((b, n_heads // heads, 8, 128), F32)],
        compiler_params=_cparams(("parallel", "parallel")),
        name="sb_sample_fast",
    )(q, kt_new, vt_new, cache_kt, cache_vt)
    if width == past:
        return o_fast

    def full():
        heads = 4
        grid, in_specs, o_blk = specs(heads, past, 0)
        return pl.pallas_call(
            functools.partial(_sb_sample_kernel, heads=heads, strip=1),
            grid=grid,
            in_specs=in_specs,
            out_specs=o_blk,
            out_shape=o_shape,
            compiler_params=_cparams(("parallel", "parallel")),
            name="sb_sample_full",
        )(q, kt_new, vt_new, cache_kt, cache_vt)

    return lax.cond(jnp.max(flags) > SB_DEAD, full, lambda: o_fast)


def _route(logits):
    tm = logits.shape[0]
    lane = lax.broadcasted_iota(jnp.int32, (tm, ROUTER_LANES), 1)
    lane_f = lane.astype(F32)
    neg = -jnp.inf
    first = lambda hit: jnp.min(jnp.where(hit, lane_f, float(ROUTER_LANES)), axis=-1, keepdims=True)

    gl = jnp.where(lane < N_GROUPS, logits, neg)
    g_max = jnp.max(gl, axis=-1, keepdims=True)
    g_top = 1.0 / jnp.sum(jnp.exp(gl - g_max), axis=-1, keepdims=True)
    g_idx = first(gl == g_max)

    lo = N_GROUPS + g_idx * EXPERTS_PER_GROUP
    in_group = (lane_f >= lo) & (lane_f < lo + EXPERTS_PER_GROUP)
    sel = jnp.where(in_group, logits, neg)
    t1 = jnp.max(sel, axis=-1, keepdims=True)
    i1 = first(sel == t1)
    sel2 = jnp.where(lane_f == i1, neg, sel)
    t2 = jnp.max(sel2, axis=-1, keepdims=True)
    i2 = first(sel2 == t2)
    e2 = jnp.exp(t2 - t1)
    den = 1.0 + e2
    w1 = (1.0 / den) * g_top
    w2 = (e2 / den) * g_top
    return jnp.where(lane_f == i1, w1, 0.0) + jnp.where(lane_f == i2, w2, 0.0)


CHANNEL_ROW_SPLITS = 2
EXPERTS_PER_TRIP = 16


def _channel_kernel(x_ref, o_ref, p_ref, wo_ref, gffn_ref, wr_ref, wg_ref, wu_ref, wd_ref, gple_ref,
                    wpg_ref, wpp_ref, *rest, n_next):
    next_in, (out_ref, *next_out) = rest[:n_next], rest[n_next:]
    acc_ref = out_ref
    tm = x_ref.shape[0]
    d_expert = wd_ref.shape[2]
    half = tm // CHANNEL_ROW_SPLITS
    rows = [slice(i * half, (i + 1) * half) for i in range(CHANNEL_ROW_SPLITS)]

    h1s = [x_ref[r, :] + jnp.dot(o_ref[r, :], wo_ref[...], preferred_element_type=F32) for r in rows]
    xns = [_rms_unit(h1) * gffn_ref[...] for h1 in h1s]
    xnbs = [xn.astype(BF16) for xn in xns]
    xlos = [(xn - xnb.astype(F32)).astype(BF16) for xn, xnb in zip(xns, xnbs)]
    logits = []
    for xnb, xlo in zip(xnbs, xlos):
        hi_lo = jnp.dot(xnb, wr_ref[...], preferred_element_type=F32)
        lo_hi = jnp.dot(xlo, wr_ref[:, 0:ROUTER_LANES], preferred_element_type=F32)
        logits.append(hi_lo[:, 0:ROUTER_LANES] + hi_lo[:, ROUTER_LANES:] + lo_hi)
    gates = jnp.concatenate([_route(l) for l in logits], axis=0)
    xnb = jnp.concatenate(xnbs, axis=0)
    lane = lax.broadcasted_iota(jnp.int32, gates.shape, 1)
    for r, h1 in zip(rows, h1s):
        acc_ref[r, :] = h1

    def experts(j, _):
        es = [j * EXPERTS_PER_TRIP + k for k in range(EXPERTS_PER_TRIP)]
        gs = [jnp.dot(xnb, wg_ref[0, e], preferred_element_type=F32) for e in es]
        us = [jnp.dot(xnb, wu_ref[0, e], preferred_element_type=F32) for e in es]
        hids = []
        for e, g, u in zip(es, gs, us):
            gate_e = jnp.sum(jnp.where(lane == N_GROUPS + e, gates, 0.0), axis=-1, keepdims=True)
            hids.append(((g * jax.nn.sigmoid(g)) * u * gate_e).astype(BF16))
        wd = wd_ref[0, pl.ds(j * EXPERTS_PER_TRIP, EXPERTS_PER_TRIP)]
        acc_ref[...] += jnp.dot(jnp.concatenate(hids, axis=-1),
                                wd.reshape(EXPERTS_PER_TRIP * d_expert, wd.shape[-1]),
                                preferred_element_type=F32)
        return 0

    lax.fori_loop(0, N_EXPERTS // EXPERTS_PER_TRIP, experts, 0)
    h2s = [acc_ref[r, :] for r in rows]
    x3s = [(_rms_unit(h2) * gple_ref[...]).astype(BF16) for h2 in h2s]
    ple_gates = [jax.nn.sigmoid(jnp.dot(x3, wpg_ref[0], preferred_element_type=F32)) for x3 in x3s]
    projs = [jnp.dot(p_ref[0, r, :].astype(BF16), wpp_ref[0], preferred_element_type=F32) for r in rows]
    h3s = [h2 + proj * gate for h2, proj, gate in zip(h2s, projs, ple_gates)]
    for r, h3 in zip(rows, h3s):
        out_ref[r, :] = h3
    if n_next:
        _store_proj_b(rows, _proj_b_rows(h3s, *next_in), *next_out)


def _channel(x, o, p, layer, w, stacks, next_proj=None, batch=1):
    n, d = x.shape
    tm = min(n, 512)
    row = lambda cols: pl.BlockSpec((tm, cols), lambda i: (i, 0))
    small = [w["wo"], w["gffn"], w["wr"]]
    moe = [stacks["wg"], stacks["wu"], stacks["wd"]]
    ple = [stacks["wpg"], stacks["wpp"]]
    next_proj = list(next_proj or [])
    out_specs = [row(d)]
    out_shape = [jax.ShapeDtypeStruct((n, d), F32)]
    if next_proj:
        t = n // batch
        per_entry = t // tm
        assert t % tm == 0
        n_heads = next_proj[1].shape[1] // HEAD_DIM
        n_kv = next_proj[4].shape[1] // (2 * HEAD_DIM)
        hm = lambda i: (i // per_entry, 0, i % per_entry, 0)
        for heads, dtype in ((n_heads, BF16), (n_kv, F32), (n_kv, F32)):
            out_specs.append(pl.BlockSpec((1, heads, tm, HEAD_DIM), hm))
            out_shape.append(jax.ShapeDtypeStruct((batch, heads, t, HEAD_DIM), dtype))
    outs = pl.pallas_call(
        functools.partial(_channel_kernel, n_next=len(next_proj)),
        grid=(n // tm,),
        in_specs=([row(d), row(d), pl.BlockSpec((1, tm, p.shape[2]), lambda i: (layer, i, 0))]
                  + [_resident(a.shape) for a in small] + [_resident_layer(a.shape, layer) for a in moe]
                  + [_resident(w["gple"].shape)] + [_resident_layer(a.shape, layer) for a in ple]
                  + [_resident(a.shape) for a in next_proj]),
        out_specs=out_specs,
        out_shape=out_shape,
        compiler_params=_cparams(("parallel",)),
        name="channel",
    )(x, o, p, *small, *moe, w["gple"], *ple, *next_proj)
    return outs if next_proj else outs[0]


def _proj_b_rows(hs, ga_ref, wq_ref, gq_ref, gkv_ref, wkv_ref, gk_ref, gsum_ref, gexp_ref):
    n_kv = wkv_ref.shape[1] // (2 * HEAD_DIM)
    ys = [_rms_unit(h) for h in hs]
    qs = [jnp.dot((y * ga_ref[...]).astype(BF16), wq_ref[...], preferred_element_type=F32) for y in ys]
    kvs = [jnp.dot((y * gkv_ref[...]).astype(BF16), wkv_ref[...], preferred_element_type=F32) for y in ys]
    mss = [jnp.dot((q * q).astype(BF16), gsum_ref[...], preferred_element_type=F32) * (1.0 / HEAD_DIM)
           for q in qs]
    inv_fulls = [jnp.dot(_split_hl(lax.rsqrt(ms + EPS)), gexp_ref[...], preferred_element_type=F32)
                 for ms in mss]
    out = []
    for q, inv_full, kv in zip(qs, inv_fulls, kvs):
        qn = (q * inv_full * gq_ref[...]) * (HEAD_DIM ** -0.5)
        ks = [_rms_unit(kv[:, h * HEAD_DIM:(h + 1) * HEAD_DIM]) * gk_ref[...] for h in range(n_kv)]
        vs = [kv[:, (n_kv + h) * HEAD_DIM:(n_kv + h + 1) * HEAD_DIM] for h in range(n_kv)]
        out.append((qn, ks, vs))
    return out


def _store_proj_b(rows, projected, q_ref, k_ref, v_ref):
    for r, (qn, ks, vs) in zip(rows, projected):
        for h in range(q_ref.shape[1]):
            q_ref[0, h, r, :] = qn[:, h * HEAD_DIM:(h + 1) * HEAD_DIM].astype(BF16)
        for h in range(k_ref.shape[1]):
            k_ref[0, h, r, :] = ks[h]
            v_ref[0, h, r, :] = vs[h]


def _t5_bucket(rel):
    nb = NUM_BUCKETS // 2
    max_exact = nb // 2
    n = jnp.abs(rel)
    large = max_exact + (jnp.log(jnp.maximum(n, 1).astype(jnp.float32) / max_exact)
                         / math.log(MAX_DISTANCE / max_exact) * (nb - max_exact)).astype(jnp.int32)
    large = jnp.minimum(large, nb - 1)
    return jnp.where(rel > 0, nb, 0) + jnp.where(n < max_exact, n, large)


def _bias_kernel(bucket_ref, rbt_ref, out_ref):
    bucket = bucket_ref[...].astype(F32)
    guard = 2.0 ** -10
    acc = jnp.zeros(out_ref.shape, F32)
    for b in range(NUM_BUCKETS):
        hit = (bucket >= b - guard) & (bucket < b + 1 - guard)
        acc = acc + jnp.where(hit, rbt_ref[:, b:b + 1], 0.0)
    out_ref[...] = jnp.where(bucket < 0, -jnp.inf, acc)


def _bias_table(rel_bias, nq, nk, key_offset, chunk_window):
    qi = jnp.arange(nq, dtype=jnp.int32)[:, None]
    kj = jnp.arange(nk, dtype=jnp.int32)[None, :] - key_offset
    bucket = _t5_bucket(kj - qi)
    if chunk_window:
        qc, kc = qi // CHUNK, kj // CHUNK
        bucket = jnp.where((kc <= qc) & (kc >= qc - WIN_CHUNKS), bucket, -1)
    bucket = bucket.reshape(1, nq * nk)
    n_heads = rel_bias.shape[1]
    out = pl.pallas_call(
        _bias_kernel,
        out_shape=jax.ShapeDtypeStruct((n_heads, nq * nk), F32),
        name="bias_table",
    )(bucket, rel_bias.T)
    return out.reshape(n_heads, nq, nk)


SWA_ROWS_PER_TRIP = 128


def _swa_kernel(sink_ref, q_ref, k_ref, v_ref, bias_ref, o_ref, kpad_ref, vpad_ref, *, cq, wl, pad, n_chunks,
                group):
    n_kv, tk = k_ref.shape[1], k_ref.shape[2]
    if pad:
        kpad_ref[:, 0:pad, :] = jnp.zeros((n_kv, pad, HEAD_DIM), BF16)
        vpad_ref[:, 0:pad, :] = jnp.zeros((n_kv, pad, HEAD_DIM), BF16)
    kpad_ref[:, pad:pad + tk, :] = k_ref[0].astype(BF16)
    vpad_ref[:, pad:pad + tk, :] = v_ref[0].astype(BF16)

    per_trip = max(1, SWA_ROWS_PER_TRIP // cq)
    per_trip = per_trip if n_chunks % per_trip == 0 else 1
    n_heads = n_kv * group

    def chunks(c, masked):
        r0s = [pl.multiple_of((c * per_trip + i) * cq, cq) for i in range(per_trip)]
        logits = [[lax.dot_general(
            q_ref[0, kv * group:(kv + 1) * group, pl.ds(r0, cq), :].reshape(group * cq, HEAD_DIM),
            kpad_ref[kv, pl.ds(r0, wl), :], _NT, preferred_element_type=F32)
            for kv in range(n_kv)] for r0 in r0s]
        es, dens = [], []
        for r0, lg in zip(r0s, logits):
            valid = lax.broadcasted_iota(jnp.int32, (cq, wl), 1) + r0 >= pad
            for h in range(n_heads):
                l = lg[h // group][(h % group) * cq:(h % group + 1) * cq] + bias_ref[h]
                if masked:
                    l = jnp.where(valid, l, -jnp.inf)
                sink = sink_ref[h]
                m = jnp.maximum(jnp.max(l, axis=-1, keepdims=True), sink)
                e = jnp.exp(l - m)
                dens.append(jnp.sum(e, axis=-1, keepdims=True) + jnp.exp(sink - m))
                es.append(e.astype(BF16))
        pvs = [[jnp.dot(jnp.concatenate(es[i * n_heads + kv * group:i * n_heads + (kv + 1) * group], axis=0),
                        vpad_ref[kv, pl.ds(r0, wl), :], preferred_element_type=F32)
                for kv in range(n_kv)] for i, r0 in enumerate(r0s)]
        for i, r0 in enumerate(r0s):
            outs = [pvs[i][h // group][(h % group) * cq:(h % group + 1) * cq] / dens[i * n_heads + h]
                    for h in range(n_heads)]
            o_ref[0, pl.ds(r0, cq), :] = jnp.concatenate(outs, axis=-1).astype(BF16)
        return 0

    trips = n_chunks // per_trip
    masked_trips = min(trips, -(-pad // (cq * per_trip)))
    for c in range(masked_trips):
        chunks(c, True)
    lax.fori_loop(masked_trips, trips, lambda c, _: chunks(c, False), 0)


def _swa(q, k_win, v_win, bias, sinks, *, cq, wl, pad):
    b, n_heads, tq, _ = q.shape
    n_kv, tk = k_win.shape[1], k_win.shape[2]
    n_chunks = tq // cq
    assert (n_chunks - 1) * cq + wl == pad + tk
    kv_blk = pl.BlockSpec((1, n_kv, tk, HEAD_DIM), lambda bi: (bi, 0, 0, 0))
    return pl.pallas_call(
        functools.partial(_swa_kernel, cq=cq, wl=wl, pad=pad, n_chunks=n_chunks, group=n_heads // n_kv),
        grid=(b,),
        in_specs=[pl.BlockSpec(memory_space=pltpu.SMEM),
                  pl.BlockSpec((1, n_heads, tq, HEAD_DIM), lambda bi: (bi, 0, 0, 0)),
                  kv_blk, kv_blk, _resident(bias.shape)],
        out_specs=pl.BlockSpec((1, tq, n_heads * HEAD_DIM), lambda bi: (bi, 0, 0)),
        out_shape=jax.ShapeDtypeStruct((b, tq, n_heads * HEAD_DIM), BF16),
        scratch_shapes=[pltpu.VMEM((n_kv, pad + tk, HEAD_DIM), BF16),
                        pltpu.VMEM((n_kv, pad + tk, HEAD_DIM), BF16)],
        compiler_params=_cparams(("parallel",)),
        name="swa",
    )(sinks, q, k_win, v_win, bias)


def _row(v):
    return v.reshape(1, -1).astype(F32)


def _router_hi_lo(wr):
    hi = wr.astype(BF16)
    lo = (wr - hi.astype(F32)).astype(BF16)
    return jnp.concatenate([hi, lo], axis=1)


def _prep_weights(prm):
    d = prm["a_w_o"].shape[1]
    depth = prm["norm_ffn"].shape[0]
    n_a = prm["a_w_qkv"].shape[0]
    scale = HEAD_DIM ** -0.5
    w = {"channel": [], "n_a": n_a, "depth": depth}
    for i in range(depth):
        wo = prm["a_w_o"][i] if i < n_a else prm["b_w_o"][i - n_a]
        pad = ROUTER_LANES - N_GROUPS - N_EXPERTS
        wr = jnp.concatenate([prm["moe_w_group"][i], prm["moe_w_router"][i].reshape(d, N_EXPERTS),
                              jnp.zeros((d, pad), F32)], axis=1)
        w["channel"].append({"wo": wo.astype(BF16), "gffn": _row(prm["norm_ffn"][i]),
                             "wr": _router_hi_lo(wr), "gple": _row(prm["norm_ple"][i])})
    w["stacks"] = {"wg": prm["moe_w_gate"].astype(BF16), "wu": prm["moe_w_up"].astype(BF16),
                   "wd": prm["moe_w_down"].astype(BF16), "wpg": prm["ple_w_gate"].astype(BF16),
                   "wpp": prm["ple_w_proj"].astype(BF16)}
    w["qkv"] = []
    for i in range(n_a):
        wq = prm["a_w_qkv"][i]
        hd = wq.shape[1] // 3
        w["qkv"].append(((wq[:, :hd] * scale).astype(BF16),
                         wq[:, hd:].T.astype(BF16)))
    n_heads = prm["b_w_q"].shape[2] // HEAD_DIM
    head_of = jnp.arange(n_heads * HEAD_DIM, dtype=jnp.int32) // HEAD_DIM
    lanes = jnp.arange(ROUTER_LANES, dtype=jnp.int32)
    member = (head_of[:, None] == lanes[None, :]).astype(BF16)
    w["gsum"] = member
    w["gexp"] = jnp.concatenate([member.T, member.T], axis=0)
    w["wq_b"] = [prm["b_w_q"][j].astype(BF16) for j in range(depth - n_a)]
    w["gq_b"] = [_row(jnp.tile(prm["b_q_norm"][j], n_heads)) for j in range(depth - n_a)]
    w["wkv"] = prm["b_w_kv"].astype(BF16)
    return w


def _assert_sample_window_visible(past_len, s, tk):
    q_chunk = [(past_len + i) // CHUNK for i in range(s)]
    k_pos = [past_len + s - tk + j for j in range(tk)]
    ok = all(kp >= 0 and qc - WIN_CHUNKS <= kp // CHUNK <= qc for qc in q_chunk for kp in k_pos)
    if not ok:
        raise NotImplementedError("sample window with masked keys")


def _fold(x):
    return x.reshape(1, -1, x.shape[-1])


def _unfold_rows(a, b):
    _, n_heads, rows, dh = a.shape
    return a.reshape(n_heads, b, rows // b, dh).transpose(1, 0, 2, 3)


def _unfold_cols(a, b):
    _, n_heads, dh, rows = a.shape
    return a.reshape(n_heads, dh, b, rows // b).transpose(2, 0, 1, 3)


def _run_trunk(x, p, prm, w, sb_cache_k=None, sb_cache_v=None, swa_cache_k=None, swa_cache_v=None):
    bx, t, d = x.shape
    n_a, depth = w["n_a"], w["depth"]
    sample = sb_cache_k is not None
    h = x
    sb_k, sb_v = [], []
    k_win = v_win = None
    q_b = None
    for i in range(depth):
        if i < n_a:
            if sample:
                q, kt, vt = _proj_a(_fold(h), _row(prm["norm_attn"][i]), *w["qkv"][i])
                q, kt, vt = _unfold_rows(q, bx), _unfold_cols(kt, bx), _unfold_cols(vt, bx)
            else:
                q, kt, vt = _proj_a(h, _row(prm["norm_attn"][i]), *w["qkv"][i], pack_q=True)
            sb_k.append(kt)
            sb_v.append(vt)
            if sample:
                o = _sb_sample(q, kt, vt, jnp.swapaxes(sb_cache_k, -1, -2), jnp.swapaxes(sb_cache_v, -1, -2), i)
            else:
                o = _sb_prompt(q, kt, vt)
        else:
            j = i - n_a
            if j > 0:
                raise NotImplementedError("one B layer supported")
            if sample:
                tk = k_win.shape[2]
                _assert_sample_window_visible(sb_cache_k.shape[3], t, tk)
                o = _swa(q_b, k_win, v_win, _bias_table(prm["rel_bias"], t, tk, tk - t, False),
                         prm["b_sinks"][j], cq=t, wl=tk, pad=0)
            else:
                cq = 2 * CHUNK
                wl = WINDOW + cq
                assert t % cq == 0
                o = _swa(q_b, k_win, v_win, _bias_table(prm["rel_bias"], cq, wl, WINDOW, True),
                         prm["b_sinks"][j], cq=cq, wl=wl, pad=WINDOW)
        flat = (h.reshape(bx * t, d), o.reshape(bx * t, d), p.reshape(depth, bx * t, -1), i,
                w["channel"][i], w["stacks"])
        if i != n_a - 1:
            h = _channel(*flat).reshape(bx, t, d)
        else:
            next_proj = [_row(prm["norm_attn"][n_a]), w["wq_b"][0], w["gq_b"][0], _row(prm["kv_norm"]), w["wkv"],
                         _row(prm["b_k_norm"]), w["gsum"], w["gexp"]]
            h, q_b, k_s, v_s = _channel(*flat, next_proj=next_proj, batch=1 if sample else bx)
            h = h.reshape(bx, t, d)
            if sample:
                q_b, k_s, v_s = _unfold_rows(q_b, bx), _unfold_rows(k_s, bx), _unfold_rows(v_s, bx)
                k_win = jnp.concatenate([swa_cache_k, k_s], axis=2)
                v_win = jnp.concatenate([swa_cache_v, v_s], axis=2)
            else:
                k_win, v_win = k_s, v_s
    sb_k = jnp.swapaxes(jnp.stack(sb_k), -1, -2)
    sb_v = jnp.swapaxes(jnp.stack(sb_v), -1, -2)
    return h, sb_k, sb_v, k_win[:, :, -WINDOW:], v_win[:, :, -WINDOW:]


def kernel(x_prompt, x_sample, p_prompt, p_sample, cache_sb_k, cache_sb_v, cache_swa_k, cache_swa_v, norm_attn, norm_ffn, norm_ple, a_w_qkv, a_w_o, kv_norm, b_w_kv, b_k_norm, b_w_q, b_q_norm, b_sinks, b_w_o, rel_bias, moe_w_group, moe_w_router, moe_w_gate, moe_w_up, moe_w_down, ple_w_proj, ple_w_gate):
    prm = {
        "norm_attn": norm_attn, "norm_ffn": norm_ffn, "norm_ple": norm_ple,
        "a_w_qkv": a_w_qkv, "a_w_o": a_w_o, "kv_norm": kv_norm, "b_w_kv": b_w_kv, "b_k_norm": b_k_norm,
        "b_w_q": b_w_q, "b_q_norm": b_q_norm, "b_sinks": b_sinks, "b_w_o": b_w_o, "rel_bias": rel_bias,
        "moe_w_group": moe_w_group, "moe_w_router": moe_w_router, "moe_w_gate": moe_w_gate,
        "moe_w_up": moe_w_up, "moe_w_down": moe_w_down, "ple_w_proj": ple_w_proj, "ple_w_gate": ple_w_gate,
    }
    w = _prep_weights(prm)
    y_p, sb_k_p, sb_v_p, swa_k_p, swa_v_p = _run_trunk(x_prompt, p_prompt, prm, w)
    y_s, sb_k_s, sb_v_s, swa_k_s, swa_v_s = _run_trunk(x_sample, p_sample, prm, w, cache_sb_k, cache_sb_v,
                                                       cache_swa_k, cache_swa_v)
    return (y_p, y_s, sb_k_p, sb_v_p, swa_k_p, swa_v_p, sb_k_s, sb_v_s, swa_k_s, swa_v_s)
```

```python
import functools
import math

import jax
import jax.numpy as jnp
from jax import lax
from jax.experimental import pallas as pl
from jax.experimental.pallas import tpu as pltpu

F32 = jnp.float32
BF16 = jnp.bfloat16

HEAD_DIM = 64
CHUNK = 64
WINDOW = 128
WIN_CHUNKS = WINDOW // CHUNK
NUM_BUCKETS = 32
MAX_DISTANCE = 128
N_GROUPS = 4
EXPERTS_PER_GROUP = 4
N_EXPERTS = N_GROUPS * EXPERTS_PER_GROUP
EPS = 1e-6
LOG2E = 1.4426950408889634
SB_BLOCK = 128
ROUTER_LANES = 128
VMEM_LIMIT = 63 * 1024 * 1024

_NT = (((1,), (1,)), ((), ()))


def _cparams(sem):
    return pltpu.CompilerParams(dimension_semantics=sem, vmem_limit_bytes=VMEM_LIMIT)


def _rms_unit(x):
    return x * lax.rsqrt(jnp.mean(x * x, axis=-1, keepdims=True) + EPS)


def _split_hl(a):
    hi = a.astype(BF16)
    lo = (a - hi.astype(F32)).astype(BF16)
    return jnp.concatenate([hi, lo], axis=-1)


def _resident(shape):
    nd = len(shape)
    return pl.BlockSpec(shape, lambda *_: (0,) * nd, pipeline_mode=pl.Buffered(1))


def _resident_layer(shape, layer):
    nd = len(shape)
    return pl.BlockSpec((1,) + tuple(shape[1:]), lambda *_: (layer,) + (0,) * (nd - 1),
                        pipeline_mode=pl.Buffered(1))


def _proj_a_kernel(x_ref, g_ref, wq_ref, wkvt_ref, q_ref, kt_ref, vt_ref, ktb_ref, vtb_ref, *, n_heads, pack_q):
    xn = (_rms_unit(x_ref[0]) * g_ref[...]).astype(BF16)
    heads_per_dot = 4
    width = heads_per_dot * HEAD_DIM
    per_store = 2 if pack_q else 1
    for c in range(n_heads // heads_per_dot):
        r = jnp.dot(xn, wq_ref[:, c * width:(c + 1) * width], preferred_element_type=F32)
        for s in range(heads_per_dot // per_store):
            lanes = slice(s * per_store * HEAD_DIM, (s + 1) * per_store * HEAD_DIM)
            q_ref[0, c * heads_per_dot // per_store + s] = r[:, lanes].astype(BF16)
    for c in range(2 * n_heads // heads_per_dot):
        r = lax.dot_general(wkvt_ref[c * width:(c + 1) * width, :], xn, _NT, preferred_element_type=F32)
        for hh in range(heads_per_dot):
            h = c * heads_per_dot + hh
            piece = r[hh * HEAD_DIM:(hh + 1) * HEAD_DIM, :]
            if h < n_heads:
                kt_ref[0, h] = piece
                ktb_ref[0, h] = piece.astype(BF16)
            else:
                vt_ref[0, h - n_heads] = piece
                vtb_ref[0, h - n_heads] = piece.astype(BF16)


def _proj_a(x, g, wq, wkvt, pack_q=False):
    bx, t, d = x.shape
    n_heads = wq.shape[1] // HEAD_DIM
    tm = min(t, 512)
    q_slabs, q_width = (n_heads // 2, 2 * HEAD_DIM) if pack_q else (n_heads, HEAD_DIM)
    return pl.pallas_call(
        functools.partial(_proj_a_kernel, n_heads=n_heads, pack_q=pack_q),
        grid=(bx, t // tm),
        in_specs=[pl.BlockSpec((1, tm, d), lambda b, i: (b, i, 0)),
                  _resident((1, d)), _resident(wq.shape), _resident(wkvt.shape)],
        out_specs=[pl.BlockSpec((1, q_slabs, tm, q_width), lambda b, i: (b, 0, i, 0)),
                   ] + [pl.BlockSpec((1, n_heads, HEAD_DIM, tm), lambda b, i: (b, 0, 0, i))] * 4,
        out_shape=[jax.ShapeDtypeStruct((bx, q_slabs, t, q_width), BF16)]
        + [jax.ShapeDtypeStruct((bx, n_heads, HEAD_DIM, t), dt) for dt in (F32, F32, BF16, BF16)],
        compiler_params=_cparams(("parallel", "parallel")),
        name="proj_a",
    )(x, g, wq, wkvt)


SB_DEAD = -104.0
SB_WINDOW_BLOCKS = 3
SB_SAMPLE_WINDOW = 256


def _suffix_neg_ones(n):
    r = lax.broadcasted_iota(jnp.int32, (n, n), 0)
    c = lax.broadcasted_iota(jnp.int32, (n, n), 1)
    return jnp.where(r >= c, -1.0, 0.0).astype(BF16)


def _strict_mask(n):
    row = lax.broadcasted_iota(jnp.int32, (n, n), 0)
    col = lax.broadcasted_iota(jnp.int32, (n, n), 1)
    return col < row


def _pair_rows(kt, half):
    zeros = jnp.zeros_like(kt)
    return jnp.concatenate([kt, zeros] if half == 0 else [zeros, kt], axis=0)


def _softplus(z):
    return jnp.maximum(z, 0.0) + jnp.log(1.0 + jnp.exp2(jnp.abs(z) * -LOG2E))


def _sb_strips(qs, kts, vts, carries, accs, uu, masks):
    n = len(qs)
    blk = uu.shape[1]
    nb = kts[0].shape[1] // blk
    zs = [jnp.dot(qs[i], kts[i], preferred_element_type=F32) for i in range(n)]
    sps = [_softplus(z) for z in zs]
    ws = [[None] * nb for _ in range(n)]
    carries = list(carries)
    for b in reversed(range(nb)):
        sl = slice(b * blk, (b + 1) * blk)
        spbs = [sp[:, sl] if masks[b] is None else jnp.where(masks[b], sp[:, sl], 0.0) for sp in sps]
        sufs = [jnp.dot(spb.astype(BF16), uu, preferred_element_type=F32) for spb in spbs]
        for i in range(n):
            wb = jnp.exp(zs[i][:, sl] + sufs[i] + carries[i])
            if masks[b] is not None:
                wb = jnp.where(masks[b], wb, 0.0)
            ws[i][b] = wb.astype(BF16)
            carries[i] = carries[i] + sufs[i][:, 0:1]
    wcat = [w[0] if nb == 1 else jnp.concatenate(w, axis=-1) for w in ws]
    accs = [accs[i] + lax.dot_general(wcat[i], vts[i], _NT, preferred_element_type=F32) for i in range(n)]
    return carries, accs


SB_STAGES = 5
SB_OLD_ROWS = 64


def _sb_prompt_fast_kernel(q_ref, kt_ref, vt_ref, o_ref, flag_ref, z_ref, sp_ref, suf_ref, w_ref, *, heads, groups):
    t = q_ref.shape[2]
    blk = SB_BLOCK
    old = SB_OLD_ROWS
    nq = t // blk
    win = SB_WINDOW_BLOCKS
    wl = win * blk
    ring = z_ref.shape[0]
    first = win - 1
    per_group = nq - first
    units = groups * per_group
    width = heads * HEAD_DIM
    uu = _suffix_neg_ones(blk)
    strict = _strict_mask(blk)
    z_ref[...] = jnp.zeros(z_ref.shape, F32)
    sp_ref[...] = jnp.zeros(sp_ref.shape, BF16)
    suf_ref[...] = jnp.zeros(suf_ref.shape, F32)
    w_ref[...] = jnp.zeros(w_ref.shape, BF16)

    hs = range(groups * heads)
    for i in range(first):
        _, accs = _sb_strips([q_ref[0, h // 2, i * blk:(i + 1) * blk, :] for h in hs],
                             [_pair_rows(kt_ref[0, h, :, 0:(i + 1) * blk].astype(BF16), h % 2) for h in hs],
                             [vt_ref[0, h, :, 0:(i + 1) * blk].astype(BF16) for h in hs],
                             [jnp.zeros((blk, 1), F32)] * len(hs), [jnp.zeros((blk, HEAD_DIM), F32)] * len(hs),
                             uu, [None] * i + [strict])
        o_ref[0, i * blk:(i + 1) * blk, :] = jnp.concatenate(accs, axis=-1).astype(BF16)

    def where_is(n):
        u = jnp.clip(n, 0, units - 1)
        g = u // per_group
        i = first + u - g * per_group
        q0 = pl.multiple_of(i * blk, blk)
        k0 = pl.multiple_of((i - first) * blk, blk)
        return g * heads, q0, k0, lax.rem(n + ring * SB_STAGES, ring)

    def body(n, worst):
        h0, q0, k0, _ = where_is(n - 4)
        accs = []
        for h in range(heads):
            vt = vt_ref[0, h0 + h, :, pl.ds(k0, wl)].astype(BF16)
            accs.append(lax.dot_general(w_ref[h], vt, _NT, preferred_element_type=F32))
        o_ref[0, pl.ds(q0, blk), pl.ds(pl.multiple_of(h0 * HEAD_DIM, width), width)] = (
            jnp.concatenate(accs, axis=-1).astype(BF16))

        _, _, _, slot = where_is(n - 3)
        counts = n - 3 >= 0
        for h in range(heads):
            s_old, s_prev, s_own = suf_ref[h, 0:old], suf_ref[h, old:old + blk], suf_ref[h, old + blk:]
            w_own = jnp.where(strict, jnp.exp(z_ref[slot, h, :, 2 * blk:] + s_own), 0.0)
            carry = s_own[:, 0:1]
            w_prev = jnp.exp(z_ref[slot, h, :, blk:2 * blk] + s_prev + carry)
            carry = carry + s_prev[:, 0:1]
            w_old = jnp.exp(z_ref[slot, h, 0:old, 0:blk] + s_old + carry[0:old])
            w_ref[h, :, 2 * blk:] = w_own.astype(BF16)
            w_ref[h, :, blk:2 * blk] = w_prev.astype(BF16)
            w_ref[h, 0:old, 0:blk] = w_old.astype(BF16)
            edge = jnp.concatenate([carry[0:old] + s_old[:, 0:1], carry[old:]], axis=0)
            worst = jnp.maximum(worst, jnp.where(counts, edge, -jnp.inf))

        for h in range(heads):
            suf_ref[h] = jnp.dot(sp_ref[h], uu, preferred_element_type=F32)

        _, _, _, slot = where_is(n - 1)
        for h in range(heads):
            sp_ref[h, 0:old] = _softplus(z_ref[slot, h, 0:old, 0:blk]).astype(BF16)
            sp_ref[h, old:old + blk] = _softplus(z_ref[slot, h, :, blk:2 * blk]).astype(BF16)
            sp_ref[h, old + blk:] = jnp.where(strict, _softplus(z_ref[slot, h, :, 2 * blk:]), 0.0).astype(BF16)

        h0, q0, k0, slot = where_is(n)
        for h in range(heads):
            q = q_ref[0, h0 // 2 + h // 2, pl.ds(q0, blk), :]
            kt = _pair_rows(kt_ref[0, h0 + h, :, pl.ds(k0, wl)].astype(BF16), h % 2)
            z_ref[slot, h, :, blk:] = jnp.dot(q, kt[:, blk:], preferred_element_type=F32)
            z_ref[slot, h, 0:old, 0:blk] = jnp.dot(q[0:old], kt[:, 0:blk], preferred_element_type=F32)
        return worst

    worst = lax.fori_loop(0, units + SB_STAGES - 1, body, jnp.full((blk, 1), -jnp.inf, F32))
    flag_ref[...] = jnp.broadcast_to(jnp.max(worst), flag_ref.shape)


def _sb_prompt_full_kernel(q_ref, kt_ref, vt_ref, o_ref, *, heads):
    t = q_ref.shape[2]
    blk = SB_BLOCK
    uu = _suffix_neg_ones(blk)
    strict = _strict_mask(blk)

    def q_block(i, _):
        q0 = pl.multiple_of(i * blk, blk)
        qs = [q_ref[0, h // 2, pl.ds(q0, blk), :] for h in range(heads)]

        def strips(k0, carries, accs, masks):
            return _sb_strips(qs, [_pair_rows(kt_ref[0, h, :, pl.ds(k0, blk)].astype(BF16), h % 2)
                                   for h in range(heads)],
                              [vt_ref[0, h, :, pl.ds(k0, blk)].astype(BF16) for h in range(heads)],
                              carries, accs, uu, masks)

        state = strips(q0, [jnp.zeros((blk, 1), F32)] * heads, [jnp.zeros((blk, HEAD_DIM), F32)] * heads,
                       [strict])

        def k_block(jj, st):
            carries, accs = strips(pl.multiple_of((i - 1 - jj) * blk, blk), st[0], st[1], [None])
            return tuple(carries), tuple(accs)

        _, accs = lax.fori_loop(0, i, k_block, (tuple(state[0]), tuple(state[1])))
        o_ref[0, pl.ds(q0, blk), :] = jnp.concatenate(list(accs), axis=-1).astype(BF16)
        return 0

    lax.fori_loop(0, t // blk, q_block, 0)


def _sb_prompt(q, kt, vt):
    b, n_heads, _, t = kt.shape
    wl = SB_WINDOW_BLOCKS * SB_BLOCK
    assert t >= wl
    o_shape = jax.ShapeDtypeStruct((b, t, n_heads * HEAD_DIM), BF16)

    def specs(hb):
        q_blk = pl.BlockSpec((1, hb // 2, t, 2 * HEAD_DIM), lambda bi, hg: (bi, hg, 0, 0))
        kv_blk = pl.BlockSpec((1, hb, HEAD_DIM, t), lambda bi, hg: (bi, hg, 0, 0))
        o_blk = pl.BlockSpec((1, t, hb * HEAD_DIM), lambda bi, hg: (bi, 0, hg))
        return (b, n_heads // hb), [q_blk, kv_blk, kv_blk], o_blk

    heads, groups = 4, 4
    rows = SB_OLD_ROWS + 2 * SB_BLOCK
    grid, in_specs, o_blk = specs(heads * groups)
    o_fast, flags = pl.pallas_call(
        functools.partial(_sb_prompt_fast_kernel, heads=heads, groups=groups),
        grid=grid,
        in_specs=in_specs,
        out_specs=[o_blk, pl.BlockSpec((1, 1, 8, 128), lambda bi, hg: (bi, hg, 0, 0))],
        out_shape=[o_shape, jax.ShapeDtypeStruct((b, grid[1], 8, 128), F32)],
        scratch_shapes=[pltpu.VMEM((SB_STAGES - 1, heads, SB_BLOCK, wl), F32),
                        pltpu.VMEM((heads, rows, SB_BLOCK), BF16),
                        pltpu.VMEM((heads, rows, SB_BLOCK), F32),
                        pltpu.VMEM((heads, SB_BLOCK, wl), BF16)],
        compiler_params=_cparams(("parallel", "parallel")),
        name="sb_prompt_fast",
    )(q, kt, vt)

    def full():
        grid, in_specs, o_blk = specs(4)
        return pl.pallas_call(
            functools.partial(_sb_prompt_full_kernel, heads=4),
            grid=grid,
            in_specs=in_specs,
            out_specs=o_blk,
            out_shape=o_shape,
            compiler_params=_cparams(("parallel", "parallel")),
            name="sb_prompt_full",
        )(q, kt, vt)

    return lax.cond(jnp.max(flags) > SB_DEAD, full, lambda: o_fast)


def _sb_sample_kernel(q_ref, ktn_ref, vtn_ref, ktc_ref, vtc_ref, o_ref, *maybe_flag, heads, strip):
    s = q_ref.shape[2]
    width = ktc_ref.shape[4]
    blk = SB_BLOCK
    uu = _suffix_neg_ones(blk)
    uu_new = _suffix_neg_ones(s)
    strict = _strict_mask(s)

    qs = [q_ref[0, h] for h in range(heads)]
    state = _sb_strips(qs, [ktn_ref[0, h].astype(BF16) for h in range(heads)],
                       [vtn_ref[0, h].astype(BF16) for h in range(heads)],
                       [jnp.zeros((s, 1), F32)] * heads, [jnp.zeros((s, HEAD_DIM), F32)] * heads,
                       uu_new, [strict])

    def k_strip(jj, st):
        k0 = pl.multiple_of(width - (jj + 1) * strip * blk, blk)
        carries, accs = _sb_strips(
            qs, [ktc_ref[0, 0, h, :, pl.ds(k0, strip * blk)].astype(BF16) for h in range(heads)],
            [vtc_ref[0, 0, h, :, pl.ds(k0, strip * blk)].astype(BF16) for h in range(heads)],
            st[0], st[1], uu, [None] * strip)
        return tuple(carries), tuple(accs)

    carries, accs = lax.fori_loop(0, width // (strip * blk), k_strip, (tuple(state[0]), tuple(state[1])))
    o_ref[0] = jnp.concatenate(list(accs), axis=-1).astype(BF16)
    if maybe_flag:
        worst = carries[0]
        for h in range(1, heads):
            worst = jnp.maximum(worst, carries[h])
        maybe_flag[0][...] = jnp.broadcast_to(jnp.max(worst), maybe_flag[0].shape)


def _sb_sample(q, kt_new, vt_new, cache_kt, cache_vt, layer):
    b, n_heads, s, _ = q.shape
    past = cache_kt.shape[4]
    width = min(past, SB_SAMPLE_WINDOW)
    assert past % width == 0 and width % SB_BLOCK == 0
    last = past // width - 1
    o_shape = jax.ShapeDtypeStruct((b, s, n_heads * HEAD_DIM), BF16)

    def specs(heads, cache_width, cache_block):
        q_blk = pl.BlockSpec((1, heads, s, HEAD_DIM), lambda bi, hg: (bi, hg, 0, 0))
        new_blk = pl.BlockSpec((1, heads, HEAD_DIM, s), lambda bi, hg: (bi, hg, 0, 0))
        cache_blk = pl.BlockSpec((1, 1, heads, HEAD_DIM, cache_width),
                                 lambda bi, hg: (layer, bi, hg, 0, cache_block))
        o_blk = pl.BlockSpec((1, s, heads * HEAD_DIM), lambda bi, hg: (bi, 0, hg))
        return (b, n_heads // heads), [q_blk, new_blk, new_blk, cache_blk, cache_blk], o_blk

    heads = n_heads
    grid, in_specs, o_blk = specs(heads, width, last)
    o_fast, flags = pl.pallas_call(
        functools.partial(_sb_sample_kernel, heads=heads, strip=width // SB_BLOCK),
        grid=grid,
        in_specs=in_specs,
        out_specs=[o_blk, pl.BlockSpec((1, 1, 8, 128), lambda bi, hg: (bi, hg, 0, 0))],
        out_shape=[o_shape, jax.ShapeDtypeStruct((b, n_heads // heads, 8, 128), F32)],
        compiler_params=_cparams(("parallel", "parallel")),
        name="sb_sample_fast",
    )(q, kt_new, vt_new, cache_kt, cache_vt)
    if width == past:
        return o_fast

    def full():
        heads = 4
        grid, in_specs, o_blk = specs(heads, past, 0)
        return pl.pallas_call(
            functools.partial(_sb_sample_kernel, heads=heads, strip=1),
            grid=grid,
            in_specs=in_specs,
            out_specs=o_blk,
            out_shape=o_shape,
            compiler_params=_cparams(("parallel", "parallel")),
            name="sb_sample_full",
        )(q, kt_new, vt_new, cache_kt, cache_vt)

    return lax.cond(jnp.max(flags) > SB_DEAD, full, lambda: o_fast)


def _route(logits):
    tm = logits.shape[0]
    lane = lax.broadcasted_iota(jnp.int32, (tm, ROUTER_LANES), 1)
    lane_f = lane.astype(F32)
    neg = -jnp.inf
    first = lambda hit: jnp.min(jnp.where(hit, lane_f, float(ROUTER_LANES)), axis=-1, keepdims=True)

    gl = jnp.where(lane < N_GROUPS, logits, neg)
    g_max = jnp.max(gl, axis=-1, keepdims=True)
    g_top = 1.0 / jnp.sum(jnp.exp(gl - g_max), axis=-1, keepdims=True)
    g_idx = first(gl == g_max)

    lo = N_GROUPS + g_idx * EXPERTS_PER_GROUP
    in_group = (lane_f >= lo) & (lane_f < lo + EXPERTS_PER_GROUP)
    sel = jnp.where(in_group, logits, neg)
    t1 = jnp.max(sel, axis=-1, keepdims=True)
    i1 = first(sel == t1)
    sel2 = jnp.where(lane_f == i1, neg, sel)
    t2 = jnp.max(sel2, axis=-1, keepdims=True)
    i2 = first(sel2 == t2)
    e2 = jnp.exp(t2 - t1)
    den = 1.0 + e2
    w1 = (1.0 / den) * g_top
    w2 = (e2 / den) * g_top
    return jnp.where(lane_f == i1, w1, 0.0) + jnp.where(lane_f == i2, w2, 0.0)


CHANNEL_ROW_SPLITS = 2
EXPERTS_PER_TRIP = 16


def _channel_kernel(x_ref, o_ref, p_ref, wo_ref, gffn_ref, wr_ref, wg_ref, wu_ref, wd_ref, gple_ref,
                    wpg_ref, wpp_ref, *rest, n_next):
    next_in, (out_ref, *next_out) = rest[:n_next], rest[n_next:]
    acc_ref = out_ref
    tm = x_ref.shape[0]
    d_expert = wd_ref.shape[2]
    half = tm // CHANNEL_ROW_SPLITS
    rows = [slice(i * half, (i + 1) * half) for i in range(CHANNEL_ROW_SPLITS)]

    h1s = [x_ref[r, :] + jnp.dot(o_ref[r, :], wo_ref[...], preferred_element_type=F32) for r in rows]
    xns = [_rms_unit(h1) * gffn_ref[...] for h1 in h1s]
    xnbs = [xn.astype(BF16) for xn in xns]
    xlos = [(xn - xnb.astype(F32)).astype(BF16) for xn, xnb in zip(xns, xnbs)]
    logits = []
    for xnb, xlo in zip(xnbs, xlos):
        hi_lo = jnp.dot(xnb, wr_ref[...], preferred_element_type=F32)
        lo_hi = jnp.dot(xlo, wr_ref[:, 0:ROUTER_LANES], preferred_element_type=F32)
        logits.append(hi_lo[:, 0:ROUTER_LANES] + hi_lo[:, ROUTER_LANES:] + lo_hi)
    gates = jnp.concatenate([_route(l) for l in logits], axis=0)
    xnb = jnp.concatenate(xnbs, axis=0)
    lane = lax.broadcasted_iota(jnp.int32, gates.shape, 1)
    for r, h1 in zip(rows, h1s):
        acc_ref[r, :] = h1

    def experts(j, _):
        es = [j * EXPERTS_PER_TRIP + k for k in range(EXPERTS_PER_TRIP)]
        gs = [jnp.dot(xnb, wg_ref[0, e], preferred_element_type=F32) for e in es]
        us = [jnp.dot(xnb, wu_ref[0, e], preferred_element_type=F32) for e in es]
        hids = []
        for e, g, u in zip(es, gs, us):
            gate_e = jnp.sum(jnp.where(lane == N_GROUPS + e, gates, 0.0), axis=-1, keepdims=True)
            hids.append(((g * jax.nn.sigmoid(g)) * u * gate_e).astype(BF16))
        wd = wd_ref[0, pl.ds(j * EXPERTS_PER_TRIP, EXPERTS_PER_TRIP)]
        acc_ref[...] += jnp.dot(jnp.concatenate(hids, axis=-1),
                                wd.reshape(EXPERTS_PER_TRIP * d_expert, wd.shape[-1]),
                                preferred_element_type=F32)
        return 0

    lax.fori_loop(0, N_EXPERTS // EXPERTS_PER_TRIP, experts, 0)
    h2s = [acc_ref[r, :] for r in rows]
    x3s = [(_rms_unit(h2) * gple_ref[...]).astype(BF16) for h2 in h2s]
    ple_gates = [jax.nn.sigmoid(jnp.dot(x3, wpg_ref[0], preferred_element_type=F32)) for x3 in x3s]
    projs = [jnp.dot(p_ref[0, r, :].astype(BF16), wpp_ref[0], preferred_element_type=F32) for r in rows]
    h3s = [h2 + proj * gate for h2, proj, gate in zip(h2s, projs, ple_gates)]
    for r, h3 in zip(rows, h3s):
        out_ref[r, :] = h3
    if n_next:
        _store_proj_b(rows, _proj_b_rows(h3s, *next_in), *next_out)


def _channel(x, o, p, layer, w, stacks, next_proj=None, batch=1):
    n, d = x.shape
    tm = min(n, 512)
    row = lambda cols: pl.BlockSpec((tm, cols), lambda i: (i, 0))
    small = [w["wo"], w["gffn"], w["wr"]]
    moe = [stacks["wg"], stacks["wu"], stacks["wd"]]
    ple = [stacks["wpg"], stacks["wpp"]]
    next_proj = list(next_proj or [])
    out_specs = [row(d)]
    out_shape = [jax.ShapeDtypeStruct((n, d), F32)]
    if next_proj:
        t = n // batch
        per_entry = t // tm
        assert t % tm == 0
        n_heads = next_proj[1].shape[1] // HEAD_DIM
        n_kv = next_proj[4].shape[1] // (2 * HEAD_DIM)
        hm = lambda i: (i // per_entry, 0, i % per_entry, 0)
        for heads, dtype in ((n_heads, BF16), (n_kv, F32), (n_kv, F32)):
            out_specs.append(pl.BlockSpec((1, heads, tm, HEAD_DIM), hm))
            out_shape.append(jax.ShapeDtypeStruct((batch, heads, t, HEAD_DIM), dtype))
    outs = pl.pallas_call(
        functools.partial(_channel_kernel, n_next=len(next_proj)),
        grid=(n // tm,),
        in_specs=([row(d), row(d), pl.BlockSpec((1, tm, p.shape[2]), lambda i: (layer, i, 0))]
                  + [_resident(a.shape) for a in small] + [_resident_layer(a.shape, layer) for a in moe]
                  + [_resident(w["gple"].shape)] + [_resident_layer(a.shape, layer) for a in ple]
                  + [_resident(a.shape) for a in next_proj]),
        out_specs=out_specs,
        out_shape=out_shape,
        compiler_params=_cparams(("parallel",)),
        name="channel",
    )(x, o, p, *small, *moe, w["gple"], *ple, *next_proj)
    return outs if next_proj else outs[0]


def _proj_b_rows(hs, ga_ref, wq_ref, gq_ref, gkv_ref, wkv_ref, gk_ref, gsum_ref, gexp_ref):
    n_kv = wkv_ref.shape[1] // (2 * HEAD_DIM)
    ys = [_rms_unit(h) for h in hs]
    qs = [jnp.dot((y * ga_ref[...]).astype(BF16), wq_ref[...], preferred_element_type=F32) for y in ys]
    kvs = [jnp.dot((y * gkv_ref[...]).astype(BF16), wkv_ref[...], preferred_element_type=F32) for y in ys]
    mss = [jnp.dot((q * q).astype(BF16), gsum_ref[...], preferred_element_type=F32) * (1.0 / HEAD_DIM)
           for q in qs]
    inv_fulls = [jnp.dot(_split_hl(lax.rsqrt(ms + EPS)), gexp_ref[...], preferred_element_type=F32)
                 for ms in mss]
    out = []
    for q, inv_full, kv in zip(qs, inv_fulls, kvs):
        qn = (q * inv_full * gq_ref[...]) * (HEAD_DIM ** -0.5)
        ks = [_rms_unit(kv[:, h * HEAD_DIM:(h + 1) * HEAD_DIM]) * gk_ref[...] for h in range(n_kv)]
        vs = [kv[:, (n_kv + h) * HEAD_DIM:(n_kv + h + 1) * HEAD_DIM] for h in range(n_kv)]
        out.append((qn, ks, vs))
    return out


def _store_proj_b(rows, projected, q_ref, k_ref, v_ref):
    for r, (qn, ks, vs) in zip(rows, projected):
        for h in range(q_ref.shape[1]):
            q_ref[0, h, r, :] = qn[:, h * HEAD_DIM:(h + 1) * HEAD_DIM].astype(BF16)
        for h in range(k_ref.shape[1]):
            k_ref[0, h, r, :] = ks[h]
            v_ref[0, h, r, :] = vs[h]


def _t5_bucket(rel):
    nb = NUM_BUCKETS // 2
    max_exact = nb // 2
    n = jnp.abs(rel)
    large = max_exact + (jnp.log(jnp.maximum(n, 1).astype(jnp.float32) / max_exact)
                         / math.log(MAX_DISTANCE / max_exact) * (nb - max_exact)).astype(jnp.int32)
    large = jnp.minimum(large, nb - 1)
    return jnp.where(rel > 0, nb, 0) + jnp.where(n < max_exact, n, large)


def _bias_kernel(bucket_ref, rbt_ref, out_ref):
    bucket = bucket_ref[...].astype(F32)
    guard = 2.0 ** -10
    acc = jnp.zeros(out_ref.shape, F32)
    for b in range(NUM_BUCKETS):
        hit = (bucket >= b - guard) & (bucket < b + 1 - guard)
        acc = acc + jnp.where(hit, rbt_ref[:, b:b + 1], 0.0)
    out_ref[...] = jnp.where(bucket < 0, -jnp.inf, acc)


def _bias_table(rel_bias, nq, nk, key_offset, chunk_window):
    qi = jnp.arange(nq, dtype=jnp.int32)[:, None]
    kj = jnp.arange(nk, dtype=jnp.int32)[None, :] - key_offset
    bucket = _t5_bucket(kj - qi)
    if chunk_window:
        qc, kc = qi // CHUNK, kj // CHUNK
        bucket = jnp.where((kc <= qc) & (kc >= qc - WIN_CHUNKS), bucket, -1)
    bucket = bucket.reshape(1, nq * nk)
    n_heads = rel_bias.shape[1]
    out = pl.pallas_call(
        _bias_kernel,
        out_shape=jax.ShapeDtypeStruct((n_heads, nq * nk), F32),
        name="bias_table",
    )(bucket, rel_bias.T)
    return out.reshape(n_heads, nq, nk)


SWA_ROWS_PER_TRIP = 128


def _swa_kernel(sink_ref, q_ref, k_ref, v_ref, bias_ref, o_ref, kpad_ref, vpad_ref, *, cq, wl, pad, n_chunks,
                group):
    n_kv, tk = k_ref.shape[1], k_ref.shape[2]
    if pad:
        kpad_ref[:, 0:pad, :] = jnp.zeros((n_kv, pad, HEAD_DIM), BF16)
        vpad_ref[:, 0:pad, :] = jnp.zeros((n_kv, pad, HEAD_DIM), BF16)
    kpad_ref[:, pad:pad + tk, :] = k_ref[0].astype(BF16)
    vpad_ref[:, pad:pad + tk, :] = v_ref[0].astype(BF16)

    per_trip = max(1, SWA_ROWS_PER_TRIP // cq)
    per_trip = per_trip if n_chunks % per_trip == 0 else 1
    n_heads = n_kv * group

    def chunks(c, masked):
        r0s = [pl.multiple_of((c * per_trip + i) * cq, cq) for i in range(per_trip)]
        logits = [[lax.dot_general(
            q_ref[0, kv * group:(kv + 1) * group, pl.ds(r0, cq), :].reshape(group * cq, HEAD_DIM),
            kpad_ref[kv, pl.ds(r0, wl), :], _NT, preferred_element_type=F32)
            for kv in range(n_kv)] for r0 in r0s]
        es, dens = [], []
        for r0, lg in zip(r0s, logits):
            valid = lax.broadcasted_iota(jnp.int32, (cq, wl), 1) + r0 >= pad
            for h in range(n_heads):
                l = lg[h // group][(h % group) * cq:(h % group + 1) * cq] + bias_ref[h]
                if masked:
                    l = jnp.where(valid, l, -jnp.inf)
                sink = sink_ref[h]
                m = jnp.maximum(jnp.max(l, axis=-1, keepdims=True), sink)
                e = jnp.exp(l - m)
                dens.append(jnp.sum(e, axis=-1, keepdims=True) + jnp.exp(sink - m))
                es.append(e.astype(BF16))
        pvs = [[jnp.dot(jnp.concatenate(es[i * n_heads + kv * group:i * n_heads + (kv + 1) * group], axis=0),
                        vpad_ref[kv, pl.ds(r0, wl), :], preferred_element_type=F32)
                for kv in range(n_kv)] for i, r0 in enumerate(r0s)]
        for i, r0 in enumerate(r0s):
            outs = [pvs[i][h // group][(h % group) * cq:(h % group + 1) * cq] / dens[i * n_heads + h]
                    for h in range(n_heads)]
            o_ref[0, pl.ds(r0, cq), :] = jnp.concatenate(outs, axis=-1).astype(BF16)
        return 0

    trips = n_chunks // per_trip
    masked_trips = min(trips, -(-pad // (cq * per_trip)))
    for c in range(masked_trips):
        chunks(c, True)
    lax.fori_loop(masked_trips, trips, lambda c, _: chunks(c, False), 0)


def _swa(q, k_win, v_win, bias, sinks, *, cq, wl, pad):
    b, n_heads, tq, _ = q.shape
    n_kv, tk = k_win.shape[1], k_win.shape[2]
    n_chunks = tq // cq
    assert (n_chunks - 1) * cq + wl == pad + tk
    kv_blk = pl.BlockSpec((1, n_kv, tk, HEAD_DIM), lambda bi: (bi, 0, 0, 0))
    return pl.pallas_call(
        functools.partial(_swa_kernel, cq=cq, wl=wl, pad=pad, n_chunks=n_chunks, group=n_heads // n_kv),
        grid=(b,),
        in_specs=[pl.BlockSpec(memory_space=pltpu.SMEM),
                  pl.BlockSpec((1, n_heads, tq, HEAD_DIM), lambda bi: (bi, 0, 0, 0)),
                  kv_blk, kv_blk, _resident(bias.shape)],
        out_specs=pl.BlockSpec((1, tq, n_heads * HEAD_DIM), lambda bi: (bi, 0, 0)),
        out_shape=jax.ShapeDtypeStruct((b, tq, n_heads * HEAD_DIM), BF16),
        scratch_shapes=[pltpu.VMEM((n_kv, pad + tk, HEAD_DIM), BF16),
                        pltpu.VMEM((n_kv, pad + tk, HEAD_DIM), BF16)],
        compiler_params=_cparams(("parallel",)),
        name="swa",
    )(sinks, q, k_win, v_win, bias)


def _row(v):
    return v.reshape(1, -1).astype(F32)


def _router_hi_lo(wr):
    hi = wr.astype(BF16)
    lo = (wr - hi.astype(F32)).astype(BF16)
    return jnp.concatenate([hi, lo], axis=1)


def _prep_weights(prm):
    d = prm["a_w_o"].shape[1]
    depth = prm["norm_ffn"].shape[0]
    n_a = prm["a_w_qkv"].shape[0]
    scale = HEAD_DIM ** -0.5
    w = {"channel": [], "n_a": n_a, "depth": depth}
    for i in range(depth):
        wo = prm["a_w_o"][i] if i < n_a else prm["b_w_o"][i - n_a]
        pad = ROUTER_LANES - N_GROUPS - N_EXPERTS
        wr = jnp.concatenate([prm["moe_w_group"][i], prm["moe_w_router"][i].reshape(d, N_EXPERTS),
                              jnp.zeros((d, pad), F32)], axis=1)
        w["channel"].append({"wo": wo.astype(BF16), "gffn": _row(prm["norm_ffn"][i]),
                             "wr": _router_hi_lo(wr), "gple": _row(prm["norm_ple"][i])})
    w["stacks"] = {"wg": prm["moe_w_gate"].astype(BF16), "wu": prm["moe_w_up"].astype(BF16),
                   "wd": prm["moe_w_down"].astype(BF16), "wpg": prm["ple_w_gate"].astype(BF16),
                   "wpp": prm["ple_w_proj"].astype(BF16)}
    w["qkv"] = []
    for i in range(n_a):
        wq = prm["a_w_qkv"][i]
        hd = wq.shape[1] // 3
        w["qkv"].append(((wq[:, :hd] * scale).astype(BF16),
                         wq[:, hd:].T.astype(BF16)))
    n_heads = prm["b_w_q"].shape[2] // HEAD_DIM
    head_of = jnp.arange(n_heads * HEAD_DIM, dtype=jnp.int32) // HEAD_DIM
    lanes = jnp.arange(ROUTER_LANES, dtype=jnp.int32)
    member = (head_of[:, None] == lanes[None, :]).astype(BF16)
    w["gsum"] = member
    w["gexp"] = jnp.concatenate([member.T, member.T], axis=0)
    w["wq_b"] = [prm["b_w_q"][j].astype(BF16) for j in range(depth - n_a)]
    w["gq_b"] = [_row(jnp.tile(prm["b_q_norm"][j], n_heads)) for j in range(depth - n_a)]
    w["wkv"] = prm["b_w_kv"].astype(BF16)
    return w


def _assert_sample_window_visible(past_len, s, tk):
    q_chunk = [(past_len + i) // CHUNK for i in range(s)]
    k_pos = [past_len + s - tk + j for j in range(tk)]
    ok = all(kp >= 0 and qc - WIN_CHUNKS <= kp // CHUNK <= qc for qc in q_chunk for kp in k_pos)
    if not ok:
        raise NotImplementedError("sample window with masked keys")


def _fold(x):
    return x.reshape(1, -1, x.shape[-1])


def _unfold_rows(a, b):
    _, n_heads, rows, dh = a.shape
    return a.reshape(n_heads, b, rows // b, dh).transpose(1, 0, 2, 3)


def _unfold_cols(a, b):
    _, n_heads, dh, rows = a.shape
    return a.reshape(n_heads, dh, b, rows // b).transpose(2, 0, 1, 3)


def _run_trunk(x, p, prm, w, sb_cache_k=None, sb_cache_v=None, swa_cache_k=None, swa_cache_v=None):
    bx, t, d = x.shape
    n_a, depth = w["n_a"], w["depth"]
    sample = sb_cache_k is not None
    h = x
    sb_k, sb_v = [], []
    k_win = v_win = None
    q_b = None
    for i in range(depth):
        if i < n_a:
            if sample:
                q, kt, vt, _, _ = _proj_a(_fold(h), _row(prm["norm_attn"][i]), *w["qkv"][i])
                q, kt, vt = _unfold_rows(q, bx), _unfold_cols(kt, bx), _unfold_cols(vt, bx)
            else:
                q, kt, vt, kt_bf, vt_bf = _proj_a(h, _row(prm["norm_attn"][i]), *w["qkv"][i], pack_q=True)
            sb_k.append(kt)
            sb_v.append(vt)
            if sample:
                o = _sb_sample(q, kt, vt, jnp.swapaxes(sb_cache_k, -1, -2), jnp.swapaxes(sb_cache_v, -1, -2), i)
            else:
                o = _sb_prompt(q, kt_bf, vt_bf)
        else:
            j = i - n_a
            if j > 0:
                raise NotImplementedError("one B layer supported")
            if sample:
                tk = k_win.shape[2]
                _assert_sample_window_visible(sb_cache_k.shape[3], t, tk)
                o = _swa(q_b, k_win, v_win, _bias_table(prm["rel_bias"], t, tk, tk - t, False),
                         prm["b_sinks"][j], cq=t, wl=tk, pad=0)
            else:
                cq = 2 * CHUNK
                wl = WINDOW + cq
                assert t % cq == 0
                o = _swa(q_b, k_win, v_win, _bias_table(prm["rel_bias"], cq, wl, WINDOW, True),
                         prm["b_sinks"][j], cq=cq, wl=wl, pad=WINDOW)
        flat = (h.reshape(bx * t, d), o.reshape(bx * t, d), p.reshape(depth, bx * t, -1), i,
                w["channel"][i], w["stacks"])
        if i != n_a - 1:
            h = _channel(*flat).reshape(bx, t, d)
        else:
            next_proj = [_row(prm["norm_attn"][n_a]), w["wq_b"][0], w["gq_b"][0], _row(prm["kv_norm"]), w["wkv"],
                         _row(prm["b_k_norm"]), w["gsum"], w["gexp"]]
            h, q_b, k_s, v_s = _channel(*flat, next_proj=next_proj, batch=1 if sample else bx)
            h = h.reshape(bx, t, d)
            if sample:
                q_b, k_s, v_s = _unfold_rows(q_b, bx), _unfold_rows(k_s, bx), _unfold_rows(v_s, bx)
                k_win = jnp.concatenate([swa_cache_k, k_s], axis=2)
                v_win = jnp.concatenate([swa_cache_v, v_s], axis=2)
            else:
                k_win, v_win = k_s, v_s
    sb_k = jnp.swapaxes(jnp.stack(sb_k), -1, -2)
    sb_v = jnp.swapaxes(jnp.stack(sb_v), -1, -2)
    return h, sb_k, sb_v, k_win[:, :, -WINDOW:], v_win[:, :, -WINDOW:]


def kernel(x_prompt, x_sample, p_prompt, p_sample, cache_sb_k, cache_sb_v, cache_swa_k, cache_swa_v, norm_attn, norm_ffn, norm_ple, a_w_qkv, a_w_o, kv_norm, b_w_kv, b_k_norm, b_w_q, b_q_norm, b_sinks, b_w_o, rel_bias, moe_w_group, moe_w_router, moe_w_gate, moe_w_up, moe_w_down, ple_w_proj, ple_w_gate):
    prm = {
        "norm_attn": norm_attn, "norm_ffn": norm_ffn, "norm_ple": norm_ple,
        "a_w_qkv": a_w_qkv, "a_w_o": a_w_o, "kv_norm": kv_norm, "b_w_kv": b_w_kv, "b_k_norm": b_k_norm,
        "b_w_q": b_w_q, "b_q_norm": b_q_norm, "b_sinks": b_sinks, "b_w_o": b_w_o, "rel_bias": rel_bias,
        "moe_w_group": moe_w_group, "moe_w_router": moe_w_router, "moe_w_gate": moe_w_gate,
        "moe_w_up": moe_w_up, "moe_w_down": moe_w_down, "ple_w_proj": ple_w_proj, "ple_w_gate": ple_w_gate,
    }
    w = _prep_weights(prm)
    y_p, sb_k_p, sb_v_p, swa_k_p, swa_v_p = _run_trunk(x_prompt, p_prompt, prm, w)
    y_s, sb_k_s, sb_v_s, swa_k_s, swa_v_s = _run_trunk(x_sample, p_sample, prm, w, cache_sb_k, cache_sb_v,
                                                       cache_swa_k, cache_swa_v)
    return (y_p, y_s, sb_k_p, sb_v_p, swa_k_p, swa_v_p, sb_k_s, sb_v_s, swa_k_s, swa_v_s)
```
